```python
import math
import jax, jax.numpy as jnp
from jax import lax
import numpy as np

D_MODEL = 1024
BATCH = 4
SEQ = 8192
DEPTH = 4
DEC_BATCH = 8
DEC_SEQ = 32
PAST_LEN = 2048

CHUNK = 64
N_META = 16
N_MIXERS = 3
N_POOL_LAYERS = (DEPTH + 2) // 3
N_SWA_LAYERS = (DEPTH + 1) // 3
N_CONV_LAYERS = DEPTH // 3
POOL_WINDOWS = (2, 4, 8, 16)
N_POOL_GROUPS = 4
POOL_GROUP = D_MODEL // N_POOL_GROUPS
POOL_STATE = max(POOL_WINDOWS) - 1
HEAD_DIM = 64
N_HEADS = D_MODEL // HEAD_DIM
N_KV_HEADS = 4
GQA = N_HEADS // N_KV_HEADS
Q_DIM = N_HEADS * HEAD_DIM
KV_DIM = N_KV_HEADS * HEAD_DIM
WINDOW = 128
WIN_CHUNKS = WINDOW // CHUNK
ROPE_THETA = 10000.0
CONV_WIDTH = 3
N_GROUPS = 4
EXPERTS_PER_GROUP = 4
N_EXPERTS = N_GROUPS * EXPERTS_PER_GROUP
D_EXPERT = 256
TOP_K = 2
EPS = 1e-6

kernel_name = 'hybrid_stream_pool_swa_conv_hmoe'


def rms_norm(x, g):
    xf = x.astype(jnp.float32)
    y = xf * lax.rsqrt(jnp.mean(xf * xf, axis=-1, keepdims=True) + EPS)
    return (y * g.astype(jnp.float32)).astype(x.dtype)


def rope(x, pos):
    half = HEAD_DIM // 2
    inv = ROPE_THETA ** (-jnp.arange(half, dtype=jnp.float32) / half)
    ang = pos.astype(jnp.float32)[:, None] * inv[None, :]
    cos = jnp.cos(ang)[:, None, :]
    sin = jnp.sin(ang)[:, None, :]
    xf = x.astype(jnp.float32)
    x1, x2 = xf[..., :half], xf[..., half:]
    return jnp.concatenate([x1 * cos - x2 * sin, x2 * cos + x1 * sin], axis=-1).astype(x.dtype)


def pool_mix(h, hist, has_history, w_pool, pool_scale):
    B, S, D = h.shape
    P = POOL_STATE
    u_all = jnp.concatenate([hist, h], axis=1)
    u = u_all.astype(jnp.float32)
    cs = jnp.concatenate([jnp.zeros((B, 1, D), jnp.float32), jnp.cumsum(u, axis=1)], axis=1)
    diffs = []
    for gi, w in enumerate(POOL_WINDOWS):
        c0, c1 = gi * POOL_GROUP, (gi + 1) * POOL_GROUP
        win_sum = cs[:, P + 1:P + 1 + S, c0:c1] - cs[:, P + 1 - w:P + 1 - w + S, c0:c1]
        if has_history:
            count = jnp.full((S,), w, jnp.float32)
        else:
            count = jnp.minimum(jnp.arange(1, S + 1), w).astype(jnp.float32)
        diffs.append(win_sum / count[None, :, None] - u[:, P:, c0:c1])
    d = jnp.stack(diffs, axis=2).astype(h.dtype)
    y = jnp.einsum('bsgc,gce->bsge', d, w_pool).reshape(B, S, D) * pool_scale
    return y, u_all[:, -P:]


def conv_mix(h, hist, w_in, conv_w, w_out):
    S = h.shape[1]
    gate_b, gate_c, v = jnp.split(h @ w_in, 3, axis=-1)
    u = gate_c * v
    up = jnp.concatenate([hist, u], axis=1)
    acc = up[:, 0:S] * conv_w[0]
    for k in range(1, CONV_WIDTH):
        acc = acc + up[:, k:k + S] * conv_w[k]
    return (gate_b * acc) @ w_out, up[:, -(CONV_WIDTH - 1):]


def qkv_proj(h, w_qkv, pos):
    B, S, _ = h.shape
    z = h @ w_qkv
    q = rope(z[..., :Q_DIM].reshape(B, S, N_HEADS, HEAD_DIM), pos).reshape(B, S, N_KV_HEADS, GQA, HEAD_DIM)
    k = rope(z[..., Q_DIM:Q_DIM + KV_DIM].reshape(B, S, N_KV_HEADS, HEAD_DIM), pos)
    v = z[..., Q_DIM + KV_DIM:].reshape(B, S, N_KV_HEADS, HEAD_DIM)
    return q, k, v


def sink_attention(q, k, v, mask, sinks):
    s = jnp.einsum('bnqhgd,bnkhd->bnhgqk', q.astype(jnp.float32), k.astype(jnp.float32)) * (HEAD_DIM ** -0.5)
    s = jnp.where(mask[None, :, None, None, None, :], s, -jnp.inf)
    sink = jnp.broadcast_to(sinks.astype(jnp.float32).reshape(1, 1, N_KV_HEADS, GQA, 1, 1), s.shape[:-1] + (1,))
    p = jax.nn.softmax(jnp.concatenate([s, sink], axis=-1), axis=-1)[..., :-1]
    return jnp.einsum('bnhgqk,bnkhd->bnqhgd', p.astype(v.dtype), v)


def chunk_band(t, nc):
    B = t.shape[0]
    tp = jnp.pad(t, ((0, 0), (WIN_CHUNKS * CHUNK, 0), (0, 0), (0, 0)))
    tp = tp.reshape((B, nc + WIN_CHUNKS, CHUNK) + t.shape[2:])
    return jnp.concatenate([tp[:, i:i + nc] for i in range(WIN_CHUNKS + 1)], axis=2)


def swa_prompt(h, w_qkv, w_o, sinks):
    B, L, _ = h.shape
    S = L - N_META
    nc = S // CHUNK
    q, k, v = qkv_proj(h, w_qkv, jnp.arange(L))
    km, vm = k[:, :N_META], v[:, :N_META]
    om = sink_attention(q[:, None, :N_META], km[:, None], vm[:, None], jnp.ones((1, N_META), bool), sinks)[:, 0]
    qf = q[:, N_META:].reshape(B, nc, CHUNK, N_KV_HEADS, GQA, HEAD_DIM)
    kb = jnp.concatenate([jnp.broadcast_to(km[:, None], (B, nc) + km.shape[1:]), chunk_band(k[:, N_META:], nc)], axis=2)
    vb = jnp.concatenate([jnp.broadcast_to(vm[:, None], (B, nc) + vm.shape[1:]), chunk_band(v[:, N_META:], nc)], axis=2)
    band_pos = (jnp.arange(nc)[:, None] * CHUNK - WIN_CHUNKS * CHUNK
                + jnp.arange((WIN_CHUNKS + 1) * CHUNK)[None, :])
    mask = jnp.concatenate([jnp.ones((nc, N_META), bool), band_pos >= 0], axis=1)
    of = sink_attention(qf, kb, vb, mask, sinks).reshape(B, S, N_KV_HEADS, GQA, HEAD_DIM)
    o = jnp.concatenate([om, of], axis=1).reshape(B, L, Q_DIM)
    kv_win = jnp.stack([k[:, -WINDOW:], v[:, -WINDOW:]], axis=2)
    kv_meta = jnp.stack([km, vm], axis=2)
    return o @ w_o, kv_win, kv_meta


def swa_sample(h, cache_kv, meta_kv, w_qkv, w_o, sinks):
    B, S, _ = h.shape
    q, k, v = qkv_proj(h, w_qkv, PAST_LEN + N_META + jnp.arange(S))
    keys = jnp.concatenate([meta_kv[:, :, 0], cache_kv[:, :, 0], k], axis=1)
    vals = jnp.concatenate([meta_kv[:, :, 1], cache_kv[:, :, 1], v], axis=1)
    mask = jnp.ones((1, keys.shape[1]), bool)
    o = sink_attention(q[:, None], keys[:, None], vals[:, None], mask, sinks)[:, 0].reshape(B, S, Q_DIM)
    return o @ w_o, jnp.stack([k, v], axis=2)


def hier_moe(h, w_group, b_group, w_router, b_router, w_gate, w_up, w_down):
    hf = h.astype(jnp.float32)
    g_logits = hf @ w_group.astype(jnp.float32) + b_group.astype(jnp.float32)
    g_idx = jnp.argmax(g_logits, axis=-1)
    g_w = jnp.max(jax.nn.softmax(g_logits, axis=-1), axis=-1, keepdims=True)
    e_logits = (hf @ w_router.astype(jnp.float32) + b_router.astype(jnp.float32)).reshape(
        h.shape[:-1] + (N_GROUPS, EXPERTS_PER_GROUP))
    e_sel = jnp.sum(e_logits * jax.nn.one_hot(g_idx, N_GROUPS, dtype=jnp.float32)[..., None], axis=-2)
    top_v, top_i = lax.top_k(e_sel, TOP_K)
    top_w = jax.nn.softmax(top_v, axis=-1) * g_w
    eid = g_idx[..., None] * EXPERTS_PER_GROUP + top_i
    combine = jnp.einsum('bsk,bske->bse', top_w, jax.nn.one_hot(eid, N_EXPERTS, dtype=jnp.float32))
    gate = jnp.einsum('bsd,edf->bsef', h, w_gate)
    up = jnp.einsum('bsd,edf->bsef', h, w_up)
    act = jax.nn.silu(gate) * up * combine[..., None].astype(h.dtype)
    return jnp.einsum('bsef,efd->bsd', act, w_down)


def setup_inputs(seed: int = 0) -> dict:
    key = jax.random.key(seed)
    ks = jax.random.split(key, 32)
    f32 = jnp.float32
    nrm = lambda k, shape, s: jax.random.normal(k, shape, f32) * s
    win_rows = min(WINDOW, PAST_LEN)
    return {
        'x_prompt': nrm(ks[0], (BATCH, SEQ, D_MODEL), 1.0),
        'x_sample': nrm(ks[1], (DEC_BATCH, DEC_SEQ, D_MODEL), 1.0),
        'state_pool': nrm(ks[2], (N_POOL_LAYERS, DEC_BATCH, POOL_STATE, D_MODEL), 1.0),
        'cache_swa_kv': nrm(ks[3], (N_SWA_LAYERS, DEC_BATCH, win_rows, 2, N_KV_HEADS, HEAD_DIM), 1.0),
        'cache_meta_kv': nrm(ks[4], (N_SWA_LAYERS, DEC_BATCH, N_META, 2, N_KV_HEADS, HEAD_DIM), 1.0),
        'state_conv': nrm(ks[5], (N_CONV_LAYERS, DEC_BATCH, CONV_WIDTH - 1, D_MODEL), 1.0),
        'meta_tokens': nrm(ks[6], (N_META, D_MODEL), 1.0),
        'norm_mix': 1.0 + nrm(ks[7], (DEPTH, D_MODEL), 0.01),
        'norm_ffn': 1.0 + nrm(ks[8], (DEPTH, D_MODEL), 0.01),
        'norm_final': 1.0 + nrm(ks[9], (D_MODEL,), 0.01),
        'w_pool': nrm(ks[10], (N_POOL_LAYERS, N_POOL_GROUPS, POOL_GROUP, POOL_GROUP), POOL_GROUP ** -0.5),
        'pool_scale': 1.0 + nrm(ks[11], (N_POOL_LAYERS, D_MODEL), 0.05),
        'w_qkv': nrm(ks[12], (N_SWA_LAYERS, D_MODEL, Q_DIM + 2 * KV_DIM), D_MODEL ** -0.5),
        'w_o': nrm(ks[13], (N_SWA_LAYERS, Q_DIM, D_MODEL), Q_DIM ** -0.5),
        'attn_sinks': nrm(ks[14], (N_SWA_LAYERS, N_HEADS), 0.5),
        'w_conv_in': nrm(ks[15], (N_CONV_LAYERS, D_MODEL, 3 * D_MODEL), D_MODEL ** -0.5),
        'conv_w': nrm(ks[16], (N_CONV_LAYERS, CONV_WIDTH, D_MODEL), CONV_WIDTH ** -0.5),
        'w_conv_out': nrm(ks[17], (N_CONV_LAYERS, D_MODEL, D_MODEL), D_MODEL ** -0.5),
        'w_group': nrm(ks[18], (DEPTH, D_MODEL, N_GROUPS), D_MODEL ** -0.5),
        'b_group': nrm(ks[19], (DEPTH, N_GROUPS), 0.01),
        'w_expert_router': nrm(ks[20], (DEPTH, D_MODEL, N_EXPERTS), D_MODEL ** -0.5),
        'b_expert_router': nrm(ks[21], (DEPTH, N_EXPERTS), 0.01),
        'w_gate': nrm(ks[22], (DEPTH, N_EXPERTS, D_MODEL, D_EXPERT), D_MODEL ** -0.5),
        'w_up': nrm(ks[23], (DEPTH, N_EXPERTS, D_MODEL, D_EXPERT), D_MODEL ** -0.5),
        'w_down': nrm(ks[24], (DEPTH, N_EXPERTS, D_EXPERT, D_MODEL), D_EXPERT ** -0.5),
    }


def reference(x_prompt, x_sample, state_pool, cache_swa_kv, cache_meta_kv, state_conv, meta_tokens,
              norm_mix, norm_ffn, norm_final, w_pool, pool_scale, w_qkv, w_o, attn_sinks,
              w_conv_in, conv_w, w_conv_out, w_group, b_group, w_expert_router, b_expert_router,
              w_gate, w_up, w_down):
    B = x_prompt.shape[0]
    meta = jnp.broadcast_to(meta_tokens[None].astype(x_prompt.dtype), (B, N_META, D_MODEL))
    hp = jnp.concatenate([meta, x_prompt], axis=1)
    pool_p, swa_p, meta_p, conv_p = [], [], [], []
    for i in range(DEPTH):
        j = i // N_MIXERS
        hn = rms_norm(hp, norm_mix[i])
        if i % N_MIXERS == 0:
            y, st = pool_mix(hn, jnp.zeros((B, POOL_STATE, D_MODEL), hn.dtype), False, w_pool[j], pool_scale[j])
            pool_p.append(st)
        elif i % N_MIXERS == 1:
            y, kv_win, kv_meta = swa_prompt(hn, w_qkv[j], w_o[j], attn_sinks[j])
            swa_p.append(kv_win)
            meta_p.append(kv_meta)
        else:
            y, st = conv_mix(hn, jnp.zeros((B, CONV_WIDTH - 1, D_MODEL), hn.dtype), w_conv_in[j], conv_w[j], w_conv_out[j])
            conv_p.append(st)
        hp = hp + y
        hp = hp + hier_moe(rms_norm(hp, norm_ffn[i]), w_group[i], b_group[i], w_expert_router[i],
                           b_expert_router[i], w_gate[i], w_up[i], w_down[i])
    y_prompt = rms_norm(hp, norm_final)[:, N_META:]

    hs = x_sample
    pool_s, swa_s, conv_s = [], [], []
    for i in range(DEPTH):
        j = i // N_MIXERS
        hn = rms_norm(hs, norm_mix[i])
        if i % N_MIXERS == 0:
            y, st = pool_mix(hn, state_pool[j].astype(hn.dtype), True, w_pool[j], pool_scale[j])
            pool_s.append(st)
        elif i % N_MIXERS == 1:
            y, rows = swa_sample(hn, cache_swa_kv[j], cache_meta_kv[j], w_qkv[j], w_o[j], attn_sinks[j])
            swa_s.append(rows)
        else:
            y, st = conv_mix(hn, state_conv[j].astype(hn.dtype), w_conv_in[j], conv_w[j], w_conv_out[j])
            conv_s.append(st)
        hs = hs + y
        hs = hs + hier_moe(rms_norm(hs, norm_ffn[i]), w_group[i], b_group[i], w_expert_router[i],
                           b_expert_router[i], w_gate[i], w_up[i], w_down[i])
    y_sample = rms_norm(hs, norm_final)

    return (y_prompt, y_sample, jnp.stack(pool_p), jnp.stack(swa_p), jnp.stack(meta_p), jnp.stack(conv_p),
            jnp.stack(pool_s), jnp.stack(swa_s), jnp.stack(conv_s))
```

```python
import functools

import jax
import jax.numpy as jnp
from jax import lax
from jax.experimental import pallas as pl
from jax.experimental.pallas import tpu as pltpu

F32 = jnp.float32
BF16 = jnp.bfloat16

CHUNK = 64
N_META = 16
N_MIXERS = 3
POOL_WINDOWS = (2, 4, 8, 16)
POOL_STATE = max(POOL_WINDOWS) - 1
HEAD_DIM = 64
N_KV_HEADS = 4
WINDOW = 128
WIN_CHUNKS = WINDOW // CHUNK
ROPE_THETA = 10000.0
CONV_WIDTH = 3
N_GROUPS = 4
EXPERTS_PER_GROUP = 4
N_EXPERTS = N_GROUPS * EXPERTS_PER_GROUP
PAST_LEN = 2048
EPS = 1e-6

LANES = 128
SUBLANES = 8
VMEM_LIMIT = 48 * 1024 * 1024
NEG_INF = float("-inf")


def _params(*sem):
    return pltpu.CompilerParams(dimension_semantics=sem, vmem_limit_bytes=VMEM_LIMIT)


def _mm(a, b, precise):
    if precise:
        return jnp.dot(a.astype(F32), b.astype(F32), preferred_element_type=F32,
                       precision=lax.Precision.HIGHEST)
    return jnp.dot(a.astype(BF16), b.astype(BF16), preferred_element_type=F32)


def _mm_nt(a, b, precise):
    dn = (((1,), (1,)), ((), ()))
    if precise:
        return lax.dot_general(a.astype(F32), b.astype(F32), dn, preferred_element_type=F32,
                               precision=lax.Precision.HIGHEST)
    return lax.dot_general(a.astype(BF16), b.astype(BF16), dn, preferred_element_type=F32)


def _rms(x, g):
    ms = jnp.mean(x * x, axis=-1, keepdims=True)
    return x * lax.rsqrt(ms + EPS) * g


def _pool_kernel(h_ref, hist_ref, g_ref, w_ref, scale_ref, out_ref, state_ref, buf_ref, *,
                 tm, has_history, precise):
    t = pl.program_id(1)
    halo = POOL_STATE + 1
    pg = h_ref.shape[-1] // len(POOL_WINDOWS)

    @pl.when(t == 0)
    def _():
        buf_ref[0:halo, :] = hist_ref[0]

    h = h_ref[0]
    hn = _rms(h, g_ref[...])
    buf_ref[halo:halo + tm, :] = hn
    if not has_history:
        pos = t * tm + lax.broadcasted_iota(jnp.int32, (tm, 1), 0)
    ys = []
    for gi, w in enumerate(POOL_WINDOWS):
        c0, c1 = gi * pg, (gi + 1) * pg
        win = buf_ref[halo:halo + tm, c0:c1]
        for k in range(1, w):
            win = win + buf_ref[halo - k:halo - k + tm, c0:c1]
        if has_history:
            mean = win * (1.0 / w)
        else:
            mean = win / jnp.minimum(pos + 1, w).astype(F32)
        ys.append(_mm(mean - hn[:, c0:c1], w_ref[gi], precise))
    y = jnp.concatenate(ys, axis=1) * scale_ref[...]
    out_ref[0] = h + y
    tail = buf_ref[tm:tm + halo, :]
    buf_ref[0:halo, :] = tail

    @pl.when(t == pl.num_programs(1) - 1)
    def _():
        state_ref[0] = tail


def _pool_mix(h, hist, hist_per_batch, g, w, scale, *, tm, has_history, precise):
    nb, s, d = h.shape
    halo = POOL_STATE + 1
    pg = d // len(POOL_WINDOWS)
    hist_map = (lambda b, t: (b, 0, 0)) if hist_per_batch else (lambda b, t: (0, 0, 0))
    out, state = pl.pallas_call(
        functools.partial(_pool_kernel, tm=tm, has_history=has_history, precise=precise),
        out_shape=(jax.ShapeDtypeStruct((nb, s, d), F32), jax.ShapeDtypeStruct((nb, halo, d), F32)),
        grid=(nb, s // tm),
        in_specs=[
            pl.BlockSpec((1, tm, d), lambda b, t: (b, t, 0)),
            pl.BlockSpec((1, halo, d), hist_map),
            pl.BlockSpec((1, d), lambda b, t: (0, 0)),
            pl.BlockSpec((len(POOL_WINDOWS), pg, pg), lambda b, t: (0, 0, 0)),
            pl.BlockSpec((1, d), lambda b, t: (0, 0)),
        ],
        out_specs=(pl.BlockSpec((1, tm, d), lambda b, t: (b, t, 0)),
                   pl.BlockSpec((1, halo, d), lambda b, t: (b, 0, 0))),
        scratch_shapes=[pltpu.VMEM((tm + halo, d), F32)],
        compiler_params=_params("arbitrary", "arbitrary"),
        name="pool_mix",
    )(h, hist, g, w, scale)
    return out, state


def _conv_kernel(h_ref, hist_ref, g_ref, win_ref, cw_ref, wout_ref, out_ref, state_ref, buf_ref, *,
                 tm, precise):
    t = pl.program_id(1)
    d = h_ref.shape[-1]

    @pl.when(t == 0)
    def _():
        buf_ref[0:SUBLANES, :] = hist_ref[0]

    h = h_ref[0]
    hn = _rms(h, g_ref[...])
    z = _mm(hn, win_ref[...], precise)
    gate_b = z[:, 0:d]
    buf_ref[SUBLANES:SUBLANES + tm, :] = z[:, d:2 * d] * z[:, 2 * d:3 * d]
    acc = buf_ref[SUBLANES - 2:SUBLANES - 2 + tm, :] * cw_ref[0:1, :]
    for k in range(1, CONV_WIDTH):
        acc = acc + buf_ref[SUBLANES - 2 + k:SUBLANES - 2 + k + tm, :] * cw_ref[k:k + 1, :]
    out_ref[0] = h + _mm(gate_b * acc, wout_ref[...], precise)
    tail = buf_ref[tm:tm + SUBLANES, :]
    buf_ref[0:SUBLANES, :] = tail

    @pl.when(t == pl.num_programs(1) - 1)
    def _():
        state_ref[0] = tail


def _conv_mix(h, hist, hist_per_batch, g, w_in, cw, w_out, *, tm, precise):
    nb, s, d = h.shape
    hist_map = (lambda b, t: (b, 0, 0)) if hist_per_batch else (lambda b, t: (0, 0, 0))
    out, state = pl.pallas_call(
        functools.partial(_conv_kernel, tm=tm, precise=precise),
        out_shape=(jax.ShapeDtypeStruct((nb, s, d), F32), jax.ShapeDtypeStruct((nb, SUBLANES, d), F32)),
        grid=(nb, s // tm),
        in_specs=[
            pl.BlockSpec((1, tm, d), lambda b, t: (b, t, 0)),
            pl.BlockSpec((1, SUBLANES, d), hist_map),
            pl.BlockSpec((1, d), lambda b, t: (0, 0)),
            pl.BlockSpec((d, 3 * d), lambda b, t: (0, 0)),
            pl.BlockSpec((CONV_WIDTH, d), lambda b, t: (0, 0)),
            pl.BlockSpec((d, d), lambda b, t: (0, 0)),
        ],
        out_specs=(pl.BlockSpec((1, tm, d), lambda b, t: (b, t, 0)),
                   pl.BlockSpec((1, SUBLANES, d), lambda b, t: (b, 0, 0))),
        scratch_shapes=[pltpu.VMEM((tm + SUBLANES, d), F32)],
        compiler_params=_params("arbitrary", "arbitrary"),
        name="conv_mix",
    )(h, hist, g, w_in, cw, w_out)
    return out, state


def _qkv_kernel(h_ref, g_ref, w_ref, cos_ref, sin_ref, q_ref, k_ref, v_ref, *, precise):
    d = h_ref.shape[-1]
    kvd = k_ref.shape[-1]
    hn = _rms(h_ref[0], g_ref[...])
    z = _mm(hn, w_ref[...], precise)
    cos = cos_ref[...]
    sin = sin_ref[...]
    lane = lax.broadcasted_iota(jnp.int32, (1, LANES), 1)
    first_half = (lane % HEAD_DIM) < (HEAD_DIM // 2)

    def rope(blk):
        partner = jnp.where(first_half, pltpu.roll(blk, LANES - HEAD_DIM // 2, 1),
                            pltpu.roll(blk, HEAD_DIM // 2, 1))
        return blk * cos + partner * sin

    for j in range(d // LANES):
        q_ref[0, :, j * LANES:(j + 1) * LANES] = rope(z[:, j * LANES:(j + 1) * LANES]).astype(q_ref.dtype)
    for j in range(kvd // LANES):
        k_ref[0, :, j * LANES:(j + 1) * LANES] = rope(z[:, d + j * LANES:d + (j + 1) * LANES])
    v_ref[0] = z[:, d + kvd:d + 2 * kvd]


def _rope_tables(pos):
    half = HEAD_DIM // 2
    inv = ROPE_THETA ** (-jnp.arange(half, dtype=F32) / half)
    ang = pos.astype(F32)[:, None] * inv[None, :]
    cos, sin = jnp.cos(ang), jnp.sin(ang)
    reps = LANES // HEAD_DIM
    return (jnp.tile(jnp.concatenate([cos, cos], axis=1), (1, reps)),
            jnp.tile(jnp.concatenate([-sin, sin], axis=1), (1, reps)))


def _qkv_proj(h, g, w_qkv, pos, *, tm, precise):
    nb, s, d = h.shape
    kvd = N_KV_HEADS * HEAD_DIM
    cos, sin = _rope_tables(pos)
    qdt = F32 if precise else BF16
    return pl.pallas_call(
        functools.partial(_qkv_kernel, precise=precise),
        out_shape=(jax.ShapeDtypeStruct((nb, s, d), qdt), jax.ShapeDtypeStruct((nb, s, kvd), F32),
                   jax.ShapeDtypeStruct((nb, s, kvd), F32)),
        grid=(nb, s // tm),
        in_specs=[
            pl.BlockSpec((1, tm, d), lambda b, t: (b, t, 0)),
            pl.BlockSpec((1, d), lambda b, t: (0, 0)),
            pl.BlockSpec((d, d + 2 * kvd), lambda b, t: (0, 0)),
            pl.BlockSpec((tm, LANES), lambda b, t: (t, 0)),
            pl.BlockSpec((tm, LANES), lambda b, t: (t, 0)),
        ],
        out_specs=(pl.BlockSpec((1, tm, d), lambda b, t: (b, t, 0)),
                   pl.BlockSpec((1, tm, kvd), lambda b, t: (b, t, 0)),
                   pl.BlockSpec((1, tm, kvd), lambda b, t: (b, t, 0))),
        compiler_params=_params("arbitrary", "arbitrary"),
        name="qkv_proj",
    )(h, g, w_qkv, cos, sin)


def _attend(qh, kb, vb, sink, valid, precise):
    s = _mm_nt(qh, kb, precise) * (HEAD_DIM ** -0.5)
    if valid is not None:
        s = jnp.where(valid, s, NEG_INF)
    m = jnp.maximum(jnp.max(s, axis=-1, keepdims=True), sink)
    p = jnp.exp(s - m)
    denom = jnp.sum(p, axis=-1, keepdims=True) + jnp.exp(sink - m)
    return _mm(p, vb, precise) / denom


def _sink_column(sinks_ref, hk, rows):
    gqa = sinks_ref.shape[0] // N_KV_HEADS
    return jnp.concatenate([jnp.full((rows, 1), sinks_ref[hk * gqa + g], F32) for g in range(gqa)], axis=0)


def _attn_band_kernel(sinks_ref, q_ref, kc_ref, vc_ref, kp_ref, vp_ref, mk_ref, mv_ref, h_ref, wo_ref,
                      out_ref, o_ref, *, tq):
    t = pl.program_id(1)
    gqa = q_ref.shape[-1] // (N_KV_HEADS * HEAD_DIM)
    band = (WIN_CHUNKS + 1) * CHUNK
    kk = jnp.concatenate([kp_ref[0], kc_ref[0]], axis=0).astype(BF16)
    vv = jnp.concatenate([vp_ref[0], vc_ref[0]], axis=0).astype(BF16)
    mk = mk_ref[...].astype(BF16)
    mv = mv_ref[...].astype(BF16)
    col = lax.broadcasted_iota(jnp.int32, (1, N_META + band), 1)
    for c in range(tq // CHUNK):
        first_row = (t * (tq // CHUNK) + c - WIN_CHUNKS) * CHUNK
        valid = (col < N_META) | (first_row + col - N_META >= 0)
        r0 = c * CHUNK
        for hk in range(N_KV_HEADS):
            hs = slice(hk * HEAD_DIM, (hk + 1) * HEAD_DIM)
            qh = jnp.concatenate(
                [q_ref[0, r0:r0 + CHUNK, (hk * gqa + g) * HEAD_DIM:(hk * gqa + g + 1) * HEAD_DIM]
                 for g in range(gqa)], axis=0)
            kb = jnp.concatenate([mk[:, hs], kk[r0:r0 + band, hs]], axis=0)
            vb = jnp.concatenate([mv[:, hs], vv[r0:r0 + band, hs]], axis=0)
            o = _attend(qh, kb, vb, _sink_column(sinks_ref, hk, CHUNK), valid, False)
            for g in range(gqa):
                o_ref[r0:r0 + CHUNK, (hk * gqa + g) * HEAD_DIM:(hk * gqa + g + 1) * HEAD_DIM] = (
                    o[g * CHUNK:(g + 1) * CHUNK])
    out_ref[0] = h_ref[0] + _mm(o_ref[...], wo_ref[...], False)


def _attn_band(q, k, v, mk, mv, sinks, h, w_o, *, tq):
    nb, s, d = h.shape
    kvd = k.shape[-1]
    prev = WIN_CHUNKS * CHUNK
    ratio = tq // prev
    prev_map = lambda b, t: (b, jnp.maximum(t * ratio - 1, 0), 0)
    return pl.pallas_call(
        functools.partial(_attn_band_kernel, tq=tq),
        out_shape=jax.ShapeDtypeStruct((nb, s, d), F32),
        grid=(nb, s // tq),
        in_specs=[
            pl.BlockSpec(memory_space=pltpu.SMEM),
            pl.BlockSpec((1, tq, d), lambda b, t: (b, t, 0)),
            pl.BlockSpec((1, tq, kvd), lambda b, t: (b, t, 0)),
            pl.BlockSpec((1, tq, kvd), lambda b, t: (b, t, 0)),
            pl.BlockSpec((1, prev, kvd), prev_map),
            pl.BlockSpec((1, prev, kvd), prev_map),
            pl.BlockSpec((N_META, kvd), lambda b, t: (0, 0)),
            pl.BlockSpec((N_META, kvd), lambda b, t: (0, 0)),
            pl.BlockSpec((1, tq, d), lambda b, t: (b, t, 0)),
            pl.BlockSpec((d, d), lambda b, t: (0, 0)),
        ],
        out_specs=pl.BlockSpec((1, tq, d), lambda b, t: (b, t, 0)),
        scratch_shapes=[pltpu.VMEM((tq, d), F32)],
        compiler_params=_params("arbitrary", "arbitrary"),
        name="attn_band",
    )(sinks, q, k, v, k, v, mk, mv, h, w_o)


def _attn_full_kernel(sinks_ref, q_ref, k_ref, v_ref, h_ref, wo_ref, out_ref, o_ref, *, precise):
    s = q_ref.shape[1]
    gqa = q_ref.shape[-1] // (N_KV_HEADS * HEAD_DIM)
    for hk in range(N_KV_HEADS):
        hs = slice(hk * HEAD_DIM, (hk + 1) * HEAD_DIM)
        qh = jnp.concatenate(
            [q_ref[0, :, (hk * gqa + g) * HEAD_DIM:(hk * gqa + g + 1) * HEAD_DIM] for g in range(gqa)], axis=0)
        o = _attend(qh, k_ref[0, :, hs], v_ref[0, :, hs], _sink_column(sinks_ref, hk, s), None, precise)
        for g in range(gqa):
            o_ref[:, (hk * gqa + g) * HEAD_DIM:(hk * gqa + g + 1) * HEAD_DIM] = o[g * s:(g + 1) * s]
    out_ref[0] = h_ref[0] + _mm(o_ref[...], wo_ref[...], precise)


def _attn_full(q, keys, vals, sinks, h, w_o, *, precise):
    nb, s, d = h.shape
    kn, kvd = keys.shape[1:]
    return pl.pallas_call(
        functools.partial(_attn_full_kernel, precise=precise),
        out_shape=jax.ShapeDtypeStruct((nb, s, d), F32),
        grid=(nb,),
        in_specs=[
            pl.BlockSpec(memory_space=pltpu.SMEM),
            pl.BlockSpec((1, s, d), lambda b: (b, 0, 0)),
            pl.BlockSpec((1, kn, kvd), lambda b: (b, 0, 0)),
            pl.BlockSpec((1, kn, kvd), lambda b: (b, 0, 0)),
            pl.BlockSpec((1, s, d), lambda b: (b, 0, 0)),
            pl.BlockSpec((d, d), lambda b: (0, 0)),
        ],
        out_specs=pl.BlockSpec((1, s, d), lambda b: (b, 0, 0)),
        scratch_shapes=[pltpu.VMEM((s, d), F32)],
        compiler_params=_params("arbitrary"),
        name="attn_full",
    )(sinks, q, keys, vals, h, w_o)


def _route(logits):
    col = lax.broadcasted_iota(jnp.int32, logits.shape, 1)
    is_group = col < N_GROUPS
    lg = jnp.where(is_group, logits, NEG_INF)
    gmax = jnp.max(lg, axis=-1, keepdims=True)
    g_idx = jnp.min(jnp.where(lg == gmax, col, LANES), axis=-1, keepdims=True)
    g_w = 1.0 / jnp.sum(jnp.exp(lg - gmax), axis=-1, keepdims=True)
    ecol = col - N_GROUPS
    in_group = (ecol >= 0) & (ecol < N_EXPERTS) & ((ecol // EXPERTS_PER_GROUP) == g_idx)
    le = jnp.where(in_group, logits, NEG_INF)
    m1 = jnp.max(le, axis=-1, keepdims=True)
    i1 = jnp.min(jnp.where(in_group & (le == m1), col, LANES), axis=-1, keepdims=True)
    rest = in_group & (col != i1)
    le2 = jnp.where(rest, logits, NEG_INF)
    m2 = jnp.max(le2, axis=-1, keepdims=True)
    i2 = jnp.min(jnp.where(rest & (le2 == m2), col, LANES), axis=-1, keepdims=True)
    e2 = jnp.exp(m2 - m1)
    p1 = 1.0 / (1.0 + e2)
    return g_w * (jnp.where(col == i1, p1, 0.0) + jnp.where(col == i2, e2 * p1, 0.0))


def _moe_dense_kernel(h_ref, g_ref, wr_ref, br_ref, wg_ref, wu_ref, wd_ref, gf_ref, out_ref,
                      hn_ref, comb_ref, acc_ref, *, precise, final_norm):
    e = pl.program_id(1)

    @pl.when(e == 0)
    def _():
        hn = _rms(h_ref[...], g_ref[...])
        hn_ref[...] = hn.astype(hn_ref.dtype)
        logits = jnp.dot(hn, wr_ref[...], preferred_element_type=F32,
                         precision=lax.Precision.HIGHEST) + br_ref[...]
        comb_ref[...] = _route(logits)
        acc_ref[...] = jnp.zeros_like(acc_ref)

    x = hn_ref[...]
    col = lax.broadcasted_iota(jnp.int32, comb_ref.shape, 1)
    c = jnp.sum(jnp.where(col == e + N_GROUPS, comb_ref[...], 0.0), axis=-1, keepdims=True)
    gate = _mm(x, wg_ref[0], precise)
    up = _mm(x, wu_ref[0], precise)
    act = gate * jax.nn.sigmoid(gate) * up * c
    acc_ref[...] += _mm(act, wd_ref[0], precise)

    @pl.when(e == pl.num_programs(1) - 1)
    def _():
        y = h_ref[...] + acc_ref[...]
        if final_norm:
            y = _rms(y, gf_ref[...])
        out_ref[...] = y


def _moe_dense(h, g, wr, br, wg, wu, wd, gf, *, tm, precise, final_norm):
    n, d = h.shape
    ne, _, de = wg.shape
    return pl.pallas_call(
        functools.partial(_moe_dense_kernel, precise=precise, final_norm=final_norm),
        out_shape=jax.ShapeDtypeStruct((n, d), F32),
        grid=(n // tm, ne),
        in_specs=[
            pl.BlockSpec((tm, d), lambda i, e: (i, 0)),
            pl.BlockSpec((1, d), lambda i, e: (0, 0)),
            pl.BlockSpec((d, LANES), lambda i, e: (0, 0)),
            pl.BlockSpec((1, LANES), lambda i, e: (0, 0)),
            pl.BlockSpec((1, d, de), lambda i, e: (e, 0, 0)),
            pl.BlockSpec((1, d, de), lambda i, e: (e, 0, 0)),
            pl.BlockSpec((1, de, d), lambda i, e: (e, 0, 0)),
            pl.BlockSpec((1, d), lambda i, e: (0, 0)),
        ],
        out_specs=pl.BlockSpec((tm, d), lambda i, e: (i, 0)),
        scratch_shapes=[pltpu.VMEM((tm, d), F32 if precise else BF16), pltpu.VMEM((tm, LANES), F32),
                        pltpu.VMEM((tm, d), F32)],
        compiler_params=_params("arbitrary", "arbitrary"),
        name="moe_dense",
    )(h, g, wr, br, wg, wu, wd, gf)


def kernel(x_prompt, x_sample, state_pool, cache_swa_kv, cache_meta_kv, state_conv, meta_tokens, norm_mix, norm_ffn, norm_final, w_pool, pool_scale, w_qkv, w_o, attn_sinks, w_conv_in, conv_w, w_conv_out, w_group, b_group, w_expert_router, b_expert_router, w_gate, w_up, w_down):
    nb, seq, d = x_prompt.shape
    db, dseq, _ = x_sample.shape
    depth = norm_mix.shape[0]
    kvd = N_KV_HEADS * HEAD_DIM
    tm_main = min(512, seq)
    tq_main = min(256, seq)
    halo = POOL_STATE + 1

    row = lambda a: a.reshape(1, -1)
    wr = jnp.pad(jnp.concatenate([w_group, w_expert_router], axis=-1),
                 ((0, 0), (0, 0), (0, LANES - N_GROUPS - N_EXPERTS)))
    br = jnp.pad(jnp.concatenate([b_group, b_expert_router], axis=-1),
                 ((0, 0), (0, LANES - N_GROUPS - N_EXPERTS)))
    bf = lambda a: a.astype(BF16)
    w_pool_b, w_qkv_b, w_o_b = bf(w_pool), bf(w_qkv), bf(w_o)
    w_conv_in_b, w_conv_out_b = bf(w_conv_in), bf(w_conv_out)
    w_gate_b, w_up_b, w_down_b = bf(w_gate), bf(w_up), bf(w_down)

    hm = meta_tokens.astype(F32)[None]
    hp = x_prompt
    hs = x_sample
    pool_p, swa_p, meta_p, conv_p, pool_s, swa_s, conv_s = [], [], [], [], [], [], []
    y_prompt = y_sample = None
    for i in range(depth):
        j = i // N_MIXERS
        g = row(norm_mix[i])
        if i % N_MIXERS == 0:
            sc = row(pool_scale[j])
            hm, st_m = _pool_mix(hm, jnp.zeros((1, halo, d), F32), False, g, w_pool[j], sc,
                                 tm=N_META, has_history=False, precise=True)
            hp, st_p = _pool_mix(hp, st_m, False, g, w_pool_b[j], sc, tm=tm_main, has_history=True,
                                 precise=False)
            hist_s = jnp.pad(state_pool[j].astype(F32), ((0, 0), (1, 0), (0, 0)))
            hs, st_s = _pool_mix(hs, hist_s, True, g, w_pool[j], sc, tm=dseq, has_history=True,
                                 precise=True)
            pool_p.append(st_p[:, 1:])
            pool_s.append(st_s[:, 1:])
        elif i % N_MIXERS == 1:
            sinks = attn_sinks[j].astype(F32)
            qm, km, vm = _qkv_proj(hm, g, w_qkv[j], jnp.arange(N_META), tm=N_META, precise=True)
            qp, kp, vp = _qkv_proj(hp, g, w_qkv_b[j], N_META + jnp.arange(seq), tm=tm_main, precise=False)
            qs, ks, vs = _qkv_proj(hs, g, w_qkv[j], PAST_LEN + N_META + jnp.arange(dseq), tm=dseq,
                                   precise=True)
            hm = _attn_full(qm, km, vm, sinks, hm, w_o[j], precise=True)
            hp = _attn_band(qp, kp, vp, km[0], vm[0], sinks, hp, w_o_b[j], tq=tq_main)
            flat = lambda a: a.reshape(a.shape[0], a.shape[1], kvd)
            keys = jnp.concatenate([flat(cache_meta_kv[j][:, :, 0]), flat(cache_swa_kv[j][:, :, 0]), ks], axis=1)
            vals = jnp.concatenate([flat(cache_meta_kv[j][:, :, 1]), flat(cache_swa_kv[j][:, :, 1]), vs], axis=1)
            hs = _attn_full(qs, keys, vals, sinks, hs, w_o[j], precise=True)
            heads = lambda a: a.reshape(a.shape[0], a.shape[1], N_KV_HEADS, HEAD_DIM)
            swa_p.append(jnp.stack([heads(kp[:, -WINDOW:]), heads(vp[:, -WINDOW:])], axis=2))
            meta_kv = jnp.stack([heads(km), heads(vm)], axis=2)
            meta_p.append(jnp.broadcast_to(meta_kv, (nb,) + meta_kv.shape[1:]))
            swa_s.append(jnp.stack([heads(ks), heads(vs)], axis=2))
        else:
            cw = conv_w[j].astype(F32)
            hm, st_m = _conv_mix(hm, jnp.zeros((1, SUBLANES, d), F32), False, g, w_conv_in[j], cw,
                                 w_conv_out[j], tm=N_META, precise=True)
            hp, st_p = _conv_mix(hp, st_m, False, g, w_conv_in_b[j], cw, w_conv_out_b[j], tm=tm_main,
                                 precise=False)
            hist_s = jnp.pad(state_conv[j].astype(F32), ((0, 0), (SUBLANES - (CONV_WIDTH - 1), 0), (0, 0)))
            hs, st_s = _conv_mix(hs, hist_s, True, g, w_conv_in[j], cw, w_conv_out[j], tm=dseq, precise=True)
            conv_p.append(st_p[:, SUBLANES - (CONV_WIDTH - 1):])
            conv_s.append(st_s[:, SUBLANES - (CONV_WIDTH - 1):])

        final = i == depth - 1
        gf = row(norm_final)
        gn = row(norm_ffn[i])
        small = jnp.concatenate([hm.reshape(-1, d), hs.reshape(-1, d)], axis=0)
        small = _moe_dense(small, gn, wr[i], br[i:i + 1], w_gate[i], w_up[i], w_down[i], gf,
                           tm=small.shape[0], precise=True, final_norm=final)
        hm = small[:N_META].reshape(1, N_META, d)
        hs = small[N_META:].reshape(db, dseq, d)
        hp = _moe_dense(hp.reshape(-1, d), gn, wr[i], br[i:i + 1], w_gate_b[i], w_up_b[i], w_down_b[i], gf,
                        tm=tm_main, precise=False, final_norm=final).reshape(nb, seq, d)
    y_prompt, y_sample = hp, hs

    return (y_prompt, y_sample, jnp.stack(pool_p), jnp.stack(swa_p), jnp.stack(meta_p), jnp.stack(conv_p),
            jnp.stack(pool_s), jnp.stack(swa_s), jnp.stack(conv_s))
```

```python
import functools

import jax
import jax.numpy as jnp
from jax import lax
from jax.experimental import pallas as pl
from jax.experimental.pallas import tpu as pltpu

F32 = jnp.float32
BF16 = jnp.bfloat16

CHUNK = 64
N_META = 16
N_MIXERS = 3
POOL_WINDOWS = (2, 4, 8, 16)
POOL_STATE = max(POOL_WINDOWS) - 1
HEAD_DIM = 64
N_KV_HEADS = 4
WINDOW = 128
WIN_CHUNKS = WINDOW // CHUNK
ROPE_THETA = 10000.0
CONV_WIDTH = 3
N_GROUPS = 4
EXPERTS_PER_GROUP = 4
N_EXPERTS = N_GROUPS * EXPERTS_PER_GROUP
PAST_LEN = 2048
EPS = 1e-6

LANES = 128
SUBLANES = 8
VMEM_LIMIT = 48 * 1024 * 1024
NEG_INF = float("-inf")


def _params(*sem):
    return pltpu.CompilerParams(dimension_semantics=sem, vmem_limit_bytes=VMEM_LIMIT)


def _mm(a, b, precise):
    if precise:
        return jnp.dot(a.astype(F32), b.astype(F32), preferred_element_type=F32,
                       precision=lax.Precision.HIGHEST)
    return jnp.dot(a.astype(BF16), b.astype(BF16), preferred_element_type=F32)


def _mm_nt(a, b, precise):
    dn = (((1,), (1,)), ((), ()))
    if precise:
        return lax.dot_general(a.astype(F32), b.astype(F32), dn, preferred_element_type=F32,
                               precision=lax.Precision.HIGHEST)
    return lax.dot_general(a.astype(BF16), b.astype(BF16), dn, preferred_element_type=F32)


def _rms(x, g):
    ms = jnp.mean(x * x, axis=-1, keepdims=True)
    return x * lax.rsqrt(ms + EPS) * g


def _pool_kernel(h_ref, hist_ref, g_ref, w_ref, scale_ref, out_ref, state_ref, buf_ref, *,
                 tm, has_history, precise):
    t = pl.program_id(1)
    halo = POOL_STATE + 1
    pg = h_ref.shape[-1] // len(POOL_WINDOWS)

    @pl.when(t == 0)
    def _():
        buf_ref[0:halo, :] = hist_ref[0]

    h = h_ref[0]
    hn = _rms(h, g_ref[...])
    buf_ref[halo:halo + tm, :] = hn
    if not has_history:
        pos = t * tm + lax.broadcasted_iota(jnp.int32, (tm, 1), 0)
    ys = []
    for gi, w in enumerate(POOL_WINDOWS):
        c0, c1 = gi * pg, (gi + 1) * pg
        win = buf_ref[halo:halo + tm, c0:c1]
        for k in range(1, w):
            win = win + buf_ref[halo - k:halo - k + tm, c0:c1]
        if has_history:
            mean = win * (1.0 / w)
        else:
            mean = win / jnp.minimum(pos + 1, w).astype(F32)
        ys.append(_mm(mean - hn[:, c0:c1], w_ref[gi], precise))
    y = jnp.concatenate(ys, axis=1) * scale_ref[...]
    out_ref[0] = h + y
    tail = buf_ref[tm:tm + halo, :]
    buf_ref[0:halo, :] = tail

    @pl.when(t == pl.num_programs(1) - 1)
    def _():
        state_ref[0] = tail


def _pool_mix(h, hist, hist_per_batch, g, w, scale, *, tm, has_history, precise):
    nb, s, d = h.shape
    halo = POOL_STATE + 1
    pg = d // len(POOL_WINDOWS)
    hist_map = (lambda b, t: (b, 0, 0)) if hist_per_batch else (lambda b, t: (0, 0, 0))
    out, state = pl.pallas_call(
        functools.partial(_pool_kernel, tm=tm, has_history=has_history, precise=precise),
        out_shape=(jax.ShapeDtypeStruct((nb, s, d), F32), jax.ShapeDtypeStruct((nb, halo, d), F32)),
        grid=(nb, s // tm),
        in_specs=[
            pl.BlockSpec((1, tm, d), lambda b, t: (b, t, 0)),
            pl.BlockSpec((1, halo, d), hist_map),
            pl.BlockSpec((1, d), lambda b, t: (0, 0)),
            pl.BlockSpec((len(POOL_WINDOWS), pg, pg), lambda b, t: (0, 0, 0)),
            pl.BlockSpec((1, d), lambda b, t: (0, 0)),
        ],
        out_specs=(pl.BlockSpec((1, tm, d), lambda b, t: (b, t, 0)),
                   pl.BlockSpec((1, halo, d), lambda b, t: (b, 0, 0))),
        scratch_shapes=[pltpu.VMEM((tm + halo, d), F32)],
        compiler_params=_params("arbitrary", "arbitrary"),
        name="pool_mix",
    )(h, hist, g, w, scale)
    return out, state


def _conv_kernel(h_ref, hist_ref, g_ref, win_ref, cw_ref, wout_ref, out_ref, state_ref, buf_ref, *,
                 tm, precise):
    t = pl.program_id(1)
    d = h_ref.shape[-1]

    @pl.when(t == 0)
    def _():
        buf_ref[0:SUBLANES, :] = hist_ref[0]

    h = h_ref[0]
    hn = _rms(h, g_ref[...])
    z = _mm(hn, win_ref[...], precise)
    gate_b = z[:, 0:d]
    buf_ref[SUBLANES:SUBLANES + tm, :] = z[:, d:2 * d] * z[:, 2 * d:3 * d]
    acc = buf_ref[SUBLANES - 2:SUBLANES - 2 + tm, :] * cw_ref[0:1, :]
    for k in range(1, CONV_WIDTH):
        acc = acc + buf_ref[SUBLANES - 2 + k:SUBLANES - 2 + k + tm, :] * cw_ref[k:k + 1, :]
    out_ref[0] = h + _mm(gate_b * acc, wout_ref[...], precise)
    tail = buf_ref[tm:tm + SUBLANES, :]
    buf_ref[0:SUBLANES, :] = tail

    @pl.when(t == pl.num_programs(1) - 1)
    def _():
        state_ref[0] = tail


def _conv_mix(h, hist, hist_per_batch, g, w_in, cw, w_out, *, tm, precise):
    nb, s, d = h.shape
    hist_map = (lambda b, t: (b, 0, 0)) if hist_per_batch else (lambda b, t: (0, 0, 0))
    out, state = pl.pallas_call(
        functools.partial(_conv_kernel, tm=tm, precise=precise),
        out_shape=(jax.ShapeDtypeStruct((nb, s, d), F32), jax.ShapeDtypeStruct((nb, SUBLANES, d), F32)),
        grid=(nb, s // tm),
        in_specs=[
            pl.BlockSpec((1, tm, d), lambda b, t: (b, t, 0)),
            pl.BlockSpec((1, SUBLANES, d), hist_map),
            pl.BlockSpec((1, d), lambda b, t: (0, 0)),
            pl.BlockSpec((d, 3 * d), lambda b, t: (0, 0)),
            pl.BlockSpec((CONV_WIDTH, d), lambda b, t: (0, 0)),
            pl.BlockSpec((d, d), lambda b, t: (0, 0)),
        ],
        out_specs=(pl.BlockSpec((1, tm, d), lambda b, t: (b, t, 0)),
                   pl.BlockSpec((1, SUBLANES, d), lambda b, t: (b, 0, 0))),
        scratch_shapes=[pltpu.VMEM((tm + SUBLANES, d), F32)],
        compiler_params=_params("arbitrary", "arbitrary"),
        name="conv_mix",
    )(h, hist, g, w_in, cw, w_out)
    return out, state


def _qkv_kernel(h_ref, g_ref, w_ref, cos_ref, sin_ref, q_ref, k_ref, v_ref, *, precise):
    d = h_ref.shape[-1]
    kvd = k_ref.shape[-1]
    hn = _rms(h_ref[0], g_ref[...])
    z = _mm(hn, w_ref[...], precise)
    cos = cos_ref[...]
    sin = sin_ref[...]
    lane = lax.broadcasted_iota(jnp.int32, (1, LANES), 1)
    first_half = (lane % HEAD_DIM) < (HEAD_DIM // 2)

    def rope(blk):
        partner = jnp.where(first_half, pltpu.roll(blk, LANES - HEAD_DIM // 2, 1),
                            pltpu.roll(blk, HEAD_DIM // 2, 1))
        return blk * cos + partner * sin

    for j in range(d // LANES):
        q_ref[0, :, j * LANES:(j + 1) * LANES] = rope(z[:, j * LANES:(j + 1) * LANES]).astype(q_ref.dtype)
    for j in range(kvd // LANES):
        k_ref[0, :, j * LANES:(j + 1) * LANES] = rope(z[:, d + j * LANES:d + (j + 1) * LANES])
    v_ref[0] = z[:, d + kvd:d + 2 * kvd]


def _rope_tables(pos):
    half = HEAD_DIM // 2
    inv = ROPE_THETA ** (-jnp.arange(half, dtype=F32) / half)
    ang = pos.astype(F32)[:, None] * inv[None, :]
    cos, sin = jnp.cos(ang), jnp.sin(ang)
    reps = LANES // HEAD_DIM
    return (jnp.tile(jnp.concatenate([cos, cos], axis=1), (1, reps)),
            jnp.tile(jnp.concatenate([-sin, sin], axis=1), (1, reps)))


def _qkv_proj(h, g, w_qkv, pos, *, tm, precise):
    nb, s, d = h.shape
    kvd = N_KV_HEADS * HEAD_DIM
    cos, sin = _rope_tables(pos)
    qdt = F32 if precise else BF16
    return pl.pallas_call(
        functools.partial(_qkv_kernel, precise=precise),
        out_shape=(jax.ShapeDtypeStruct((nb, s, d), qdt), jax.ShapeDtypeStruct((nb, s, kvd), F32),
                   jax.ShapeDtypeStruct((nb, s, kvd), F32)),
        grid=(nb, s // tm),
        in_specs=[
            pl.BlockSpec((1, tm, d), lambda b, t: (b, t, 0)),
            pl.BlockSpec((1, d), lambda b, t: (0, 0)),
            pl.BlockSpec((d, d + 2 * kvd), lambda b, t: (0, 0)),
            pl.BlockSpec((tm, LANES), lambda b, t: (t, 0)),
            pl.BlockSpec((tm, LANES), lambda b, t: (t, 0)),
        ],
        out_specs=(pl.BlockSpec((1, tm, d), lambda b, t: (b, t, 0)),
                   pl.BlockSpec((1, tm, kvd), lambda b, t: (b, t, 0)),
                   pl.BlockSpec((1, tm, kvd), lambda b, t: (b, t, 0))),
        compiler_params=_params("arbitrary", "arbitrary"),
        name="qkv_proj",
    )(h, g, w_qkv, cos, sin)


def _attend(qh, kb, vb, sink, valid, precise):
    s = _mm_nt(qh, kb, precise) * (HEAD_DIM ** -0.5)
    if valid is not None:
        s = jnp.where(valid, s, NEG_INF)
    m = jnp.maximum(jnp.max(s, axis=-1, keepdims=True), sink)
    p = jnp.exp(s - m)
    denom = jnp.sum(p, axis=-1, keepdims=True) + jnp.exp(sink - m)
    return _mm(p, vb, precise) / denom


def _sink_column(sinks_ref, hk, rows):
    gqa = sinks_ref.shape[0] // N_KV_HEADS
    return jnp.concatenate([jnp.full((rows, 1), sinks_ref[hk * gqa + g], F32) for g in range(gqa)], axis=0)


def _attn_band_kernel(sinks_ref, q_ref, kc_ref, vc_ref, kp_ref, vp_ref, mk_ref, mv_ref, h_ref, wo_ref,
                      out_ref, o_ref, *, tq):
    t = pl.program_id(1)
    gqa = q_ref.shape[-1] // (N_KV_HEADS * HEAD_DIM)
    band = (WIN_CHUNKS + 1) * CHUNK
    kk = jnp.concatenate([kp_ref[0], kc_ref[0]], axis=0).astype(BF16)
    vv = jnp.concatenate([vp_ref[0], vc_ref[0]], axis=0).astype(BF16)
    mk = mk_ref[...].astype(BF16)
    mv = mv_ref[...].astype(BF16)
    col = lax.broadcasted_iota(jnp.int32, (1, N_META + band), 1)
    for c in range(tq // CHUNK):
        first_row = (t * (tq // CHUNK) + c - WIN_CHUNKS) * CHUNK
        valid = (col < N_META) | (first_row + col - N_META >= 0)
        r0 = c * CHUNK
        for hk in range(N_KV_HEADS):
            hs = slice(hk * HEAD_DIM, (hk + 1) * HEAD_DIM)
            qh = jnp.concatenate(
                [q_ref[0, r0:r0 + CHUNK, (hk * gqa + g) * HEAD_DIM:(hk * gqa + g + 1) * HEAD_DIM]
                 for g in range(gqa)], axis=0)
            kb = jnp.concatenate([mk[:, hs], kk[r0:r0 + band, hs]], axis=0)
            vb = jnp.concatenate([mv[:, hs], vv[r0:r0 + band, hs]], axis=0)
            o = _attend(qh, kb, vb, _sink_column(sinks_ref, hk, CHUNK), valid, False)
            for g in range(gqa):
                o_ref[r0:r0 + CHUNK, (hk * gqa + g) * HEAD_DIM:(hk * gqa + g + 1) * HEAD_DIM] = (
                    o[g * CHUNK:(g + 1) * CHUNK])
    out_ref[0] = h_ref[0] + _mm(o_ref[...], wo_ref[...], False)


def _attn_band(q, k, v, mk, mv, sinks, h, w_o, *, tq):
    nb, s, d = h.shape
    kvd = k.shape[-1]
    prev = WIN_CHUNKS * CHUNK
    ratio = tq // prev
    prev_map = lambda b, t: (b, jnp.maximum(t * ratio - 1, 0), 0)
    return pl.pallas_call(
        functools.partial(_attn_band_kernel, tq=tq),
        out_shape=jax.ShapeDtypeStruct((nb, s, d), F32),
        grid=(nb, s // tq),
        in_specs=[
            pl.BlockSpec(memory_space=pltpu.SMEM),
            pl.BlockSpec((1, tq, d), lambda b, t: (b, t, 0)),
            pl.BlockSpec((1, tq, kvd), lambda b, t: (b, t, 0)),
            pl.BlockSpec((1, tq, kvd), lambda b, t: (b, t, 0)),
            pl.BlockSpec((1, prev, kvd), prev_map),
            pl.BlockSpec((1, prev, kvd), prev_map),
            pl.BlockSpec((N_META, kvd), lambda b, t: (0, 0)),
            pl.BlockSpec((N_META, kvd), lambda b, t: (0, 0)),
            pl.BlockSpec((1, tq, d), lambda b, t: (b, t, 0)),
            pl.BlockSpec((d, d), lambda b, t: (0, 0)),
        ],
        out_specs=pl.BlockSpec((1, tq, d), lambda b, t: (b, t, 0)),
        scratch_shapes=[pltpu.VMEM((tq, d), F32)],
        compiler_params=_params("arbitrary", "arbitrary"),
        name="attn_band",
    )(sinks, q, k, v, k, v, mk, mv, h, w_o)


def _attn_full_kernel(sinks_ref, q_ref, k_ref, v_ref, h_ref, wo_ref, out_ref, o_ref, *, precise):
    s = q_ref.shape[1]
    gqa = q_ref.shape[-1] // (N_KV_HEADS * HEAD_DIM)
    for hk in range(N_KV_HEADS):
        hs = slice(hk * HEAD_DIM, (hk + 1) * HEAD_DIM)
        qh = jnp.concatenate(
            [q_ref[0, :, (hk * gqa + g) * HEAD_DIM:(hk * gqa + g + 1) * HEAD_DIM] for g in range(gqa)], axis=0)
        o = _attend(qh, k_ref[0, :, hs], v_ref[0, :, hs], _sink_column(sinks_ref, hk, s), None, precise)
        for g in range(gqa):
            o_ref[:, (hk * gqa + g) * HEAD_DIM:(hk * gqa + g + 1) * HEAD_DIM] = o[g * s:(g + 1) * s]
    out_ref[0] = h_ref[0] + _mm(o_ref[...], wo_ref[...], precise)


def _attn_full(q, keys, vals, sinks, h, w_o, *, precise):
    nb, s, d = h.shape
    kn, kvd = keys.shape[1:]
    return pl.pallas_call(
        functools.partial(_attn_full_kernel, precise=precise),
        out_shape=jax.ShapeDtypeStruct((nb, s, d), F32),
        grid=(nb,),
        in_specs=[
            pl.BlockSpec(memory_space=pltpu.SMEM),
            pl.BlockSpec((1, s, d), lambda b: (b, 0, 0)),
            pl.BlockSpec((1, kn, kvd), lambda b: (b, 0, 0)),
            pl.BlockSpec((1, kn, kvd), lambda b: (b, 0, 0)),
            pl.BlockSpec((1, s, d), lambda b: (b, 0, 0)),
            pl.BlockSpec((d, d), lambda b: (0, 0)),
        ],
        out_specs=pl.BlockSpec((1, s, d), lambda b: (b, 0, 0)),
        scratch_shapes=[pltpu.VMEM((s, d), F32)],
        compiler_params=_params("arbitrary"),
        name="attn_full",
    )(sinks, q, keys, vals, h, w_o)


def _route(logits):
    col = lax.broadcasted_iota(jnp.int32, logits.shape, 1)
    is_group = col < N_GROUPS
    lg = jnp.where(is_group, logits, NEG_INF)
    gmax = jnp.max(lg, axis=-1, keepdims=True)
    g_idx = jnp.min(jnp.where(lg == gmax, col, LANES), axis=-1, keepdims=True)
    g_w = 1.0 / jnp.sum(jnp.exp(lg - gmax), axis=-1, keepdims=True)
    ecol = col - N_GROUPS
    in_group = (ecol >= 0) & (ecol < N_EXPERTS) & ((ecol // EXPERTS_PER_GROUP) == g_idx)
    le = jnp.where(in_group, logits, NEG_INF)
    m1 = jnp.max(le, axis=-1, keepdims=True)
    i1 = jnp.min(jnp.where(in_group & (le == m1), col, LANES), axis=-1, keepdims=True)
    rest = in_group & (col != i1)
    le2 = jnp.where(rest, logits, NEG_INF)
    m2 = jnp.max(le2, axis=-1, keepdims=True)
    i2 = jnp.min(jnp.where(rest & (le2 == m2), col, LANES), axis=-1, keepdims=True)
    e2 = jnp.exp(m2 - m1)
    p1 = 1.0 / (1.0 + e2)
    return g_w * (jnp.where(col == i1, p1, 0.0) + jnp.where(col == i2, e2 * p1, 0.0))


def _moe_dense_kernel(h_ref, g_ref, wr_ref, br_ref, wg_ref, wu_ref, wd_ref, gf_ref, out_ref,
                      hn_ref, comb_ref, acc_ref, *, precise, final_norm):
    e = pl.program_id(1)

    @pl.when(e == 0)
    def _():
        hn = _rms(h_ref[...], g_ref[...])
        hn_ref[...] = hn.astype(hn_ref.dtype)
        logits = jnp.dot(hn, wr_ref[...], preferred_element_type=F32,
                         precision=lax.Precision.HIGHEST) + br_ref[...]
        comb_ref[...] = _route(logits)
        acc_ref[...] = jnp.zeros_like(acc_ref)

    x = hn_ref[...]
    col = lax.broadcasted_iota(jnp.int32, comb_ref.shape, 1)
    c = jnp.sum(jnp.where(col == e + N_GROUPS, comb_ref[...], 0.0), axis=-1, keepdims=True)
    gate = _mm(x, wg_ref[0], precise)
    up = _mm(x, wu_ref[0], precise)
    act = gate * jax.nn.sigmoid(gate) * up * c
    acc_ref[...] += _mm(act, wd_ref[0], precise)

    @pl.when(e == pl.num_programs(1) - 1)
    def _():
        y = h_ref[...] + acc_ref[...]
        if final_norm:
            y = _rms(y, gf_ref[...])
        out_ref[...] = y


def _moe_dense(h, g, wr, br, wg, wu, wd, gf, *, tm, precise, final_norm):
    n, d = h.shape
    ne, _, de = wg.shape
    return pl.pallas_call(
        functools.partial(_moe_dense_kernel, precise=precise, final_norm=final_norm),
        out_shape=jax.ShapeDtypeStruct((n, d), F32),
        grid=(n // tm, ne),
        in_specs=[
            pl.BlockSpec((tm, d), lambda i, e: (i, 0)),
            pl.BlockSpec((1, d), lambda i, e: (0, 0)),
            pl.BlockSpec((d, LANES), lambda i, e: (0, 0)),
            pl.BlockSpec((1, LANES), lambda i, e: (0, 0)),
            pl.BlockSpec((1, d, de), lambda i, e: (e, 0, 0)),
            pl.BlockSpec((1, d, de), lambda i, e: (e, 0, 0)),
            pl.BlockSpec((1, de, d), lambda i, e: (e, 0, 0)),
            pl.BlockSpec((1, d), lambda i, e: (0, 0)),
        ],
        out_specs=pl.BlockSpec((tm, d), lambda i, e: (i, 0)),
        scratch_shapes=[pltpu.VMEM((tm, d), F32 if precise else BF16), pltpu.VMEM((tm, LANES), F32),
                        pltpu.VMEM((tm, d), F32)],
        compiler_params=_params("arbitrary", "arbitrary"),
        name="moe_dense",
    )(h, g, wr, br, wg, wu, wd, gf)


MOE_CHUNK = 128
SEG_ALIGN = 16
ROUTE_ROWS = 32


def _split_bf16(x, parts):
    out = []
    for _ in range(parts):
        hi = x.astype(BF16)
        out.append(hi)
        x = x - hi.astype(F32)
    return out


def _route_t(lt, tm):
    row8 = lax.broadcasted_iota(jnp.int32, (SUBLANES, tm), 0)
    lg = jnp.where(row8 < N_GROUPS, lt[0:SUBLANES], NEG_INF)
    gmax = jnp.max(lg, axis=0, keepdims=True)
    g_idx = jnp.min(jnp.where(lg == gmax, row8, SUBLANES), axis=0, keepdims=True)
    g_w = 1.0 / jnp.sum(jnp.exp(lg - gmax), axis=0, keepdims=True)
    le = lt[SUBLANES:SUBLANES + N_EXPERTS]
    row16 = lax.broadcasted_iota(jnp.int32, (N_EXPERTS, tm), 0)
    in_group = (row16 // EXPERTS_PER_GROUP) == g_idx
    l1 = jnp.where(in_group, le, NEG_INF)
    m1 = jnp.max(l1, axis=0, keepdims=True)
    i1 = jnp.min(jnp.where(in_group & (l1 == m1), row16, N_EXPERTS), axis=0, keepdims=True)
    rest = in_group & (row16 != i1)
    l2 = jnp.where(rest, le, NEG_INF)
    m2 = jnp.max(l2, axis=0, keepdims=True)
    i2 = jnp.min(jnp.where(rest & (l2 == m2), row16, N_EXPERTS), axis=0, keepdims=True)
    e2 = jnp.exp(m2 - m1)
    p1 = 1.0 / (1.0 + e2)
    comb = g_w * (jnp.where(row16 == i1, p1, 0.0) + jnp.where(row16 == i2, e2 * p1, 0.0))
    c8 = comb[0:SUBLANES] + comb[SUBLANES:2 * SUBLANES]
    return g_idx, c8 + pltpu.roll(c8, EXPERTS_PER_GROUP, 0)


def _moe_sparse_kernel(h_ref, g_ref, wr_ref, br_ref, wg_ref, wu_ref, wd_ref, gf_ref, out_ref,
                       xs_ref, ys_ref, p_ref, cs_ref, *, tm, final_norm):
    rows = xs_ref.shape[0]
    h = h_ref[...]
    hn = _rms(h, g_ref[...])
    hn_hi, hn_lo = _split_bf16(hn, 2)
    lt = (_mm_nt(wr_ref[0], hn_hi, False) + _mm_nt(wr_ref[0], hn_lo, False)
          + _mm_nt(wr_ref[1], hn_hi, False) + br_ref[...])
    g_idx, comb4 = _route_t(lt, tm)

    row8 = lax.broadcasted_iota(jnp.int32, (SUBLANES, tm), 0)
    lane = lax.broadcasted_iota(jnp.int32, (SUBLANES, tm), 1)
    onehot = (row8 == g_idx).astype(F32)
    incl = onehot
    shift = 1
    while shift < tm:
        incl = incl + jnp.where(lane >= shift, pltpu.roll(incl, shift, 1), 0.0)
        shift *= 2
    counts = incl[:, tm - 1:tm].astype(jnp.int32)
    n = [counts[g, 0] for g in range(N_GROUPS)]
    starts = [jnp.int32(0)]
    for g in range(N_GROUPS - 1):
        starts.append(starts[-1] + (n[g] + SEG_ALIGN - 1) // SEG_ALIGN * SEG_ALIGN)
    rank = jnp.sum(onehot * incl, axis=0, keepdims=True).astype(jnp.int32) - 1
    start_tok = jnp.zeros_like(g_idx)
    for g in range(1, N_GROUPS):
        start_tok = jnp.where(g_idx == g, starts[g], start_tok)
    pos = start_tok + rank
    riota = lax.broadcasted_iota(jnp.int32, (rows, tm), 0)
    p_ref[...] = jnp.where(riota == pos, 1.0, 0.0).astype(BF16)

    xs_ref[...] = jnp.dot(p_ref[...], hn_hi, preferred_element_type=F32).astype(BF16)
    comb_parts = jnp.concatenate(_split_bf16(comb4, 3), axis=0)
    cs = _mm_nt(p_ref[...], comb_parts, False)
    cs_ref[...] = cs[:, 0:SUBLANES] + cs[:, SUBLANES:2 * SUBLANES] + cs[:, 2 * SUBLANES:3 * SUBLANES]
    ys_ref[...] = jnp.zeros_like(ys_ref)

    for g in range(N_GROUPS):
        def chunk(c, carry, g=g):
            r0 = pl.multiple_of(starts[g] + c * MOE_CHUNK, SEG_ALIGN)
            x = xs_ref[pl.ds(r0, MOE_CHUNK), :]
            cc = cs_ref[pl.ds(r0, MOE_CHUNK), :]
            acts = []
            for j in range(EXPERTS_PER_GROUP):
                e = g * EXPERTS_PER_GROUP + j
                gate = jnp.dot(x, wg_ref[e], preferred_element_type=F32)
                up = jnp.dot(x, wu_ref[e], preferred_element_type=F32)
                acts.append((gate * jax.nn.sigmoid(gate) * up * cc[:, j:j + 1]).astype(BF16))
            y = jnp.dot(jnp.concatenate(acts, axis=1), wd_ref[g], preferred_element_type=F32)
            ys_ref[pl.ds(r0, MOE_CHUNK), :] = y.astype(BF16)
            return carry

        lax.fori_loop(0, (n[g] + MOE_CHUNK - 1) // MOE_CHUNK, chunk, 0)

    back = lax.dot_general(p_ref[...], ys_ref[...], (((0,), (0,)), ((), ())), preferred_element_type=F32)
    y = h + back
    if final_norm:
        y = _rms(y, gf_ref[...])
    out_ref[...] = y


def _moe_sparse(h, g, wr2, brt, wg, wu, wd4, gf, *, tm, final_norm):
    n, d = h.shape
    ne, _, de = wg.shape
    rows = tm + N_GROUPS * SEG_ALIGN + MOE_CHUNK
    resident = dict(pipeline_mode=pl.Buffered(1))
    return pl.pallas_call(
        functools.partial(_moe_sparse_kernel, tm=tm, final_norm=final_norm),
        out_shape=jax.ShapeDtypeStruct((n, d), F32),
        grid=(n // tm,),
        in_specs=[
            pl.BlockSpec((tm, d), lambda i: (i, 0)),
            pl.BlockSpec((1, d), lambda i: (0, 0)),
            pl.BlockSpec((2, ROUTE_ROWS, d), lambda i: (0, 0, 0)),
            pl.BlockSpec((ROUTE_ROWS, 1), lambda i: (0, 0)),
            pl.BlockSpec((ne, d, de), lambda i: (0, 0, 0), **resident),
            pl.BlockSpec((ne, d, de), lambda i: (0, 0, 0), **resident),
            pl.BlockSpec((N_GROUPS, EXPERTS_PER_GROUP * de, d), lambda i: (0, 0, 0), **resident),
            pl.BlockSpec((1, d), lambda i: (0, 0)),
        ],
        out_specs=pl.BlockSpec((tm, d), lambda i: (i, 0)),
        scratch_shapes=[pltpu.VMEM((rows, d), BF16), pltpu.VMEM((rows, d), BF16),
                        pltpu.VMEM((rows, tm), BF16), pltpu.VMEM((rows, SUBLANES), F32)],
        compiler_params=pltpu.CompilerParams(dimension_semantics=("arbitrary",),
                                             vmem_limit_bytes=MOE_VMEM_LIMIT),
        name="moe_sparse",
    )(h, g, wr2, brt, wg, wu, wd4, gf)


MOE_VMEM_LIMIT = 56 * 1024 * 1024


def kernel(x_prompt, x_sample, state_pool, cache_swa_kv, cache_meta_kv, state_conv, meta_tokens, norm_mix, norm_ffn, norm_final, w_pool, pool_scale, w_qkv, w_o, attn_sinks, w_conv_in, conv_w, w_conv_out, w_group, b_group, w_expert_router, b_expert_router, w_gate, w_up, w_down):
    nb, seq, d = x_prompt.shape
    db, dseq, _ = x_sample.shape
    depth = norm_mix.shape[0]
    kvd = N_KV_HEADS * HEAD_DIM
    tm_main = min(512, seq)
    tq_main = min(256, seq)
    halo = POOL_STATE + 1

    row = lambda a: a.reshape(1, -1)
    wr = jnp.pad(jnp.concatenate([w_group, w_expert_router], axis=-1),
                 ((0, 0), (0, 0), (0, LANES - N_GROUPS - N_EXPERTS)))
    br = jnp.pad(jnp.concatenate([b_group, b_expert_router], axis=-1),
                 ((0, 0), (0, LANES - N_GROUPS - N_EXPERTS)))
    rpad = lambda a, k: jnp.pad(a, ((0, 0), (0, k)) + ((0, 0),) * (a.ndim - 2))
    wrt = jnp.concatenate([rpad(jnp.swapaxes(w_group, 1, 2), SUBLANES - N_GROUPS),
                           rpad(jnp.swapaxes(w_expert_router, 1, 2), ROUTE_ROWS - SUBLANES - N_EXPERTS)], axis=1)
    wr2 = jnp.stack(_split_bf16(wrt.astype(F32), 2), axis=1)
    brt = jnp.concatenate([rpad(b_group, SUBLANES - N_GROUPS),
                           rpad(b_expert_router, ROUTE_ROWS - SUBLANES - N_EXPERTS)], axis=1)[..., None].astype(F32)
    bf = lambda a: a.astype(BF16)
    w_pool_b, w_qkv_b, w_o_b = bf(w_pool), bf(w_qkv), bf(w_o)
    w_conv_in_b, w_conv_out_b = bf(w_conv_in), bf(w_conv_out)
    w_gate_b, w_up_b, w_down_b = bf(w_gate), bf(w_up), bf(w_down)

    hm = meta_tokens.astype(F32)[None]
    hp = x_prompt
    hs = x_sample
    pool_p, swa_p, meta_p, conv_p, pool_s, swa_s, conv_s = [], [], [], [], [], [], []
    y_prompt = y_sample = None
    for i in range(depth):
        j = i // N_MIXERS
        g = row(norm_mix[i])
        if i % N_MIXERS == 0:
            sc = row(pool_scale[j])
            hm, st_m = _pool_mix(hm, jnp.zeros((1, halo, d), F32), False, g, w_pool[j], sc,
                                 tm=N_META, has_history=False, precise=True)
            hp, st_p = _pool_mix(hp, st_m, False, g, w_pool_b[j], sc, tm=tm_main, has_history=True,
                                 precise=False)
            hist_s = jnp.pad(state_pool[j].astype(F32), ((0, 0), (1, 0), (0, 0)))
            hs, st_s = _pool_mix(hs, hist_s, True, g, w_pool[j], sc, tm=dseq, has_history=True,
                                 precise=True)
            pool_p.append(st_p[:, 1:])
            pool_s.append(st_s[:, 1:])
        elif i % N_MIXERS == 1:
            sinks = attn_sinks[j].astype(F32)
            qm, km, vm = _qkv_proj(hm, g, w_qkv[j], jnp.arange(N_META), tm=N_META, precise=True)
            qp, kp, vp = _qkv_proj(hp, g, w_qkv_b[j], N_META + jnp.arange(seq), tm=tm_main, precise=False)
            qs, ks, vs = _qkv_proj(hs, g, w_qkv[j], PAST_LEN + N_META + jnp.arange(dseq), tm=dseq,
                                   precise=True)
            hm = _attn_full(qm, km, vm, sinks, hm, w_o[j], precise=True)
            hp = _attn_band(qp, kp, vp, km[0], vm[0], sinks, hp, w_o_b[j], tq=tq_main)
            flat = lambda a: a.reshape(a.shape[0], a.shape[1], kvd)
            keys = jnp.concatenate([flat(cache_meta_kv[j][:, :, 0]), flat(cache_swa_kv[j][:, :, 0]), ks], axis=1)
            vals = jnp.concatenate([flat(cache_meta_kv[j][:, :, 1]), flat(cache_swa_kv[j][:, :, 1]), vs], axis=1)
            hs = _attn_full(qs, keys, vals, sinks, hs, w_o[j], precise=True)
            heads = lambda a: a.reshape(a.shape[0], a.shape[1], N_KV_HEADS, HEAD_DIM)
            swa_p.append(jnp.stack([heads(kp[:, -WINDOW:]), heads(vp[:, -WINDOW:])], axis=2))
            meta_kv = jnp.stack([heads(km), heads(vm)], axis=2)
            meta_p.append(jnp.broadcast_to(meta_kv, (nb,) + meta_kv.shape[1:]))
            swa_s.append(jnp.stack([heads(ks), heads(vs)], axis=2))
        else:
            cw = conv_w[j].astype(F32)
            hm, st_m = _conv_mix(hm, jnp.zeros((1, SUBLANES, d), F32), False, g, w_conv_in[j], cw,
                                 w_conv_out[j], tm=N_META, precise=True)
            hp, st_p = _conv_mix(hp, st_m, False, g, w_conv_in_b[j], cw, w_conv_out_b[j], tm=tm_main,
                                 precise=False)
            hist_s = jnp.pad(state_conv[j].astype(F32), ((0, 0), (SUBLANES - (CONV_WIDTH - 1), 0), (0, 0)))
            hs, st_s = _conv_mix(hs, hist_s, True, g, w_conv_in[j], cw, w_conv_out[j], tm=dseq, precise=True)
            conv_p.append(st_p[:, SUBLANES - (CONV_WIDTH - 1):])
            conv_s.append(st_s[:, SUBLANES - (CONV_WIDTH - 1):])

        final = i == depth - 1
        gf = row(norm_final)
        gn = row(norm_ffn[i])
        small = jnp.concatenate([hm.reshape(-1, d), hs.reshape(-1, d)], axis=0)
        small = _moe_dense(small, gn, wr[i], br[i:i + 1], w_gate[i], w_up[i], w_down[i], gf,
                           tm=small.shape[0], precise=True, final_norm=final)
        hm = small[:N_META].reshape(1, N_META, d)
        hs = small[N_META:].reshape(db, dseq, d)
        hp = _moe_sparse(hp.reshape(-1, d), gn, wr2[i], brt[i], w_gate_b[i], w_up_b[i],
                         w_down_b[i].reshape(N_GROUPS, -1, d), gf, tm=tm_main,
                         final_norm=final).reshape(nb, seq, d)
    y_prompt, y_sample = hp, hs

    return (y_prompt, y_sample, jnp.stack(pool_p), jnp.stack(swa_p), jnp.stack(meta_p), jnp.stack(conv_p),
            jnp.stack(pool_s), jnp.stack(swa_s), jnp.stack(conv_s))
```

```python
import functools

import jax
import jax.numpy as jnp
from jax import lax
from jax.experimental import pallas as pl
from jax.experimental.pallas import tpu as pltpu

F32 = jnp.float32
BF16 = jnp.bfloat16

CHUNK = 64
N_META = 16
N_MIXERS = 3
POOL_WINDOWS = (2, 4, 8, 16)
POOL_STATE = max(POOL_WINDOWS) - 1
HEAD_DIM = 64
N_KV_HEADS = 4
WINDOW = 128
WIN_CHUNKS = WINDOW // CHUNK
ROPE_THETA = 10000.0
CONV_WIDTH = 3
N_GROUPS = 4
EXPERTS_PER_GROUP = 4
N_EXPERTS = N_GROUPS * EXPERTS_PER_GROUP
PAST_LEN = 2048
EPS = 1e-6

LANES = 128
SUBLANES = 8
VMEM_LIMIT = 48 * 1024 * 1024
NEG_INF = float("-inf")


def _params(*sem):
    return pltpu.CompilerParams(dimension_semantics=sem, vmem_limit_bytes=VMEM_LIMIT)


def _mm(a, b, precise):
    if precise:
        return jnp.dot(a.astype(F32), b.astype(F32), preferred_element_type=F32,
                       precision=lax.Precision.HIGHEST)
    return jnp.dot(a.astype(BF16), b.astype(BF16), preferred_element_type=F32)


def _mm_nt(a, b, precise):
    dn = (((1,), (1,)), ((), ()))
    if precise:
        return lax.dot_general(a.astype(F32), b.astype(F32), dn, preferred_element_type=F32,
                               precision=lax.Precision.HIGHEST)
    return lax.dot_general(a.astype(BF16), b.astype(BF16), dn, preferred_element_type=F32)


def _rms(x, g):
    ms = jnp.mean(x * x, axis=-1, keepdims=True)
    return x * lax.rsqrt(ms + EPS) * g


def _pool_kernel(h_ref, hist_ref, g_ref, w_ref, scale_ref, out_ref, state_ref, buf_ref, *,
                 tm, has_history, precise):
    t = pl.program_id(1)
    halo = POOL_STATE + 1
    pg = h_ref.shape[-1] // len(POOL_WINDOWS)

    @pl.when(t == 0)
    def _():
        buf_ref[0:halo, :] = hist_ref[0]

    h = h_ref[0]
    hn = _rms(h, g_ref[...])
    buf_ref[halo:halo + tm, :] = hn
    if not has_history:
        pos = t * tm + lax.broadcasted_iota(jnp.int32, (tm, 1), 0)
    ys = []
    for gi, w in enumerate(POOL_WINDOWS):
        c0, c1 = gi * pg, (gi + 1) * pg
        win = buf_ref[halo:halo + tm, c0:c1]
        for k in range(1, w):
            win = win + buf_ref[halo - k:halo - k + tm, c0:c1]
        if has_history:
            mean = win * (1.0 / w)
        else:
            mean = win / jnp.minimum(pos + 1, w).astype(F32)
        ys.append(_mm(mean - hn[:, c0:c1], w_ref[gi], precise))
    y = jnp.concatenate(ys, axis=1) * scale_ref[...]
    out_ref[0] = h + y
    tail = buf_ref[tm:tm + halo, :]
    buf_ref[0:halo, :] = tail

    @pl.when(t == pl.num_programs(1) - 1)
    def _():
        state_ref[0] = tail


def _pool_mix(h, hist, hist_per_batch, g, w, scale, *, tm, has_history, precise):
    nb, s, d = h.shape
    halo = POOL_STATE + 1
    pg = d // len(POOL_WINDOWS)
    hist_map = (lambda b, t: (b, 0, 0)) if hist_per_batch else (lambda b, t: (0, 0, 0))
    out, state = pl.pallas_call(
        functools.partial(_pool_kernel, tm=tm, has_history=has_history, precise=precise),
        out_shape=(jax.ShapeDtypeStruct((nb, s, d), F32), jax.ShapeDtypeStruct((nb, halo, d), F32)),
        grid=(nb, s // tm),
        in_specs=[
            pl.BlockSpec((1, tm, d), lambda b, t: (b, t, 0)),
            pl.BlockSpec((1, halo, d), hist_map),
            pl.BlockSpec((1, d), lambda b, t: (0, 0)),
            pl.BlockSpec((len(POOL_WINDOWS), pg, pg), lambda b, t: (0, 0, 0)),
            pl.BlockSpec((1, d), lambda b, t: (0, 0)),
        ],
        out_specs=(pl.BlockSpec((1, tm, d), lambda b, t: (b, t, 0)),
                   pl.BlockSpec((1, halo, d), lambda b, t: (b, 0, 0))),
        scratch_shapes=[pltpu.VMEM((tm + halo, d), F32)],
        compiler_params=_params("arbitrary", "arbitrary"),
        name="pool_mix",
    )(h, hist, g, w, scale)
    return out, state


def _conv_kernel(h_ref, hist_ref, g_ref, win_ref, cw_ref, wout_ref, out_ref, state_ref, buf_ref, *,
                 tm, precise):
    t = pl.program_id(1)
    d = h_ref.shape[-1]

    @pl.when(t == 0)
    def _():
        buf_ref[0:SUBLANES, :] = hist_ref[0]

    h = h_ref[0]
    hn = _rms(h, g_ref[...])
    z = _mm(hn, win_ref[...], precise)
    gate_b = z[:, 0:d]
    buf_ref[SUBLANES:SUBLANES + tm, :] = z[:, d:2 * d] * z[:, 2 * d:3 * d]
    acc = buf_ref[SUBLANES - 2:SUBLANES - 2 + tm, :] * cw_ref[0:1, :]
    for k in range(1, CONV_WIDTH):
        acc = acc + buf_ref[SUBLANES - 2 + k:SUBLANES - 2 + k + tm, :] * cw_ref[k:k + 1, :]
    out_ref[0] = h + _mm(gate_b * acc, wout_ref[...], precise)
    tail = buf_ref[tm:tm + SUBLANES, :]
    buf_ref[0:SUBLANES, :] = tail

    @pl.when(t == pl.num_programs(1) - 1)
    def _():
        state_ref[0] = tail


def _conv_mix(h, hist, hist_per_batch, g, w_in, cw, w_out, *, tm, precise):
    nb, s, d = h.shape
    hist_map = (lambda b, t: (b, 0, 0)) if hist_per_batch else (lambda b, t: (0, 0, 0))
    out, state = pl.pallas_call(
        functools.partial(_conv_kernel, tm=tm, precise=precise),
        out_shape=(jax.ShapeDtypeStruct((nb, s, d), F32), jax.ShapeDtypeStruct((nb, SUBLANES, d), F32)),
        grid=(nb, s // tm),
        in_specs=[
            pl.BlockSpec((1, tm, d), lambda b, t: (b, t, 0)),
            pl.BlockSpec((1, SUBLANES, d), hist_map),
            pl.BlockSpec((1, d), lambda b, t: (0, 0)),
            pl.BlockSpec((d, 3 * d), lambda b, t: (0, 0)),
            pl.BlockSpec((CONV_WIDTH, d), lambda b, t: (0, 0)),
            pl.BlockSpec((d, d), lambda b, t: (0, 0)),
        ],
        out_specs=(pl.BlockSpec((1, tm, d), lambda b, t: (b, t, 0)),
                   pl.BlockSpec((1, SUBLANES, d), lambda b, t: (b, 0, 0))),
        scratch_shapes=[pltpu.VMEM((tm + SUBLANES, d), F32)],
        compiler_params=_params("arbitrary", "arbitrary"),
        name="conv_mix",
    )(h, hist, g, w_in, cw, w_out)
    return out, state


def _qkv_kernel(h_ref, g_ref, w_ref, cos_ref, sin_ref, q_ref, k_ref, v_ref, *, precise):
    d = h_ref.shape[-1]
    kvd = k_ref.shape[-1]
    hn = _rms(h_ref[0], g_ref[...])
    z = _mm(hn, w_ref[...], precise)
    cos = cos_ref[...]
    sin = sin_ref[...]
    lane = lax.broadcasted_iota(jnp.int32, (1, LANES), 1)
    first_half = (lane % HEAD_DIM) < (HEAD_DIM // 2)

    def rope(blk):
        partner = jnp.where(first_half, pltpu.roll(blk, LANES - HEAD_DIM // 2, 1),
                            pltpu.roll(blk, HEAD_DIM // 2, 1))
        return blk * cos + partner * sin

    for j in range(d // LANES):
        q_ref[0, :, j * LANES:(j + 1) * LANES] = rope(z[:, j * LANES:(j + 1) * LANES]).astype(q_ref.dtype)
    for j in range(kvd // LANES):
        k_ref[0, :, j * LANES:(j + 1) * LANES] = rope(z[:, d + j * LANES:d + (j + 1) * LANES])
    v_ref[0] = z[:, d + kvd:d + 2 * kvd]


def _rope_tables(pos):
    half = HEAD_DIM // 2
    inv = ROPE_THETA ** (-jnp.arange(half, dtype=F32) / half)
    ang = pos.astype(F32)[:, None] * inv[None, :]
    cos, sin = jnp.cos(ang), jnp.sin(ang)
    reps = LANES // HEAD_DIM
    return (jnp.tile(jnp.concatenate([cos, cos], axis=1), (1, reps)),
            jnp.tile(jnp.concatenate([-sin, sin], axis=1), (1, reps)))


def _qkv_proj(h, g, w_qkv, pos, *, tm, precise):
    nb, s, d = h.shape
    kvd = N_KV_HEADS * HEAD_DIM
    cos, sin = _rope_tables(pos)
    qdt = F32 if precise else BF16
    return pl.pallas_call(
        functools.partial(_qkv_kernel, precise=precise),
        out_shape=(jax.ShapeDtypeStruct((nb, s, d), qdt), jax.ShapeDtypeStruct((nb, s, kvd), F32),
                   jax.ShapeDtypeStruct((nb, s, kvd), F32)),
        grid=(nb, s // tm),
        in_specs=[
            pl.BlockSpec((1, tm, d), lambda b, t: (b, t, 0)),
            pl.BlockSpec((1, d), lambda b, t: (0, 0)),
            pl.BlockSpec((d, d + 2 * kvd), lambda b, t: (0, 0)),
            pl.BlockSpec((tm, LANES), lambda b, t: (t, 0)),
            pl.BlockSpec((tm, LANES), lambda b, t: (t, 0)),
        ],
        out_specs=(pl.BlockSpec((1, tm, d), lambda b, t: (b, t, 0)),
                   pl.BlockSpec((1, tm, kvd), lambda b, t: (b, t, 0)),
                   pl.BlockSpec((1, tm, kvd), lambda b, t: (b, t, 0))),
        compiler_params=_params("arbitrary", "arbitrary"),
        name="qkv_proj",
    )(h, g, w_qkv, cos, sin)


def _attend(qh, kb, vb, sink, valid, precise):
    s = _mm_nt(qh, kb, precise) * (HEAD_DIM ** -0.5)
    if valid is not None:
        s = jnp.where(valid, s, NEG_INF)
    m = jnp.maximum(jnp.max(s, axis=-1, keepdims=True), sink)
    p = jnp.exp(s - m)
    denom = jnp.sum(p, axis=-1, keepdims=True) + jnp.exp(sink - m)
    return _mm(p, vb, precise) / denom


def _sink_column(sinks_ref, hk, rows):
    gqa = sinks_ref.shape[0] // N_KV_HEADS
    return jnp.concatenate([jnp.full((rows, 1), sinks_ref[hk * gqa + g], F32) for g in range(gqa)], axis=0)


QT_TILE = 128
KPAD = LANES


def _qkv_t_kernel(h_ref, g_ref, wqt_ref, wk_ref, wvt_ref, wv_ref, cos_ref, sin_ref, cost_ref, sint_ref,
                  qt_ref, kpad_ref, vt_ref, kst_ref, vst_ref):
    t = pl.program_id(1)
    tm = h_ref.shape[1]
    half = HEAD_DIM // 2
    hb = _rms(h_ref[0], g_ref[...]).astype(BF16)
    zq = _mm_nt(wqt_ref[...], hb, False)
    cost, sint = cost_ref[...], sint_ref[...]
    for hd in range(zq.shape[0] // HEAD_DIM):
        x1 = zq[hd * HEAD_DIM:hd * HEAD_DIM + half]
        x2 = zq[hd * HEAD_DIM + half:(hd + 1) * HEAD_DIM]
        qt_ref[0, hd * HEAD_DIM:hd * HEAD_DIM + half, :] = (x1 * cost - x2 * sint).astype(BF16)
        qt_ref[0, hd * HEAD_DIM + half:(hd + 1) * HEAD_DIM, :] = (x2 * cost + x1 * sint).astype(BF16)
    vt_ref[0] = _mm_nt(wvt_ref[...], hb, False).astype(BF16)

    zk = jnp.dot(hb, wk_ref[...], preferred_element_type=F32)
    lane = lax.broadcasted_iota(jnp.int32, (1, LANES), 1)
    first_half = (lane % HEAD_DIM) < half
    kr = []
    for j in range(zk.shape[1] // LANES):
        blk = zk[:, j * LANES:(j + 1) * LANES]
        partner = jnp.where(first_half, pltpu.roll(blk, LANES - half, 1), pltpu.roll(blk, half, 1))
        blk = blk * cos_ref[...] + partner * sin_ref[...]
        kr.append(blk)
        for sub in range(LANES // HEAD_DIM):
            hk = j * (LANES // HEAD_DIM) + sub
            shifted = blk if sub == 0 else pltpu.roll(blk, LANES - sub * HEAD_DIM, 1)
            kpad_ref[0, :, hk * KPAD:(hk + 1) * KPAD] = jnp.where(lane < HEAD_DIM, shifted, 0.0).astype(BF16)

    @pl.when(t == pl.num_programs(1) - 1)
    def _():
        kst_ref[0] = jnp.concatenate(kr, axis=1)[tm - WINDOW:]
        vst_ref[0] = jnp.dot(hb[tm - WINDOW:], wv_ref[...], preferred_element_type=F32)


def _qkv_t(h, g, w_qkv, pos, *, tm):
    nb, s, d = h.shape
    kvd = N_KV_HEADS * HEAD_DIM
    half = HEAD_DIM // 2
    cos, sin = _rope_tables(pos)
    inv = ROPE_THETA ** (-jnp.arange(half, dtype=F32) / half)
    ang = inv[:, None] * pos.astype(F32)[None, :]
    cost, sint = jnp.cos(ang), jnp.sin(ang)
    wq_t = (w_qkv[:, :d] * (HEAD_DIM ** -0.5)).T.astype(BF16)
    wk = w_qkv[:, d:d + kvd].astype(BF16)
    wv = w_qkv[:, d + kvd:].astype(BF16)
    const = lambda shape: pl.BlockSpec(shape, lambda b, t: (0,) * len(shape))
    return pl.pallas_call(
        _qkv_t_kernel,
        out_shape=(jax.ShapeDtypeStruct((nb, d, s), BF16),
                   jax.ShapeDtypeStruct((nb, s, N_KV_HEADS * KPAD), BF16),
                   jax.ShapeDtypeStruct((nb, kvd, s), BF16),
                   jax.ShapeDtypeStruct((nb, WINDOW, kvd), F32),
                   jax.ShapeDtypeStruct((nb, WINDOW, kvd), F32)),
        grid=(nb, s // tm),
        in_specs=[
            pl.BlockSpec((1, tm, d), lambda b, t: (b, t, 0)),
            const((1, d)), const((d, d)), const((d, kvd)), const((kvd, d)), const((d, kvd)),
            pl.BlockSpec((tm, LANES), lambda b, t: (t, 0)),
            pl.BlockSpec((tm, LANES), lambda b, t: (t, 0)),
            pl.BlockSpec((half, tm), lambda b, t: (0, t)),
            pl.BlockSpec((half, tm), lambda b, t: (0, t)),
        ],
        out_specs=(pl.BlockSpec((1, d, tm), lambda b, t: (b, 0, t)),
                   pl.BlockSpec((1, tm, N_KV_HEADS * KPAD), lambda b, t: (b, t, 0)),
                   pl.BlockSpec((1, kvd, tm), lambda b, t: (b, 0, t)),
                   pl.BlockSpec((1, WINDOW, kvd), lambda b, t: (b, 0, 0)),
                   pl.BlockSpec((1, WINDOW, kvd), lambda b, t: (b, 0, 0))),
        compiler_params=_params("arbitrary", "arbitrary"),
        name="qkv_t",
    )(h, g, wq_t, wk, wv.T, wv, cos, sin, cost, sint)


def _attn_t_kernel(sinks_ref, qt_ref, kc_ref, kp_ref, vc_ref, vp_ref, mk_ref, mvt_ref, h_ref, wo_ref,
                   out_ref, ot_ref, *, tq):
    t = pl.program_id(1)
    gqa = qt_ref.shape[1] // (N_KV_HEADS * HEAD_DIM)
    band = QT_TILE + WIN_CHUNKS * CHUNK
    kk = jnp.concatenate([kp_ref[0], kc_ref[0]], axis=0)
    vv = jnp.concatenate([vp_ref[0], vc_ref[0]], axis=1)
    r = lax.broadcasted_iota(jnp.int32, (band, gqa * QT_TILE), 0)
    ln = lax.broadcasted_iota(jnp.int32, (band, gqa * QT_TILE), 1)
    kchunk = r // CHUNK
    qchunk = (ln // CHUNK) % (QT_TILE // CHUNK)
    visible = (kchunk >= qchunk) & (kchunk <= qchunk + WIN_CHUNKS)
    lgroup = lax.broadcasted_iota(jnp.int32, (1, gqa * QT_TILE), 1) // QT_TILE
    for sub in range(tq // QT_TILE):
        c0 = sub * QT_TILE
        if sub == 0:
            mask = visible & ((t > 0) | (r >= WIN_CHUNKS * CHUNK))
        else:
            mask = visible
        krows = kk[c0:c0 + band]
        vcols = vv[:, c0:c0 + band]
        for hk in range(N_KV_HEADS):
            q4 = jnp.concatenate(
                [qt_ref[0, (hk * gqa + g) * HEAD_DIM:(hk * gqa + g + 1) * HEAD_DIM, c0:c0 + QT_TILE]
                 for g in range(gqa)], axis=1)
            sb = jnp.dot(krows[:, hk * KPAD:hk * KPAD + HEAD_DIM], q4, preferred_element_type=F32)
            sm = jnp.dot(mk_ref[:, hk * KPAD:hk * KPAD + HEAD_DIM], q4, preferred_element_type=F32)
            sb = jnp.where(mask, sb, NEG_INF)
            sink = jnp.zeros((1, gqa * QT_TILE), F32)
            for g in range(gqa):
                sink = jnp.where(lgroup == g, sinks_ref[hk * gqa + g], sink)
            m = jnp.maximum(jnp.maximum(jnp.max(sb, axis=0, keepdims=True),
                                        jnp.max(sm, axis=0, keepdims=True)), sink)
            pb = jnp.exp(sb - m)
            pm = jnp.exp(sm - m)
            denom = (jnp.sum(pb, axis=0, keepdims=True) + jnp.sum(pm, axis=0, keepdims=True)
                     + jnp.exp(sink - m))
            hs = slice(hk * HEAD_DIM, (hk + 1) * HEAD_DIM)
            o = (jnp.dot(vcols[hs], pb.astype(BF16), preferred_element_type=F32)
                 + jnp.dot(mvt_ref[hs, :], pm.astype(BF16), preferred_element_type=F32)) / denom
            for g in range(gqa):
                ot_ref[(hk * gqa + g) * HEAD_DIM:(hk * gqa + g + 1) * HEAD_DIM, c0:c0 + QT_TILE] = (
                    o[:, g * QT_TILE:(g + 1) * QT_TILE].astype(BF16))
    proj = lax.dot_general(ot_ref[...], wo_ref[...], (((0,), (0,)), ((), ())), preferred_element_type=F32)
    out_ref[0] = h_ref[0] + proj


def _attn_t(qt, kpad, vt, mkpad, mvt, sinks, h, w_o, *, tq):
    nb, s, d = h.shape
    kvd = vt.shape[1]
    prev = WIN_CHUNKS * CHUNK
    ratio = tq // prev
    const = lambda shape: pl.BlockSpec(shape, lambda b, t: (0,) * len(shape))
    return pl.pallas_call(
        functools.partial(_attn_t_kernel, tq=tq),
        out_shape=jax.ShapeDtypeStruct((nb, s, d), F32),
        grid=(nb, s // tq),
        in_specs=[
            pl.BlockSpec(memory_space=pltpu.SMEM),
            pl.BlockSpec((1, d, tq), lambda b, t: (b, 0, t)),
            pl.BlockSpec((1, tq, N_KV_HEADS * KPAD), lambda b, t: (b, t, 0)),
            pl.BlockSpec((1, prev, N_KV_HEADS * KPAD), lambda b, t: (b, jnp.maximum(t * ratio - 1, 0), 0)),
            pl.BlockSpec((1, kvd, tq), lambda b, t: (b, 0, t)),
            pl.BlockSpec((1, kvd, prev), lambda b, t: (b, 0, jnp.maximum(t * ratio - 1, 0))),
            const((N_META, N_KV_HEADS * KPAD)), const((kvd, N_META)),
            pl.BlockSpec((1, tq, d), lambda b, t: (b, t, 0)),
            const((d, d)),
        ],
        out_specs=pl.BlockSpec((1, tq, d), lambda b, t: (b, t, 0)),
        scratch_shapes=[pltpu.VMEM((d, tq), BF16)],
        compiler_params=_params("arbitrary", "arbitrary"),
        name="attn_t",
    )(sinks, qt, kpad, kpad, vt, vt, mkpad, mvt, h, w_o)


def _attn_full_kernel(sinks_ref, q_ref, k_ref, v_ref, h_ref, wo_ref, out_ref, o_ref, *, precise):
    s = q_ref.shape[1]
    gqa = q_ref.shape[-1] // (N_KV_HEADS * HEAD_DIM)
    for hk in range(N_KV_HEADS):
        hs = slice(hk * HEAD_DIM, (hk + 1) * HEAD_DIM)
        qh = jnp.concatenate(
            [q_ref[0, :, (hk * gqa + g) * HEAD_DIM:(hk * gqa + g + 1) * HEAD_DIM] for g in range(gqa)], axis=0)
        o = _attend(qh, k_ref[0, :, hs], v_ref[0, :, hs], _sink_column(sinks_ref, hk, s), None, precise)
        for g in range(gqa):
            o_ref[:, (hk * gqa + g) * HEAD_DIM:(hk * gqa + g + 1) * HEAD_DIM] = o[g * s:(g + 1) * s]
    out_ref[0] = h_ref[0] + _mm(o_ref[...], wo_ref[...], precise)


def _attn_full(q, keys, vals, sinks, h, w_o, *, precise):
    nb, s, d = h.shape
    kn, kvd = keys.shape[1:]
    return pl.pallas_call(
        functools.partial(_attn_full_kernel, precise=precise),
        out_shape=jax.ShapeDtypeStruct((nb, s, d), F32),
        grid=(nb,),
        in_specs=[
            pl.BlockSpec(memory_space=pltpu.SMEM),
            pl.BlockSpec((1, s, d), lambda b: (b, 0, 0)),
            pl.BlockSpec((1, kn, kvd), lambda b: (b, 0, 0)),
            pl.BlockSpec((1, kn, kvd), lambda b: (b, 0, 0)),
            pl.BlockSpec((1, s, d), lambda b: (b, 0, 0)),
            pl.BlockSpec((d, d), lambda b: (0, 0)),
        ],
        out_specs=pl.BlockSpec((1, s, d), lambda b: (b, 0, 0)),
        scratch_shapes=[pltpu.VMEM((s, d), F32)],
        compiler_params=_params("arbitrary"),
        name="attn_full",
    )(sinks, q, keys, vals, h, w_o)


def _route(logits):
    col = lax.broadcasted_iota(jnp.int32, logits.shape, 1)
    is_group = col < N_GROUPS
    lg = jnp.where(is_group, logits, NEG_INF)
    gmax = jnp.max(lg, axis=-1, keepdims=True)
    g_idx = jnp.min(jnp.where(lg == gmax, col, LANES), axis=-1, keepdims=True)
    g_w = 1.0 / jnp.sum(jnp.exp(lg - gmax), axis=-1, keepdims=True)
    ecol = col - N_GROUPS
    in_group = (ecol >= 0) & (ecol < N_EXPERTS) & ((ecol // EXPERTS_PER_GROUP) == g_idx)
    le = jnp.where(in_group, logits, NEG_INF)
    m1 = jnp.max(le, axis=-1, keepdims=True)
    i1 = jnp.min(jnp.where(in_group & (le == m1), col, LANES), axis=-1, keepdims=True)
    rest = in_group & (col != i1)
    le2 = jnp.where(rest, logits, NEG_INF)
    m2 = jnp.max(le2, axis=-1, keepdims=True)
    i2 = jnp.min(jnp.where(rest & (le2 == m2), col, LANES), axis=-1, keepdims=True)
    e2 = jnp.exp(m2 - m1)
    p1 = 1.0 / (1.0 + e2)
    return g_w * (jnp.where(col == i1, p1, 0.0) + jnp.where(col == i2, e2 * p1, 0.0))


def _moe_dense_kernel(h_ref, g_ref, wr_ref, br_ref, wg_ref, wu_ref, wd_ref, gf_ref, out_ref,
                      hn_ref, comb_ref, acc_ref, *, precise, final_norm):
    e = pl.program_id(1)

    @pl.when(e == 0)
    def _():
        hn = _rms(h_ref[...], g_ref[...])
        hn_ref[...] = hn.astype(hn_ref.dtype)
        logits = jnp.dot(hn, wr_ref[...], preferred_element_type=F32,
                         precision=lax.Precision.HIGHEST) + br_ref[...]
        comb_ref[...] = _route(logits)
        acc_ref[...] = jnp.zeros_like(acc_ref)

    x = hn_ref[...]
    col = lax.broadcasted_iota(jnp.int32, comb_ref.shape, 1)
    c = jnp.sum(jnp.where(col == e + N_GROUPS, comb_ref[...], 0.0), axis=-1, keepdims=True)
    gate = _mm(x, wg_ref[0], precise)
    up = _mm(x, wu_ref[0], precise)
    act = gate * jax.nn.sigmoid(gate) * up * c
    acc_ref[...] += _mm(act, wd_ref[0], precise)

    @pl.when(e == pl.num_programs(1) - 1)
    def _():
        y = h_ref[...] + acc_ref[...]
        if final_norm:
            y = _rms(y, gf_ref[...])
        out_ref[...] = y


def _moe_dense(h, g, wr, br, wg, wu, wd, gf, *, tm, precise, final_norm):
    n, d = h.shape
    ne, _, de = wg.shape
    return pl.pallas_call(
        functools.partial(_moe_dense_kernel, precise=precise, final_norm=final_norm),
        out_shape=jax.ShapeDtypeStruct((n, d), F32),
        grid=(n // tm, ne),
        in_specs=[
            pl.BlockSpec((tm, d), lambda i, e: (i, 0)),
            pl.BlockSpec((1, d), lambda i, e: (0, 0)),
            pl.BlockSpec((d, LANES), lambda i, e: (0, 0)),
            pl.BlockSpec((1, LANES), lambda i, e: (0, 0)),
            pl.BlockSpec((1, d, de), lambda i, e: (e, 0, 0)),
            pl.BlockSpec((1, d, de), lambda i, e: (e, 0, 0)),
            pl.BlockSpec((1, de, d), lambda i, e: (e, 0, 0)),
            pl.BlockSpec((1, d), lambda i, e: (0, 0)),
        ],
        out_specs=pl.BlockSpec((tm, d), lambda i, e: (i, 0)),
        scratch_shapes=[pltpu.VMEM((tm, d), F32 if precise else BF16), pltpu.VMEM((tm, LANES), F32),
                        pltpu.VMEM((tm, d), F32)],
        compiler_params=_params("arbitrary", "arbitrary"),
        name="moe_dense",
    )(h, g, wr, br, wg, wu, wd, gf)


MOE_CHUNK = 128
SEG_ALIGN = 16
ROUTE_ROWS = 32


def _split_bf16(x, parts):
    out = []
    for _ in range(parts):
        hi = x.astype(BF16)
        out.append(hi)
        x = x - hi.astype(F32)
    return out


def _route_t(lt, tm):
    row8 = lax.broadcasted_iota(jnp.int32, (SUBLANES, tm), 0)
    lg = jnp.where(row8 < N_GROUPS, lt[0:SUBLANES], NEG_INF)
    gmax = jnp.max(lg, axis=0, keepdims=True)
    g_idx = jnp.min(jnp.where(lg == gmax, row8, SUBLANES), axis=0, keepdims=True)
    g_w = 1.0 / jnp.sum(jnp.exp(lg - gmax), axis=0, keepdims=True)
    le = lt[SUBLANES:SUBLANES + N_EXPERTS]
    row16 = lax.broadcasted_iota(jnp.int32, (N_EXPERTS, tm), 0)
    in_group = (row16 // EXPERTS_PER_GROUP) == g_idx
    l1 = jnp.where(in_group, le, NEG_INF)
    m1 = jnp.max(l1, axis=0, keepdims=True)
    i1 = jnp.min(jnp.where(in_group & (l1 == m1), row16, N_EXPERTS), axis=0, keepdims=True)
    rest = in_group & (row16 != i1)
    l2 = jnp.where(rest, le, NEG_INF)
    m2 = jnp.max(l2, axis=0, keepdims=True)
    i2 = jnp.min(jnp.where(rest & (l2 == m2), row16, N_EXPERTS), axis=0, keepdims=True)
    e2 = jnp.exp(m2 - m1)
    p1 = 1.0 / (1.0 + e2)
    comb = g_w * (jnp.where(row16 == i1, p1, 0.0) + jnp.where(row16 == i2, e2 * p1, 0.0))
    c8 = comb[0:SUBLANES] + comb[SUBLANES:2 * SUBLANES]
    return g_idx, c8 + pltpu.roll(c8, EXPERTS_PER_GROUP, 0)


def _moe_sparse_kernel(h_ref, g_ref, wr_ref, br_ref, wg_ref, wu_ref, wd_ref, gf_ref, out_ref,
                       xs_ref, ys_ref, p_ref, cs_ref, *, tm, final_norm):
    rows = xs_ref.shape[0]
    h = h_ref[...]
    hn = _rms(h, g_ref[...])
    hn_hi, hn_lo = _split_bf16(hn, 2)
    lt = (_mm_nt(wr_ref[0], hn_hi, False) + _mm_nt(wr_ref[0], hn_lo, False)
          + _mm_nt(wr_ref[1], hn_hi, False) + br_ref[...])
    g_idx, comb4 = _route_t(lt, tm)

    row8 = lax.broadcasted_iota(jnp.int32, (SUBLANES, tm), 0)
    lane = lax.broadcasted_iota(jnp.int32, (SUBLANES, tm), 1)
    onehot = (row8 == g_idx).astype(F32)
    incl = onehot
    shift = 1
    while shift < tm:
        incl = incl + jnp.where(lane >= shift, pltpu.roll(incl, shift, 1), 0.0)
        shift *= 2
    counts = incl[:, tm - 1:tm].astype(jnp.int32)
    n = [counts[g, 0] for g in range(N_GROUPS)]
    starts = [jnp.int32(0)]
    for g in range(N_GROUPS - 1):
        starts.append(starts[-1] + (n[g] + SEG_ALIGN - 1) // SEG_ALIGN * SEG_ALIGN)
    rank = jnp.sum(onehot * incl, axis=0, keepdims=True).astype(jnp.int32) - 1
    start_tok = jnp.zeros_like(g_idx)
    for g in range(1, N_GROUPS):
        start_tok = jnp.where(g_idx == g, starts[g], start_tok)
    pos = start_tok + rank
    riota = lax.broadcasted_iota(jnp.int32, (rows, tm), 0)
    p_ref[...] = jnp.where(riota == pos, 1.0, 0.0).astype(BF16)

    xs_ref[...] = jnp.dot(p_ref[...], hn_hi, preferred_element_type=F32).astype(BF16)
    comb_parts = jnp.concatenate(_split_bf16(comb4, 3), axis=0)
    cs = _mm_nt(p_ref[...], comb_parts, False)
    cs_ref[...] = cs[:, 0:SUBLANES] + cs[:, SUBLANES:2 * SUBLANES] + cs[:, 2 * SUBLANES:3 * SUBLANES]
    ys_ref[...] = jnp.zeros_like(ys_ref)

    for g in range(N_GROUPS):
        def chunk(c, carry, g=g):
            r0 = pl.multiple_of(starts[g] + c * MOE_CHUNK, SEG_ALIGN)
            x = xs_ref[pl.ds(r0, MOE_CHUNK), :]
            cc = cs_ref[pl.ds(r0, MOE_CHUNK), :]
            acts = []
            for j in range(EXPERTS_PER_GROUP):
                e = g * EXPERTS_PER_GROUP + j
                gate = jnp.dot(x, wg_ref[e], preferred_element_type=F32)
                up = jnp.dot(x, wu_ref[e], preferred_element_type=F32)
                acts.append((gate * jax.nn.sigmoid(gate) * up * cc[:, j:j + 1]).astype(BF16))
            y = jnp.dot(jnp.concatenate(acts, axis=1), wd_ref[g], preferred_element_type=F32)
            ys_ref[pl.ds(r0, MOE_CHUNK), :] = y.astype(BF16)
            return carry

        lax.fori_loop(0, (n[g] + MOE_CHUNK - 1) // MOE_CHUNK, chunk, 0)

    back = lax.dot_general(p_ref[...], ys_ref[...], (((0,), (0,)), ((), ())), preferred_element_type=F32)
    y = h + back
    if final_norm:
        y = _rms(y, gf_ref[...])
    out_ref[...] = y


def _moe_sparse(h, g, wr2, brt, wg, wu, wd4, gf, *, tm, final_norm):
    n, d = h.shape
    ne, _, de = wg.shape
    rows = tm + N_GROUPS * SEG_ALIGN + MOE_CHUNK
    resident = dict(pipeline_mode=pl.Buffered(1))
    return pl.pallas_call(
        functools.partial(_moe_sparse_kernel, tm=tm, final_norm=final_norm),
        out_shape=jax.ShapeDtypeStruct((n, d), F32),
        grid=(n // tm,),
        in_specs=[
            pl.BlockSpec((tm, d), lambda i: (i, 0)),
            pl.BlockSpec((1, d), lambda i: (0, 0)),
            pl.BlockSpec((2, ROUTE_ROWS, d), lambda i: (0, 0, 0)),
            pl.BlockSpec((ROUTE_ROWS, 1), lambda i: (0, 0)),
            pl.BlockSpec((ne, d, de), lambda i: (0, 0, 0), **resident),
            pl.BlockSpec((ne, d, de), lambda i: (0, 0, 0), **resident),
            pl.BlockSpec((N_GROUPS, EXPERTS_PER_GROUP * de, d), lambda i: (0, 0, 0), **resident),
            pl.BlockSpec((1, d), lambda i: (0, 0)),
        ],
        out_specs=pl.BlockSpec((tm, d), lambda i: (i, 0)),
        scratch_shapes=[pltpu.VMEM((rows, d), BF16), pltpu.VMEM((rows, d), BF16),
                        pltpu.VMEM((rows, tm), BF16), pltpu.VMEM((rows, SUBLANES), F32)],
        compiler_params=pltpu.CompilerParams(dimension_semantics=("arbitrary",),
                                             vmem_limit_bytes=MOE_VMEM_LIMIT),
        name="moe_sparse",
    )(h, g, wr2, brt, wg, wu, wd4, gf)


MOE_VMEM_LIMIT = 56 * 1024 * 1024


def kernel(x_prompt, x_sample, state_pool, cache_swa_kv, cache_meta_kv, state_conv, meta_tokens, norm_mix, norm_ffn, norm_final, w_pool, pool_scale, w_qkv, w_o, attn_sinks, w_conv_in, conv_w, w_conv_out, w_group, b_group, w_expert_router, b_expert_router, w_gate, w_up, w_down):
    nb, seq, d = x_prompt.shape
    db, dseq, _ = x_sample.shape
    depth = norm_mix.shape[0]
    kvd = N_KV_HEADS * HEAD_DIM
    tm_main = min(512, seq)
    tq_main = min(256, seq)
    halo = POOL_STATE + 1

    row = lambda a: a.reshape(1, -1)
    wr = jnp.pad(jnp.concatenate([w_group, w_expert_router], axis=-1),
                 ((0, 0), (0, 0), (0, LANES - N_GROUPS - N_EXPERTS)))
    br = jnp.pad(jnp.concatenate([b_group, b_expert_router], axis=-1),
                 ((0, 0), (0, LANES - N_GROUPS - N_EXPERTS)))
    rpad = lambda a, k: jnp.pad(a, ((0, 0), (0, k)) + ((0, 0),) * (a.ndim - 2))
    wrt = jnp.concatenate([rpad(jnp.swapaxes(w_group, 1, 2), SUBLANES - N_GROUPS),
                           rpad(jnp.swapaxes(w_expert_router, 1, 2), ROUTE_ROWS - SUBLANES - N_EXPERTS)], axis=1)
    wr2 = jnp.stack(_split_bf16(wrt.astype(F32), 2), axis=1)
    brt = jnp.concatenate([rpad(b_group, SUBLANES - N_GROUPS),
                           rpad(b_expert_router, ROUTE_ROWS - SUBLANES - N_EXPERTS)], axis=1)[..., None].astype(F32)
    bf = lambda a: a.astype(BF16)
    w_pool_b, w_o_b = bf(w_pool), bf(w_o)
    w_conv_in_b, w_conv_out_b = bf(w_conv_in), bf(w_conv_out)
    w_gate_b, w_up_b, w_down_b = bf(w_gate), bf(w_up), bf(w_down)

    hm = meta_tokens.astype(F32)[None]
    hp = x_prompt
    hs = x_sample
    pool_p, swa_p, meta_p, conv_p, pool_s, swa_s, conv_s = [], [], [], [], [], [], []
    y_prompt = y_sample = None
    for i in range(depth):
        j = i // N_MIXERS
        g = row(norm_mix[i])
        if i % N_MIXERS == 0:
            sc = row(pool_scale[j])
            hm, st_m = _pool_mix(hm, jnp.zeros((1, halo, d), F32), False, g, w_pool[j], sc,
                                 tm=N_META, has_history=False, precise=True)
            hp, st_p = _pool_mix(hp, st_m, False, g, w_pool_b[j], sc, tm=tm_main, has_history=True,
                                 precise=False)
            hist_s = jnp.pad(state_pool[j].astype(F32), ((0, 0), (1, 0), (0, 0)))
            hs, st_s = _pool_mix(hs, hist_s, True, g, w_pool[j], sc, tm=dseq, has_history=True,
                                 precise=True)
            pool_p.append(st_p[:, 1:])
            pool_s.append(st_s[:, 1:])
        elif i % N_MIXERS == 1:
            sinks = attn_sinks[j].astype(F32)
            qm, km, vm = _qkv_proj(hm, g, w_qkv[j], jnp.arange(N_META), tm=N_META, precise=True)
            qt, kpad, vt, kst, vst = _qkv_t(hp, g, w_qkv[j], N_META + jnp.arange(seq), tm=tm_main)
            qs, ks, vs = _qkv_proj(hs, g, w_qkv[j], PAST_LEN + N_META + jnp.arange(dseq), tm=dseq,
                                   precise=True)
            hm = _attn_full(qm, km, vm, sinks, hm, w_o[j], precise=True)
            mkpad = jnp.pad(km[0].reshape(N_META, N_KV_HEADS, HEAD_DIM),
                            ((0, 0), (0, 0), (0, KPAD - HEAD_DIM))).reshape(N_META, -1).astype(BF16)
            hp = _attn_t(qt, kpad, vt, mkpad, vm[0].T.astype(BF16), sinks, hp, w_o_b[j], tq=tq_main)
            flat = lambda a: a.reshape(a.shape[0], a.shape[1], kvd)
            keys = jnp.concatenate([flat(cache_meta_kv[j][:, :, 0]), flat(cache_swa_kv[j][:, :, 0]), ks], axis=1)
            vals = jnp.concatenate([flat(cache_meta_kv[j][:, :, 1]), flat(cache_swa_kv[j][:, :, 1]), vs], axis=1)
            hs = _attn_full(qs, keys, vals, sinks, hs, w_o[j], precise=True)
            heads = lambda a: a.reshape(a.shape[0], a.shape[1], N_KV_HEADS, HEAD_DIM)
            swa_p.append(jnp.stack([heads(kst), heads(vst)], axis=2))
            meta_kv = jnp.stack([heads(km), heads(vm)], axis=2)
            meta_p.append(jnp.broadcast_to(meta_kv, (nb,) + meta_kv.shape[1:]))
            swa_s.append(jnp.stack([heads(ks), heads(vs)], axis=2))
        else:
            cw = conv_w[j].astype(F32)
            hm, st_m = _conv_mix(hm, jnp.zeros((1, SUBLANES, d), F32), False, g, w_conv_in[j], cw,
                                 w_conv_out[j], tm=N_META, precise=True)
            hp, st_p = _conv_mix(hp, st_m, False, g, w_conv_in_b[j], cw, w_conv_out_b[j], tm=tm_main,
                                 precise=False)
            hist_s = jnp.pad(state_conv[j].astype(F32), ((0, 0), (SUBLANES - (CONV_WIDTH - 1), 0), (0, 0)))
            hs, st_s = _conv_mix(hs, hist_s, True, g, w_conv_in[j], cw, w_conv_out[j], tm=dseq, precise=True)
            conv_p.append(st_p[:, SUBLANES - (CONV_WIDTH - 1):])
            conv_s.append(st_s[:, SUBLANES - (CONV_WIDTH - 1):])

        final = i == depth - 1
        gf = row(norm_final)
        gn = row(norm_ffn[i])
        small = jnp.concatenate([hm.reshape(-1, d), hs.reshape(-1, d)], axis=0)
        small = _moe_dense(small, gn, wr[i], br[i:i + 1], w_gate[i], w_up[i], w_down[i], gf,
                           tm=small.shape[0], precise=True, final_norm=final)
        hm = small[:N_META].reshape(1, N_META, d)
        hs = small[N_META:].reshape(db, dseq, d)
        hp = _moe_sparse(hp.reshape(-1, d), gn, wr2[i], brt[i], w_gate_b[i], w_up_b[i],
                         w_down_b[i].reshape(N_GROUPS, -1, d), gf, tm=tm_main,
                         final_norm=final).reshape(nb, seq, d)
    y_prompt, y_sample = hp, hs

    return (y_prompt, y_sample, jnp.stack(pool_p), jnp.stack(swa_p), jnp.stack(meta_p), jnp.stack(conv_p),
            jnp.stack(pool_s), jnp.stack(swa_s), jnp.stack(conv_s))
```

```python
import functools

import jax
import jax.numpy as jnp
from jax import lax
from jax.experimental import pallas as pl
from jax.experimental.pallas import tpu as pltpu

F32 = jnp.float32
BF16 = jnp.bfloat16

CHUNK = 64
N_META = 16
N_MIXERS = 3
POOL_WINDOWS = (2, 4, 8, 16)
POOL_STATE = max(POOL_WINDOWS) - 1
HEAD_DIM = 64
N_KV_HEADS = 4
WINDOW = 128
WIN_CHUNKS = WINDOW // CHUNK
ROPE_THETA = 10000.0
CONV_WIDTH = 3
N_GROUPS = 4
EXPERTS_PER_GROUP = 4
N_EXPERTS = N_GROUPS * EXPERTS_PER_GROUP
PAST_LEN = 2048
EPS = 1e-6

LANES = 128
SUBLANES = 8
VMEM_LIMIT = 48 * 1024 * 1024
MOE_VMEM_LIMIT = 56 * 1024 * 1024
NEG_INF = float("-inf")


def _params(*sem):
    return pltpu.CompilerParams(dimension_semantics=sem, vmem_limit_bytes=VMEM_LIMIT)


def _split_bf16(x, parts):
    out = []
    x = x.astype(F32)
    for _ in range(parts):
        hi = x.astype(BF16)
        out.append(hi)
        x = x - hi.astype(F32)
    return out


def _dot(a, b, dn):
    return lax.dot_general(a, b, dn, preferred_element_type=F32)


def _mm_dn(a, b, dn, precise):
    if not precise:
        return _dot(a.astype(BF16), b.astype(BF16), dn)
    a_hi, a_lo = _split_bf16(a, 2)
    b_hi, b_lo = _split_bf16(b, 2)
    return _dot(a_hi, b_hi, dn) + (_dot(a_hi, b_lo, dn) + _dot(a_lo, b_hi, dn))


def _mm(a, b, precise):
    return _mm_dn(a, b, (((1,), (0,)), ((), ())), precise)


def _mm_nt(a, b, precise):
    return _mm_dn(a, b, (((1,), (1,)), ((), ())), precise)


def _rms(x, g):
    ms = jnp.mean(x * x, axis=-1, keepdims=True)
    return x * lax.rsqrt(ms + EPS) * g


def _layer_spec(shape, layer):
    nd = len(shape)
    return pl.BlockSpec((1,) + tuple(shape), lambda *_: (layer,) + (0,) * nd)


POOL_HALO = 16
POOL_LEAD = 16


def _pool_kernel(h_ref, hist_ref, g_ref, w_ref, scale_ref, out_ref, state_ref, buf_ref, sa_ref, sb_ref, *,
                 tm, has_history, precise):
    t = pl.program_id(1)
    base = POOL_LEAD + POOL_HALO
    bb, _, d = h_ref.shape
    pg = d // len(POOL_WINDOWS)
    end = base + tm

    @pl.when(t == 0)
    def _():
        buf_ref[:, 0:POOL_LEAD, :] = jnp.zeros((bb, POOL_LEAD, d), F32)
        buf_ref[:, POOL_LEAD:base, :] = jnp.broadcast_to(hist_ref[...], (bb, POOL_HALO, d))

    h = h_ref[...]
    hn = _rms(h, g_ref[...])
    buf_ref[:, base:end, :] = hn
    if not has_history:
        pos = t * tm + lax.broadcasted_iota(jnp.int32, (1, tm, 1), 1)
    ys = []
    for gi, w in enumerate(POOL_WINDOWS):
        c0, c1 = gi * pg, (gi + 1) * pg
        src, dst, shift, lo = buf_ref, sa_ref, 1, SUBLANES
        while shift < w:
            last = 2 * shift == w
            lo_k = base if last else lo
            s = src[:, lo_k:end, c0:c1] + src[:, lo_k - shift:end - shift, c0:c1]
            if last:
                win = s
            else:
                dst[:, lo_k:end, c0:c1] = s
                src, dst = dst, (sb_ref if dst is sa_ref else sa_ref)
            shift, lo = 2 * shift, lo + SUBLANES
        if has_history:
            mean = win * (1.0 / w)
        else:
            mean = win / jnp.minimum(pos + 1, w).astype(F32)
        ys.append(_mm((mean - hn[:, :, c0:c1]).reshape(bb * tm, pg), w_ref[0, gi], precise))
    y = jnp.concatenate(ys, axis=1).reshape(bb, tm, d) * scale_ref[...]
    out_ref[...] = h + y
    tail = buf_ref[:, end - POOL_HALO:end, :]
    buf_ref[:, POOL_LEAD:base, :] = tail

    @pl.when(t == pl.num_programs(1) - 1)
    def _():
        state_ref[...] = tail


def _pool_mix(h, hist, g, w, layer, scale, *, bb, tm, has_history, precise):
    nb, s, d = h.shape
    halo = POOL_HALO
    pg = d // len(POOL_WINDOWS)
    hist_map = (lambda b, t: (b, 0, 0)) if hist.shape[0] == nb and nb > 1 else (lambda b, t: (0, 0, 0))
    hb = bb if hist.shape[0] == nb and nb > 1 else 1
    out, state = pl.pallas_call(
        functools.partial(_pool_kernel, tm=tm, has_history=has_history, precise=precise),
        out_shape=(jax.ShapeDtypeStruct((nb, s, d), F32), jax.ShapeDtypeStruct((nb, halo, d), F32)),
        grid=(nb // bb, s // tm),
        in_specs=[
            pl.BlockSpec((bb, tm, d), lambda b, t: (b, t, 0)),
            pl.BlockSpec((hb, halo, d), hist_map),
            pl.BlockSpec((1, d), lambda b, t: (0, 0)),
            _layer_spec((len(POOL_WINDOWS), pg, pg), layer),
            pl.BlockSpec((1, d), lambda b, t: (0, 0)),
        ],
        out_specs=(pl.BlockSpec((bb, tm, d), lambda b, t: (b, t, 0)),
                   pl.BlockSpec((bb, halo, d), lambda b, t: (b, 0, 0))),
        scratch_shapes=[pltpu.VMEM((bb, POOL_LEAD + halo + tm, d), F32)] * 3,
        compiler_params=_params("arbitrary", "arbitrary"),
        name="pool_mix",
    )(h, hist, g, w, scale)
    return out, state


def _conv_kernel(h_ref, hist_ref, g_ref, win_ref, cw_ref, wout_ref, out_ref, state_ref, buf_ref, *,
                 tm, precise):
    t = pl.program_id(1)
    bb, _, d = h_ref.shape

    @pl.when(t == 0)
    def _():
        buf_ref[:, 0:SUBLANES, :] = jnp.broadcast_to(hist_ref[...], (bb, SUBLANES, d))

    h = h_ref[...]
    hn = _rms(h, g_ref[...])
    z = _mm(hn.reshape(bb * tm, d), win_ref[0], precise)
    gate_b = z[:, 0:d]
    buf_ref[:, SUBLANES:SUBLANES + tm, :] = (z[:, d:2 * d] * z[:, 2 * d:3 * d]).reshape(bb, tm, d)
    first = SUBLANES - (CONV_WIDTH - 1)
    acc = buf_ref[:, first:first + tm, :] * cw_ref[0:1, :]
    for k in range(1, CONV_WIDTH):
        acc = acc + buf_ref[:, first + k:first + k + tm, :] * cw_ref[k:k + 1, :]
    y = _mm(gate_b * acc.reshape(bb * tm, d), wout_ref[0], precise)
    out_ref[...] = h + y.reshape(bb, tm, d)
    tail = buf_ref[:, tm:tm + SUBLANES, :]
    buf_ref[:, 0:SUBLANES, :] = tail

    @pl.when(t == pl.num_programs(1) - 1)
    def _():
        state_ref[...] = tail


def _conv_mix(h, hist, g, w_in, layer, cw, w_out, *, bb, tm, precise):
    nb, s, d = h.shape
    per_batch = hist.shape[0] == nb and nb > 1
    hist_map = (lambda b, t: (b, 0, 0)) if per_batch else (lambda b, t: (0, 0, 0))
    out, state = pl.pallas_call(
        functools.partial(_conv_kernel, tm=tm, precise=precise),
        out_shape=(jax.ShapeDtypeStruct((nb, s, d), F32), jax.ShapeDtypeStruct((nb, SUBLANES, d), F32)),
        grid=(nb // bb, s // tm),
        in_specs=[
            pl.BlockSpec((bb, tm, d), lambda b, t: (b, t, 0)),
            pl.BlockSpec((bb if per_batch else 1, SUBLANES, d), hist_map),
            pl.BlockSpec((1, d), lambda b, t: (0, 0)),
            _layer_spec((d, 3 * d), layer),
            pl.BlockSpec((CONV_WIDTH, d), lambda b, t: (0, 0)),
            _layer_spec((d, d), layer),
        ],
        out_specs=(pl.BlockSpec((bb, tm, d), lambda b, t: (b, t, 0)),
                   pl.BlockSpec((bb, SUBLANES, d), lambda b, t: (b, 0, 0))),
        scratch_shapes=[pltpu.VMEM((bb, tm + SUBLANES, d), F32)],
        compiler_params=_params("arbitrary", "arbitrary"),
        name="conv_mix",
    )(h, hist, g, w_in, cw, w_out)
    return out, state


def _rope_tables(pos):
    half = HEAD_DIM // 2
    inv = ROPE_THETA ** (-jnp.arange(half, dtype=F32) / half)
    ang = pos.astype(F32)[:, None] * inv[None, :]
    cos, sin = jnp.cos(ang), jnp.sin(ang)
    reps = LANES // HEAD_DIM
    return (jnp.tile(jnp.concatenate([cos, cos], axis=1), (1, reps)),
            jnp.tile(jnp.concatenate([-sin, sin], axis=1), (1, reps)))


def _rope_block(blk, cos, sin):
    half = HEAD_DIM // 2
    lane = lax.broadcasted_iota(jnp.int32, (1, LANES), 1)
    partner = jnp.where((lane % HEAD_DIM) < half, pltpu.roll(blk, LANES - half, 1), pltpu.roll(blk, half, 1))
    return blk * cos + partner * sin


def _qkv_small_kernel(h_ref, g_ref, w_ref, cos_ref, sin_ref, q_ref, k_ref, v_ref):
    bb, s, d = h_ref.shape
    kvd = k_ref.shape[-1]
    hn = _rms(h_ref[...], g_ref[...]).reshape(bb * s, d)
    z = _mm(hn, w_ref[0], True)
    cos, sin = cos_ref[...], sin_ref[...]
    q = [_rope_block(z[:, j * LANES:(j + 1) * LANES], cos, sin) for j in range(d // LANES)]
    k = [_rope_block(z[:, d + j * LANES:d + (j + 1) * LANES], cos, sin) for j in range(kvd // LANES)]
    q_ref[...] = jnp.concatenate(q, axis=1).reshape(bb, s, d)
    k_ref[...] = jnp.concatenate(k, axis=1).reshape(bb, s, kvd)
    v_ref[...] = z[:, d + kvd:d + 2 * kvd].reshape(bb, s, kvd)


def _qkv_small(h, g, w_qkv, layer, pos):
    nb, s, d = h.shape
    kvd = N_KV_HEADS * HEAD_DIM
    cos, sin = _rope_tables(pos)
    cos, sin = jnp.tile(cos, (nb, 1)), jnp.tile(sin, (nb, 1))
    full = lambda shape: pl.BlockSpec(shape, lambda i: (0,) * len(shape))
    return pl.pallas_call(
        _qkv_small_kernel,
        out_shape=(jax.ShapeDtypeStruct((nb, s, d), F32), jax.ShapeDtypeStruct((nb, s, kvd), F32),
                   jax.ShapeDtypeStruct((nb, s, kvd), F32)),
        grid=(1,),
        in_specs=[full((nb, s, d)), full((1, d)), _layer_spec((d, d + 2 * kvd), layer),
                  full((nb * s, LANES)), full((nb * s, LANES))],
        out_specs=(full((nb, s, d)), full((nb, s, kvd)), full((nb, s, kvd))),
        compiler_params=_params("arbitrary"),
        name="qkv_small",
    )(h, g, w_qkv, cos, sin)


QT_TILE = 128
KPAD = LANES


def _qkv_t_kernel(h_ref, g_ref, wqt_ref, wk_ref, wvt_ref, wv_ref, cos_ref, sin_ref, cost_ref, sint_ref,
                  qt_ref, kpad_ref, vt_ref, kst_ref, vst_ref):
    t = pl.program_id(1)
    tm = h_ref.shape[1]
    half = HEAD_DIM // 2
    hb = _rms(h_ref[0], g_ref[...]).astype(BF16)
    zq = _mm_nt(wqt_ref[...], hb, False)
    cost, sint = cost_ref[...], sint_ref[...]
    for hd in range(zq.shape[0] // HEAD_DIM):
        x1 = zq[hd * HEAD_DIM:hd * HEAD_DIM + half]
        x2 = zq[hd * HEAD_DIM + half:(hd + 1) * HEAD_DIM]
        qt_ref[0, hd * HEAD_DIM:hd * HEAD_DIM + half, :] = (x1 * cost - x2 * sint).astype(BF16)
        qt_ref[0, hd * HEAD_DIM + half:(hd + 1) * HEAD_DIM, :] = (x2 * cost + x1 * sint).astype(BF16)
    vt_ref[0] = _mm_nt(wvt_ref[...], hb, False).astype(BF16)

    zk = jnp.dot(hb, wk_ref[...], preferred_element_type=F32)
    lane = lax.broadcasted_iota(jnp.int32, (1, LANES), 1)
    kr = []
    for j in range(zk.shape[1] // LANES):
        blk = _rope_block(zk[:, j * LANES:(j + 1) * LANES], cos_ref[...], sin_ref[...])
        kr.append(blk)
        for sub in range(LANES // HEAD_DIM):
            hk = j * (LANES // HEAD_DIM) + sub
            shifted = blk if sub == 0 else pltpu.roll(blk, LANES - sub * HEAD_DIM, 1)
            kpad_ref[0, :, hk * KPAD:(hk + 1) * KPAD] = jnp.where(lane < HEAD_DIM, shifted, 0.0).astype(BF16)

    @pl.when(t == pl.num_programs(1) - 1)
    def _():
        kst_ref[0] = jnp.concatenate(kr, axis=1)[tm - WINDOW:]
        vst_ref[0] = jnp.dot(hb[tm - WINDOW:], wv_ref[...], preferred_element_type=F32)


def _qkv_t(h, g, w_qkv, pos, *, tm):
    nb, s, d = h.shape
    kvd = N_KV_HEADS * HEAD_DIM
    half = HEAD_DIM // 2
    cos, sin = _rope_tables(pos)
    inv = ROPE_THETA ** (-jnp.arange(half, dtype=F32) / half)
    ang = inv[:, None] * pos.astype(F32)[None, :]
    cost, sint = jnp.cos(ang), jnp.sin(ang)
    wq_t = (w_qkv[:, :d] * (HEAD_DIM ** -0.5)).T.astype(BF16)
    wk = w_qkv[:, d:d + kvd].astype(BF16)
    wv = w_qkv[:, d + kvd:].astype(BF16)
    const = lambda shape: pl.BlockSpec(shape, lambda b, t: (0,) * len(shape))
    return pl.pallas_call(
        _qkv_t_kernel,
        out_shape=(jax.ShapeDtypeStruct((nb, d, s), BF16),
                   jax.ShapeDtypeStruct((nb, s, N_KV_HEADS * KPAD), BF16),
                   jax.ShapeDtypeStruct((nb, kvd, s), BF16),
                   jax.ShapeDtypeStruct((nb, WINDOW, kvd), F32),
                   jax.ShapeDtypeStruct((nb, WINDOW, kvd), F32)),
        grid=(nb, s // tm),
        in_specs=[
            pl.BlockSpec((1, tm, d), lambda b, t: (b, t, 0)),
            const((1, d)), const((d, d)), const((d, kvd)), const((kvd, d)), const((d, kvd)),
            pl.BlockSpec((tm, LANES), lambda b, t: (t, 0)),
            pl.BlockSpec((tm, LANES), lambda b, t: (t, 0)),
            pl.BlockSpec((half, tm), lambda b, t: (0, t)),
            pl.BlockSpec((half, tm), lambda b, t: (0, t)),
        ],
        out_specs=(pl.BlockSpec((1, d, tm), lambda b, t: (b, 0, t)),
                   pl.BlockSpec((1, tm, N_KV_HEADS * KPAD), lambda b, t: (b, t, 0)),
                   pl.BlockSpec((1, kvd, tm), lambda b, t: (b, 0, t)),
                   pl.BlockSpec((1, WINDOW, kvd), lambda b, t: (b, 0, 0)),
                   pl.BlockSpec((1, WINDOW, kvd), lambda b, t: (b, 0, 0))),
        compiler_params=_params("arbitrary", "arbitrary"),
        name="qkv_t",
    )(h, g, wq_t, wk, wv.T, wv, cos, sin, cost, sint)


def _attn_t_kernel(sinks_ref, qt_ref, kc_ref, kp_ref, vc_ref, vp_ref, mk_ref, mvt_ref, h_ref, wo_ref,
                   out_ref, ot_ref, *, tq):
    t = pl.program_id(1)
    gqa = qt_ref.shape[1] // (N_KV_HEADS * HEAD_DIM)
    band = QT_TILE + WIN_CHUNKS * CHUNK
    kk = jnp.concatenate([kp_ref[0], kc_ref[0]], axis=0)
    vv = jnp.concatenate([vp_ref[0], vc_ref[0]], axis=1)
    r = lax.broadcasted_iota(jnp.int32, (band, gqa * QT_TILE), 0)
    ln = lax.broadcasted_iota(jnp.int32, (band, gqa * QT_TILE), 1)
    kchunk = r // CHUNK
    qchunk = (ln // CHUNK) % (QT_TILE // CHUNK)
    visible = (kchunk >= qchunk) & (kchunk <= qchunk + WIN_CHUNKS)
    lgroup = lax.broadcasted_iota(jnp.int32, (1, gqa * QT_TILE), 1) // QT_TILE
    for sub in range(tq // QT_TILE):
        c0 = sub * QT_TILE
        if sub == 0:
            mask = visible & ((t > 0) | (r >= WIN_CHUNKS * CHUNK))
        else:
            mask = visible
        krows = kk[c0:c0 + band]
        vcols = vv[:, c0:c0 + band]
        for hk in range(N_KV_HEADS):
            q4 = jnp.concatenate(
                [qt_ref[0, (hk * gqa + g) * HEAD_DIM:(hk * gqa + g + 1) * HEAD_DIM, c0:c0 + QT_TILE]
                 for g in range(gqa)], axis=1)
            sb = jnp.dot(krows[:, hk * KPAD:hk * KPAD + HEAD_DIM], q4, preferred_element_type=F32)
            sm = jnp.dot(mk_ref[:, hk * KPAD:hk * KPAD + HEAD_DIM], q4, preferred_element_type=F32)
            sb = jnp.where(mask, sb, NEG_INF)
            sink = jnp.zeros((1, gqa * QT_TILE), F32)
            for g in range(gqa):
                sink = jnp.where(lgroup == g, sinks_ref[hk * gqa + g], sink)
            m = jnp.maximum(jnp.maximum(jnp.max(sb, axis=0, keepdims=True),
                                        jnp.max(sm, axis=0, keepdims=True)), sink)
            pb = jnp.exp(sb - m)
            pm = jnp.exp(sm - m)
            denom = (jnp.sum(pb, axis=0, keepdims=True) + jnp.sum(pm, axis=0, keepdims=True)
                     + jnp.exp(sink - m))
            hs = slice(hk * HEAD_DIM, (hk + 1) * HEAD_DIM)
            o = (jnp.dot(vcols[hs], pb.astype(BF16), preferred_element_type=F32)
                 + jnp.dot(mvt_ref[hs, :], pm.astype(BF16), preferred_element_type=F32)) / denom
            for g in range(gqa):
                ot_ref[(hk * gqa + g) * HEAD_DIM:(hk * gqa + g + 1) * HEAD_DIM, c0:c0 + QT_TILE] = (
                    o[:, g * QT_TILE:(g + 1) * QT_TILE].astype(BF16))
    proj = lax.dot_general(ot_ref[...], wo_ref[0], (((0,), (0,)), ((), ())), preferred_element_type=F32)
    out_ref[0] = h_ref[0] + proj


def _attn_t(qt, kpad, vt, mkpad, mvt, sinks, h, w_o, layer, *, tq):
    nb, s, d = h.shape
    kvd = vt.shape[1]
    prev = WIN_CHUNKS * CHUNK
    ratio = tq // prev
    const = lambda shape: pl.BlockSpec(shape, lambda b, t: (0,) * len(shape))
    return pl.pallas_call(
        functools.partial(_attn_t_kernel, tq=tq),
        out_shape=jax.ShapeDtypeStruct((nb, s, d), F32),
        grid=(nb, s // tq),
        in_specs=[
            pl.BlockSpec(memory_space=pltpu.SMEM),
            pl.BlockSpec((1, d, tq), lambda b, t: (b, 0, t)),
            pl.BlockSpec((1, tq, N_KV_HEADS * KPAD), lambda b, t: (b, t, 0)),
            pl.BlockSpec((1, prev, N_KV_HEADS * KPAD), lambda b, t: (b, jnp.maximum(t * ratio - 1, 0), 0)),
            pl.BlockSpec((1, kvd, tq), lambda b, t: (b, 0, t)),
            pl.BlockSpec((1, kvd, prev), lambda b, t: (b, 0, jnp.maximum(t * ratio - 1, 0))),
            const((N_META, N_KV_HEADS * KPAD)), const((kvd, N_META)),
            pl.BlockSpec((1, tq, d), lambda b, t: (b, t, 0)),
            _layer_spec((d, d), layer),
        ],
        out_specs=pl.BlockSpec((1, tq, d), lambda b, t: (b, t, 0)),
        scratch_shapes=[pltpu.VMEM((d, tq), BF16)],
        compiler_params=_params("arbitrary", "arbitrary"),
        name="attn_t",
    )(sinks, qt, kpad, kpad, vt, vt, mkpad, mvt, h, w_o)


def _attn_small_kernel(sinks_ref, q_ref, k_ref, v_ref, h_ref, wo_ref, out_ref, o_ref):
    bb, s, d = q_ref.shape
    gqa = d // (N_KV_HEADS * HEAD_DIM)
    for b in range(bb):
        for hk in range(N_KV_HEADS):
            hs = slice(hk * HEAD_DIM, (hk + 1) * HEAD_DIM)
            qh = jnp.concatenate(
                [q_ref[b, :, (hk * gqa + g) * HEAD_DIM:(hk * gqa + g + 1) * HEAD_DIM] for g in range(gqa)],
                axis=0)
            sink = jnp.concatenate([jnp.full((s, 1), sinks_ref[hk * gqa + g], F32) for g in range(gqa)], axis=0)
            sc = _mm_nt(qh, k_ref[b, :, hs], True) * (HEAD_DIM ** -0.5)
            m = jnp.maximum(jnp.max(sc, axis=-1, keepdims=True), sink)
            p = jnp.exp(sc - m)
            denom = jnp.sum(p, axis=-1, keepdims=True) + jnp.exp(sink - m)
            o = _mm(p, v_ref[b, :, hs], True) / denom
            for g in range(gqa):
                o_ref[b * s:(b + 1) * s, (hk * gqa + g) * HEAD_DIM:(hk * gqa + g + 1) * HEAD_DIM] = (
                    o[g * s:(g + 1) * s])
    out_ref[...] = h_ref[...] + _mm(o_ref[...], wo_ref[0], True).reshape(bb, s, d)


def _attn_small(q, keys, vals, sinks, h, w_o, layer):
    nb, s, d = h.shape
    kn, kvd = keys.shape[1:]
    full = lambda shape: pl.BlockSpec(shape, lambda i: (0,) * len(shape))
    return pl.pallas_call(
        _attn_small_kernel,
        out_shape=jax.ShapeDtypeStruct((nb, s, d), F32),
        grid=(1,),
        in_specs=[pl.BlockSpec(memory_space=pltpu.SMEM), full((nb, s, d)), full((nb, kn, kvd)),
                  full((nb, kn, kvd)), full((nb, s, d)), _layer_spec((d, d), layer)],
        out_specs=full((nb, s, d)),
        scratch_shapes=[pltpu.VMEM((nb * s, d), F32)],
        compiler_params=_params("arbitrary"),
        name="attn_small",
    )(sinks, q, keys, vals, h, w_o)


ROUTE_ROWS = 32
MOE_CHUNK = 128
SEG_ALIGN = 16


def _route_t(lt):
    n = lt.shape[1]
    row8 = lax.broadcasted_iota(jnp.int32, (SUBLANES, n), 0)
    lg = jnp.where(row8 < N_GROUPS, lt[0:SUBLANES], NEG_INF)
    gmax = jnp.max(lg, axis=0, keepdims=True)
    g_idx = jnp.min(jnp.where(lg == gmax, row8, SUBLANES), axis=0, keepdims=True)
    g_w = 1.0 / jnp.sum(jnp.exp(lg - gmax), axis=0, keepdims=True)
    le = lt[SUBLANES:SUBLANES + N_EXPERTS]
    row16 = lax.broadcasted_iota(jnp.int32, (N_EXPERTS, n), 0)
    in_group = (row16 // EXPERTS_PER_GROUP) == g_idx
    l1 = jnp.where(in_group, le, NEG_INF)
    m1 = jnp.max(l1, axis=0, keepdims=True)
    i1 = jnp.min(jnp.where(in_group & (l1 == m1), row16, N_EXPERTS), axis=0, keepdims=True)
    rest = in_group & (row16 != i1)
    l2 = jnp.where(rest, le, NEG_INF)
    m2 = jnp.max(l2, axis=0, keepdims=True)
    i2 = jnp.min(jnp.where(rest & (l2 == m2), row16, N_EXPERTS), axis=0, keepdims=True)
    e2 = jnp.exp(m2 - m1)
    p1 = 1.0 / (1.0 + e2)
    comb = g_w * (jnp.where(row16 == i1, p1, 0.0) + jnp.where(row16 == i2, e2 * p1, 0.0))
    c8 = comb[0:SUBLANES] + comb[SUBLANES:2 * SUBLANES]
    return g_idx, comb, c8 + pltpu.roll(c8, EXPERTS_PER_GROUP, 0)


def _route(logits):
    col = lax.broadcasted_iota(jnp.int32, logits.shape, 1)
    lg = jnp.where(col < N_GROUPS, logits, NEG_INF)
    gmax = jnp.max(lg, axis=-1, keepdims=True)
    g_idx = jnp.min(jnp.where(lg == gmax, col, LANES), axis=-1, keepdims=True)
    g_w = 1.0 / jnp.sum(jnp.exp(lg - gmax), axis=-1, keepdims=True)
    ecol = col - N_GROUPS
    in_group = (ecol >= 0) & (ecol < N_EXPERTS) & ((ecol // EXPERTS_PER_GROUP) == g_idx)
    l1 = jnp.where(in_group, logits, NEG_INF)
    m1 = jnp.max(l1, axis=-1, keepdims=True)
    i1 = jnp.min(jnp.where(in_group & (l1 == m1), col, LANES), axis=-1, keepdims=True)
    rest = in_group & (col != i1)
    l2 = jnp.where(rest, logits, NEG_INF)
    m2 = jnp.max(l2, axis=-1, keepdims=True)
    i2 = jnp.min(jnp.where(rest & (l2 == m2), col, LANES), axis=-1, keepdims=True)
    e2 = jnp.exp(m2 - m1)
    p1 = 1.0 / (1.0 + e2)
    return g_w * (jnp.where(col == i1, p1, 0.0) + jnp.where(col == i2, e2 * p1, 0.0))


def _router_logits_t(wr_ref, br_ref, hn_hi, hn_lo):
    a = _dot(wr_ref[0], hn_hi, (((1,), (1,)), ((), ())))
    b = _dot(wr_ref[0, 0:ROUTE_ROWS], hn_lo, (((1,), (1,)), ((), ())))
    return a[0:ROUTE_ROWS] + a[ROUTE_ROWS:2 * ROUTE_ROWS] + b + br_ref[0]


def _moe_small_kernel(h_ref, g_ref, wr_ref, br_ref, wg_ref, wu_ref, wd_ref, gf_ref, out_ref,
                      hn_ref, comb_ref, acc_ref, *, final_norm):
    e = pl.program_id(0)

    @pl.when(e == 0)
    def _():
        hn = _rms(h_ref[...], g_ref[...])
        hn_ref[0], hn_ref[1] = _split_bf16(hn, 2)
        comb_ref[...] = _route(_mm(hn, wr_ref[0], True) + br_ref[0])
        acc_ref[...] = jnp.zeros_like(acc_ref)

    def x_times(w):
        w_hi, w_lo = _split_bf16(w, 2)
        dn = (((1,), (0,)), ((), ()))
        return _dot(hn_ref[0], w_hi, dn) + (_dot(hn_ref[0], w_lo, dn) + _dot(hn_ref[1], w_hi, dn))

    col = lax.broadcasted_iota(jnp.int32, comb_ref.shape, 1)
    c = jnp.sum(jnp.where(col == e + N_GROUPS, comb_ref[...], 0.0), axis=-1, keepdims=True)
    gate = x_times(wg_ref[0, 0])
    up = x_times(wu_ref[0, 0])
    act = gate * jax.nn.sigmoid(gate) * up * c
    acc_ref[...] += _mm(act, wd_ref[0, 0], True)

    @pl.when(e == pl.num_programs(0) - 1)
    def _():
        y = h_ref[...] + acc_ref[...]
        if final_norm:
            y = _rms(y, gf_ref[...])
        out_ref[...] = y


def _moe_small(h, g, wr, br, wg, wu, wd, layer, gf, *, final_norm):
    n, d = h.shape
    _, ne, _, de = wg.shape
    const = lambda shape: pl.BlockSpec(shape, lambda e: (0,) * len(shape))
    return pl.pallas_call(
        functools.partial(_moe_small_kernel, final_norm=final_norm),
        out_shape=jax.ShapeDtypeStruct((n, d), F32),
        grid=(ne,),
        in_specs=[
            const((n, d)), const((1, d)),
            _layer_spec((d, LANES), layer), _layer_spec((1, LANES), layer),
            pl.BlockSpec((1, 1, d, de), lambda e: (layer, e, 0, 0)),
            pl.BlockSpec((1, 1, d, de), lambda e: (layer, e, 0, 0)),
            pl.BlockSpec((1, 1, de, d), lambda e: (layer, e, 0, 0)),
            const((1, d)),
        ],
        out_specs=const((n, d)),
        scratch_shapes=[pltpu.VMEM((2, n, d), BF16), pltpu.VMEM((n, LANES), F32), pltpu.VMEM((n, d), F32)],
        compiler_params=_params("arbitrary"),
        name="moe_small",
    )(h, g, wr, br, wg, wu, wd, gf)


def _moe_sparse_kernel(h_ref, g_ref, wr_ref, br_ref, tri_ref, wg_ref, wu_ref, wd_ref, gf_ref, out_ref,
                       xs_ref, ys_ref, p_ref, cs_ref, *, tm, final_norm):
    rows = xs_ref.shape[0]
    h = h_ref[...]
    hn = _rms(h, g_ref[...])
    hn_hi, hn_lo = _split_bf16(hn, 2)
    g_idx, _, comb4 = _route_t(_router_logits_t(wr_ref, br_ref, hn_hi, hn_lo))

    row8 = lax.broadcasted_iota(jnp.int32, (SUBLANES, tm), 0)
    onehot = row8 == g_idx
    incl = jnp.dot(jnp.where(onehot, 1.0, 0.0).astype(BF16), tri_ref[...], preferred_element_type=F32)
    counts = incl[:, tm - 1:tm].astype(jnp.int32)
    n = [counts[g, 0] for g in range(N_GROUPS)]
    starts = [jnp.int32(0)]
    for g in range(N_GROUPS - 1):
        starts.append(starts[-1] + (n[g] + SEG_ALIGN - 1) // SEG_ALIGN * SEG_ALIGN)
    rank = jnp.sum(jnp.where(onehot, incl, 0.0), axis=0, keepdims=True).astype(jnp.int32) - 1
    start_tok = jnp.zeros_like(g_idx)
    for g in range(1, N_GROUPS):
        start_tok = jnp.where(g_idx == g, starts[g], start_tok)
    pos = start_tok + rank
    riota = lax.broadcasted_iota(jnp.int32, (rows, tm), 0)
    p_ref[...] = jnp.where(riota == pos, 1.0, 0.0).astype(BF16)

    xs_ref[...] = jnp.dot(p_ref[...], hn_hi, preferred_element_type=F32).astype(BF16)
    comb_parts = jnp.concatenate(_split_bf16(comb4, 3), axis=0)
    cs = _dot(p_ref[...], comb_parts, (((1,), (1,)), ((), ())))
    cs_ref[...] = cs[:, 0:SUBLANES] + cs[:, SUBLANES:2 * SUBLANES] + cs[:, 2 * SUBLANES:3 * SUBLANES]
    ys_ref[...] = jnp.zeros_like(ys_ref)

    for g in range(N_GROUPS):
        def chunk(c, carry, g=g):
            r0 = pl.multiple_of(starts[g] + c * MOE_CHUNK, SEG_ALIGN)
            x = xs_ref[pl.ds(r0, MOE_CHUNK), :]
            cc = cs_ref[pl.ds(r0, MOE_CHUNK), :]
            acts = []
            for j in range(EXPERTS_PER_GROUP):
                e = g * EXPERTS_PER_GROUP + j
                gate = jnp.dot(x, wg_ref[0, e], preferred_element_type=F32)
                up = jnp.dot(x, wu_ref[0, e], preferred_element_type=F32)
                acts.append((gate * jax.nn.sigmoid(gate) * up * cc[:, j:j + 1]).astype(BF16))
            y = jnp.dot(jnp.concatenate(acts, axis=1), wd_ref[0, g], preferred_element_type=F32)
            ys_ref[pl.ds(r0, MOE_CHUNK), :] = y.astype(BF16)
            return carry

        lax.fori_loop(0, (n[g] + MOE_CHUNK - 1) // MOE_CHUNK, chunk, 0)

    back = lax.dot_general(p_ref[...], ys_ref[...], (((0,), (0,)), ((), ())), preferred_element_type=F32)
    y = h + back
    if final_norm:
        y = _rms(y, gf_ref[...])
    out_ref[...] = y


def _moe_sparse(h, g, wr2, brt, tri, wg, wu, wd4, layer, gf, *, tm, final_norm):
    n, d = h.shape
    _, ne, _, de = wg.shape
    rows = tm + N_GROUPS * SEG_ALIGN + MOE_CHUNK
    resident = lambda shape: pl.BlockSpec((1,) + shape, lambda i: (layer,) + (0,) * len(shape),
                                          pipeline_mode=pl.Buffered(1))
    const = lambda shape: pl.BlockSpec(shape, lambda i: (0,) * len(shape))
    return pl.pallas_call(
        functools.partial(_moe_sparse_kernel, tm=tm, final_norm=final_norm),
        out_shape=jax.ShapeDtypeStruct((n, d), F32),
        grid=(n // tm,),
        in_specs=[
            pl.BlockSpec((tm, d), lambda i: (i, 0)), const((1, d)),
            _layer_spec((2 * ROUTE_ROWS, d), layer), _layer_spec((ROUTE_ROWS, 1), layer),
            const((tm, tm)),
            resident((ne, d, de)), resident((ne, d, de)), resident((N_GROUPS, EXPERTS_PER_GROUP * de, d)),
            const((1, d)),
        ],
        out_specs=pl.BlockSpec((tm, d), lambda i: (i, 0)),
        scratch_shapes=[pltpu.VMEM((rows, d), BF16), pltpu.VMEM((rows, d), BF16),
                        pltpu.VMEM((rows, tm), BF16), pltpu.VMEM((rows, SUBLANES), F32)],
        compiler_params=pltpu.CompilerParams(dimension_semantics=("arbitrary",),
                                             vmem_limit_bytes=MOE_VMEM_LIMIT),
        name="moe_sparse",
    )(h, g, wr2, brt, tri, wg, wu, wd4, gf)


def kernel(x_prompt, x_sample, state_pool, cache_swa_kv, cache_meta_kv, state_conv, meta_tokens, norm_mix, norm_ffn, norm_final, w_pool, pool_scale, w_qkv, w_o, attn_sinks, w_conv_in, conv_w, w_conv_out, w_group, b_group, w_expert_router, b_expert_router, w_gate, w_up, w_down):
    nb, seq, d = x_prompt.shape
    db, dseq, _ = x_sample.shape
    depth = norm_mix.shape[0]
    kvd = N_KV_HEADS * HEAD_DIM
    tm_main = min(512, seq)
    tq_main = min(256, seq)
    halo = POOL_STATE + 1

    row = lambda a: a.reshape(1, -1).astype(F32)
    rpad = lambda a, k: jnp.pad(a, ((0, 0), (0, k)) + ((0, 0),) * (a.ndim - 2))
    wrt = jnp.concatenate([rpad(jnp.swapaxes(w_group, 1, 2), SUBLANES - N_GROUPS),
                           rpad(jnp.swapaxes(w_expert_router, 1, 2), ROUTE_ROWS - SUBLANES - N_EXPERTS)], axis=1)
    wr2 = jnp.concatenate(_split_bf16(wrt, 2), axis=1)
    brt = jnp.concatenate([rpad(b_group, SUBLANES - N_GROUPS),
                           rpad(b_expert_router, ROUTE_ROWS - SUBLANES - N_EXPERTS)], axis=1)[..., None].astype(F32)
    wr = jnp.pad(jnp.concatenate([w_group, w_expert_router], axis=-1).astype(F32),
                 ((0, 0), (0, 0), (0, LANES - N_GROUPS - N_EXPERTS)))
    br = jnp.pad(jnp.concatenate([b_group, b_expert_router], axis=-1).astype(F32),
                 ((0, 0), (0, LANES - N_GROUPS - N_EXPERTS)))[:, None, :]
    tri = jnp.triu(jnp.ones((tm_main, tm_main), BF16))
    bf = lambda a: a.astype(BF16)
    w_pool_b, w_o_b = bf(w_pool), bf(w_o)
    w_conv_in_b, w_conv_out_b = bf(w_conv_in), bf(w_conv_out)
    w_gate_b, w_up_b = bf(w_gate), bf(w_up)
    w_down_b = bf(w_down).reshape(depth, N_GROUPS, -1, d)

    hm = meta_tokens.astype(F32)[None]
    hp = x_prompt
    hs = x_sample
    pool_p, swa_p, meta_p, conv_p, pool_s, swa_s, conv_s = [], [], [], [], [], [], []
    for i in range(depth):
        j = i // N_MIXERS
        g = row(norm_mix[i])
        if i % N_MIXERS == 0:
            sc = row(pool_scale[j])
            hm, st_m = _pool_mix(hm, jnp.zeros((1, halo, d), F32), g, w_pool, j, sc, bb=1, tm=N_META,
                                 has_history=False, precise=True)
            hp, st_p = _pool_mix(hp, st_m, g, w_pool_b, j, sc, bb=1, tm=tm_main, has_history=True,
                                 precise=False)
            hist_s = jnp.pad(state_pool[j].astype(F32), ((0, 0), (1, 0), (0, 0)))
            hs, st_s = _pool_mix(hs, hist_s, g, w_pool, j, sc, bb=db, tm=dseq, has_history=True,
                                 precise=True)
            pool_p.append(st_p[:, 1:])
            pool_s.append(st_s[:, 1:])
        elif i % N_MIXERS == 1:
            sinks = attn_sinks[j].astype(F32)
            qm, km, vm = _qkv_small(hm, g, w_qkv, j, jnp.arange(N_META))
            qt, kpad, vt, kst, vst = _qkv_t(hp, g, w_qkv[j], N_META + jnp.arange(seq), tm=tm_main)
            qs, ks, vs = _qkv_small(hs, g, w_qkv, j, PAST_LEN + N_META + jnp.arange(dseq))
            hm = _attn_small(qm, km, vm, sinks, hm, w_o, j)
            mkpad = jnp.pad(km[0].reshape(N_META, N_KV_HEADS, HEAD_DIM),
                            ((0, 0), (0, 0), (0, KPAD - HEAD_DIM))).reshape(N_META, -1).astype(BF16)
            hp = _attn_t(qt, kpad, vt, mkpad, vm[0].T.astype(BF16), sinks, hp, w_o_b, j, tq=tq_main)
            flat = lambda a: a.reshape(a.shape[0], a.shape[1], kvd)
            keys = jnp.concatenate([flat(cache_meta_kv[j][:, :, 0]), flat(cache_swa_kv[j][:, :, 0]), ks], axis=1)
            vals = jnp.concatenate([flat(cache_meta_kv[j][:, :, 1]), flat(cache_swa_kv[j][:, :, 1]), vs], axis=1)
            hs = _attn_small(qs, keys, vals, sinks, hs, w_o, j)
            heads = lambda a: a.reshape(a.shape[0], a.shape[1], N_KV_HEADS, HEAD_DIM)
            swa_p.append(jnp.stack([heads(kst), heads(vst)], axis=2))
            meta_kv = jnp.stack([heads(km), heads(vm)], axis=2)
            meta_p.append(jnp.broadcast_to(meta_kv, (nb,) + meta_kv.shape[1:]))
            swa_s.append(jnp.stack([heads(ks), heads(vs)], axis=2))
        else:
            cw = conv_w[j].astype(F32)
            hm, st_m = _conv_mix(hm, jnp.zeros((1, SUBLANES, d), F32), g, w_conv_in, j, cw, w_conv_out,
                                 bb=1, tm=N_META, precise=True)
            hp, st_p = _conv_mix(hp, st_m, g, w_conv_in_b, j, cw, w_conv_out_b, bb=1, tm=tm_main,
                                 precise=False)
            hist_s = jnp.pad(state_conv[j].astype(F32), ((0, 0), (SUBLANES - (CONV_WIDTH - 1), 0), (0, 0)))
            hs, st_s = _conv_mix(hs, hist_s, g, w_conv_in, j, cw, w_conv_out, bb=db, tm=dseq, precise=True)
            conv_p.append(st_p[:, SUBLANES - (CONV_WIDTH - 1):])
            conv_s.append(st_s[:, SUBLANES - (CONV_WIDTH - 1):])

        final = i == depth - 1
        gf = row(norm_final)
        gn = row(norm_ffn[i])
        small = jnp.concatenate([hm.reshape(-1, d), hs.reshape(-1, d)], axis=0)
        small = _moe_small(small, gn, wr, br, w_gate, w_up, w_down, i, gf, final_norm=final)
        hm = small[:N_META].reshape(1, N_META, d)
        hs = small[N_META:].reshape(db, dseq, d)
        hp = _moe_sparse(hp.reshape(-1, d), gn, wr2, brt, tri, w_gate_b, w_up_b, w_down_b, i, gf,
                         tm=tm_main, final_norm=final).reshape(nb, seq, d)

    return (hp, hs, jnp.stack(pool_p), jnp.stack(swa_p), jnp.stack(meta_p), jnp.stack(conv_p),
            jnp.stack(pool_s), jnp.stack(swa_s), jnp.stack(conv_s))
```

```python
import functools

import jax
import jax.numpy as jnp
from jax import lax
from jax.experimental import pallas as pl
from jax.experimental.pallas import tpu as pltpu

F32 = jnp.float32
BF16 = jnp.bfloat16

CHUNK = 64
N_META = 16
N_MIXERS = 3
POOL_WINDOWS = (2, 4, 8, 16)
POOL_STATE = max(POOL_WINDOWS) - 1
HEAD_DIM = 64
N_KV_HEADS = 4
WINDOW = 128
WIN_CHUNKS = WINDOW // CHUNK
ROPE_THETA = 10000.0
CONV_WIDTH = 3
N_GROUPS = 4
EXPERTS_PER_GROUP = 4
N_EXPERTS = N_GROUPS * EXPERTS_PER_GROUP
PAST_LEN = 2048
EPS = 1e-6

LANES = 128
SUBLANES = 8
VMEM_LIMIT = 48 * 1024 * 1024
MOE_VMEM_LIMIT = 56 * 1024 * 1024
NEG_INF = float("-inf")


def _params(*sem):
    return pltpu.CompilerParams(dimension_semantics=sem, vmem_limit_bytes=VMEM_LIMIT)


def _split_bf16(x, parts):
    out = []
    x = x.astype(F32)
    for _ in range(parts):
        hi = x.astype(BF16)
        out.append(hi)
        x = x - hi.astype(F32)
    return out


def _dot(a, b, dn):
    return lax.dot_general(a, b, dn, preferred_element_type=F32)


def _mm_dn(a, b, dn, precise):
    if not precise:
        return _dot(a.astype(BF16), b.astype(BF16), dn)
    a_hi, a_lo = _split_bf16(a, 2)
    b_hi, b_lo = _split_bf16(b, 2)
    return _dot(a_hi, b_hi, dn) + (_dot(a_hi, b_lo, dn) + _dot(a_lo, b_hi, dn))


def _mm(a, b, precise):
    return _mm_dn(a, b, (((1,), (0,)), ((), ())), precise)


def _mm_nt(a, b, precise):
    return _mm_dn(a, b, (((1,), (1,)), ((), ())), precise)


def _rms(x, g):
    ms = jnp.mean(x * x, axis=-1, keepdims=True)
    return x * lax.rsqrt(ms + EPS) * g


def _layer_spec(shape, layer):
    nd = len(shape)
    return pl.BlockSpec((1,) + tuple(shape), lambda *_: (layer,) + (0,) * nd)


POOL_HALO = 16
POOL_LEAD = 16


def _pool_kernel(h_ref, hist_ref, g_ref, w_ref, scale_ref, out_ref, state_ref, buf_ref, sa_ref, sb_ref, *,
                 tm, has_history, precise):
    t = pl.program_id(1)
    base = POOL_LEAD + POOL_HALO
    bb, _, d = h_ref.shape
    pg = d // len(POOL_WINDOWS)
    end = base + tm

    @pl.when(t == 0)
    def _():
        buf_ref[:, 0:POOL_LEAD, :] = jnp.zeros((bb, POOL_LEAD, d), F32)
        buf_ref[:, POOL_LEAD:base, :] = jnp.broadcast_to(hist_ref[...], (bb, POOL_HALO, d))

    h = h_ref[...]
    hn = _rms(h, g_ref[...])
    buf_ref[:, base:end, :] = hn
    if not has_history:
        pos = t * tm + lax.broadcasted_iota(jnp.int32, (1, tm, 1), 1)
    ys = []
    for gi, w in enumerate(POOL_WINDOWS):
        c0, c1 = gi * pg, (gi + 1) * pg
        src, dst, shift, lo = buf_ref, sa_ref, 1, SUBLANES
        while shift < w:
            last = 2 * shift == w
            lo_k = base if last else lo
            s = src[:, lo_k:end, c0:c1] + src[:, lo_k - shift:end - shift, c0:c1]
            if last:
                win = s
            else:
                dst[:, lo_k:end, c0:c1] = s
                src, dst = dst, (sb_ref if dst is sa_ref else sa_ref)
            shift, lo = 2 * shift, lo + SUBLANES
        if has_history:
            mean = win * (1.0 / w)
        else:
            mean = win / jnp.minimum(pos + 1, w).astype(F32)
        ys.append(_mm((mean - hn[:, :, c0:c1]).reshape(bb * tm, pg), w_ref[0, gi], precise))
    y = jnp.concatenate(ys, axis=1).reshape(bb, tm, d) * scale_ref[...]
    out_ref[...] = h + y
    tail = buf_ref[:, end - POOL_HALO:end, :]
    buf_ref[:, POOL_LEAD:base, :] = tail

    @pl.when(t == pl.num_programs(1) - 1)
    def _():
        state_ref[...] = tail


def _pool_mix(h, hist, g, w, layer, scale, *, bb, tm, has_history, precise):
    nb, s, d = h.shape
    halo = POOL_HALO
    pg = d // len(POOL_WINDOWS)
    hist_map = (lambda b, t: (b, 0, 0)) if hist.shape[0] == nb and nb > 1 else (lambda b, t: (0, 0, 0))
    hb = bb if hist.shape[0] == nb and nb > 1 else 1
    out, state = pl.pallas_call(
        functools.partial(_pool_kernel, tm=tm, has_history=has_history, precise=precise),
        out_shape=(jax.ShapeDtypeStruct((nb, s, d), F32), jax.ShapeDtypeStruct((nb, halo, d), F32)),
        grid=(nb // bb, s // tm),
        in_specs=[
            pl.BlockSpec((bb, tm, d), lambda b, t: (b, t, 0)),
            pl.BlockSpec((hb, halo, d), hist_map),
            pl.BlockSpec((1, d), lambda b, t: (0, 0)),
            _layer_spec((len(POOL_WINDOWS), pg, pg), layer),
            pl.BlockSpec((1, d), lambda b, t: (0, 0)),
        ],
        out_specs=(pl.BlockSpec((bb, tm, d), lambda b, t: (b, t, 0)),
                   pl.BlockSpec((bb, halo, d), lambda b, t: (b, 0, 0))),
        scratch_shapes=[pltpu.VMEM((bb, POOL_LEAD + halo + tm, d), F32)] * 3,
        compiler_params=_params("arbitrary", "arbitrary"),
        name="pool_mix",
    )(h, hist, g, w, scale)
    return out, state


def _conv_kernel(h_ref, hist_ref, g_ref, win_ref, cw_ref, wout_ref, out_ref, state_ref, buf_ref, *,
                 tm, precise):
    t = pl.program_id(1)
    bb, _, d = h_ref.shape

    @pl.when(t == 0)
    def _():
        buf_ref[:, 0:SUBLANES, :] = jnp.broadcast_to(hist_ref[...], (bb, SUBLANES, d))

    h = h_ref[...]
    hn = _rms(h, g_ref[...])
    z = _mm(hn.reshape(bb * tm, d), win_ref[0], precise)
    gate_b = z[:, 0:d]
    buf_ref[:, SUBLANES:SUBLANES + tm, :] = (z[:, d:2 * d] * z[:, 2 * d:3 * d]).reshape(bb, tm, d)
    first = SUBLANES - (CONV_WIDTH - 1)
    acc = buf_ref[:, first:first + tm, :] * cw_ref[0:1, :]
    for k in range(1, CONV_WIDTH):
        acc = acc + buf_ref[:, first + k:first + k + tm, :] * cw_ref[k:k + 1, :]
    y = _mm(gate_b * acc.reshape(bb * tm, d), wout_ref[0], precise)
    out_ref[...] = h + y.reshape(bb, tm, d)
    tail = buf_ref[:, tm:tm + SUBLANES, :]
    buf_ref[:, 0:SUBLANES, :] = tail

    @pl.when(t == pl.num_programs(1) - 1)
    def _():
        state_ref[...] = tail


def _conv_mix(h, hist, g, w_in, layer, cw, w_out, *, bb, tm, precise):
    nb, s, d = h.shape
    per_batch = hist.shape[0] == nb and nb > 1
    hist_map = (lambda b, t: (b, 0, 0)) if per_batch else (lambda b, t: (0, 0, 0))
    out, state = pl.pallas_call(
        functools.partial(_conv_kernel, tm=tm, precise=precise),
        out_shape=(jax.ShapeDtypeStruct((nb, s, d), F32), jax.ShapeDtypeStruct((nb, SUBLANES, d), F32)),
        grid=(nb // bb, s // tm),
        in_specs=[
            pl.BlockSpec((bb, tm, d), lambda b, t: (b, t, 0)),
            pl.BlockSpec((bb if per_batch else 1, SUBLANES, d), hist_map),
            pl.BlockSpec((1, d), lambda b, t: (0, 0)),
            _layer_spec((d, 3 * d), layer),
            pl.BlockSpec((CONV_WIDTH, d), lambda b, t: (0, 0)),
            _layer_spec((d, d), layer),
        ],
        out_specs=(pl.BlockSpec((bb, tm, d), lambda b, t: (b, t, 0)),
                   pl.BlockSpec((bb, SUBLANES, d), lambda b, t: (b, 0, 0))),
        scratch_shapes=[pltpu.VMEM((bb, tm + SUBLANES, d), F32)],
        compiler_params=_params("arbitrary", "arbitrary"),
        name="conv_mix",
    )(h, hist, g, w_in, cw, w_out)
    return out, state


def _rope_tables(pos):
    half = HEAD_DIM // 2
    inv = ROPE_THETA ** (-jnp.arange(half, dtype=F32) / half)
    ang = pos.astype(F32)[:, None] * inv[None, :]
    cos, sin = jnp.cos(ang), jnp.sin(ang)
    reps = LANES // HEAD_DIM
    return (jnp.tile(jnp.concatenate([cos, cos], axis=1), (1, reps)),
            jnp.tile(jnp.concatenate([-sin, sin], axis=1), (1, reps)))


def _rope_block(blk, cos, sin):
    half = HEAD_DIM // 2
    lane = lax.broadcasted_iota(jnp.int32, (1, LANES), 1)
    partner = jnp.where((lane % HEAD_DIM) < half, pltpu.roll(blk, LANES - half, 1), pltpu.roll(blk, half, 1))
    return blk * cos + partner * sin


def _qkv_small_kernel(h_ref, g_ref, w_ref, cos_ref, sin_ref, q_ref, k_ref, v_ref):
    bb, s, d = h_ref.shape
    kvd = k_ref.shape[-1]
    hn = _rms(h_ref[...], g_ref[...]).reshape(bb * s, d)
    z = _mm(hn, w_ref[0], True)
    cos, sin = cos_ref[...], sin_ref[...]
    q = [_rope_block(z[:, j * LANES:(j + 1) * LANES], cos, sin) for j in range(d // LANES)]
    k = [_rope_block(z[:, d + j * LANES:d + (j + 1) * LANES], cos, sin) for j in range(kvd // LANES)]
    q_ref[...] = jnp.concatenate(q, axis=1).reshape(bb, s, d)
    k_ref[...] = jnp.concatenate(k, axis=1).reshape(bb, s, kvd)
    v_ref[...] = z[:, d + kvd:d + 2 * kvd].reshape(bb, s, kvd)


def _qkv_small(h, g, w_qkv, layer, pos):
    nb, s, d = h.shape
    kvd = N_KV_HEADS * HEAD_DIM
    cos, sin = _rope_tables(pos)
    cos, sin = jnp.tile(cos, (nb, 1)), jnp.tile(sin, (nb, 1))
    full = lambda shape: pl.BlockSpec(shape, lambda i: (0,) * len(shape))
    return pl.pallas_call(
        _qkv_small_kernel,
        out_shape=(jax.ShapeDtypeStruct((nb, s, d), F32), jax.ShapeDtypeStruct((nb, s, kvd), F32),
                   jax.ShapeDtypeStruct((nb, s, kvd), F32)),
        grid=(1,),
        in_specs=[full((nb, s, d)), full((1, d)), _layer_spec((d, d + 2 * kvd), layer),
                  full((nb * s, LANES)), full((nb * s, LANES))],
        out_specs=(full((nb, s, d)), full((nb, s, kvd)), full((nb, s, kvd))),
        compiler_params=_params("arbitrary"),
        name="qkv_small",
    )(h, g, w_qkv, cos, sin)


QT_TILE = 128
KPAD = LANES


def _qkv_t_kernel(h_ref, g_ref, wqt_ref, wk_ref, wvt_ref, wv_ref, cos_ref, sin_ref, cost_ref, sint_ref,
                  qt_ref, kpad_ref, vt_ref, kst_ref, vst_ref):
    t = pl.program_id(1)
    tm = h_ref.shape[1]
    half = HEAD_DIM // 2
    hb = _rms(h_ref[0], g_ref[...]).astype(BF16)
    zq = _mm_nt(wqt_ref[...], hb, False)
    cost, sint = cost_ref[...], sint_ref[...]
    for hd in range(zq.shape[0] // HEAD_DIM):
        x1 = zq[hd * HEAD_DIM:hd * HEAD_DIM + half]
        x2 = zq[hd * HEAD_DIM + half:(hd + 1) * HEAD_DIM]
        qt_ref[0, hd * HEAD_DIM:hd * HEAD_DIM + half, :] = (x1 * cost - x2 * sint).astype(BF16)
        qt_ref[0, hd * HEAD_DIM + half:(hd + 1) * HEAD_DIM, :] = (x2 * cost + x1 * sint).astype(BF16)
    vt_ref[0] = _mm_nt(wvt_ref[...], hb, False).astype(BF16)

    zk = jnp.dot(hb, wk_ref[...], preferred_element_type=F32)
    lane = lax.broadcasted_iota(jnp.int32, (1, LANES), 1)
    kr = []
    for j in range(zk.shape[1] // LANES):
        blk = _rope_block(zk[:, j * LANES:(j + 1) * LANES], cos_ref[...], sin_ref[...])
        kr.append(blk)
        for sub in range(LANES // HEAD_DIM):
            hk = j * (LANES // HEAD_DIM) + sub
            shifted = blk if sub == 0 else pltpu.roll(blk, LANES - sub * HEAD_DIM, 1)
            kpad_ref[0, :, hk * KPAD:(hk + 1) * KPAD] = jnp.where(lane < HEAD_DIM, shifted, 0.0).astype(BF16)

    @pl.when(t == pl.num_programs(1) - 1)
    def _():
        kst_ref[0] = jnp.concatenate(kr, axis=1)[tm - WINDOW:]
        vst_ref[0] = jnp.dot(hb[tm - WINDOW:], wv_ref[...], preferred_element_type=F32)


def _qkv_t(h, g, w_qkv, pos, *, tm):
    nb, s, d = h.shape
    kvd = N_KV_HEADS * HEAD_DIM
    half = HEAD_DIM // 2
    cos, sin = _rope_tables(pos)
    inv = ROPE_THETA ** (-jnp.arange(half, dtype=F32) / half)
    ang = inv[:, None] * pos.astype(F32)[None, :]
    cost, sint = jnp.cos(ang), jnp.sin(ang)
    wq_t = (w_qkv[:, :d] * (HEAD_DIM ** -0.5)).T.astype(BF16)
    wk = w_qkv[:, d:d + kvd].astype(BF16)
    wv = w_qkv[:, d + kvd:].astype(BF16)
    const = lambda shape: pl.BlockSpec(shape, lambda b, t: (0,) * len(shape))
    return pl.pallas_call(
        _qkv_t_kernel,
        out_shape=(jax.ShapeDtypeStruct((nb, d, s), BF16),
                   jax.ShapeDtypeStruct((nb, s, N_KV_HEADS * KPAD), BF16),
                   jax.ShapeDtypeStruct((nb, kvd, s), BF16),
                   jax.ShapeDtypeStruct((nb, WINDOW, kvd), F32),
                   jax.ShapeDtypeStruct((nb, WINDOW, kvd), F32)),
        grid=(nb, s // tm),
        in_specs=[
            pl.BlockSpec((1, tm, d), lambda b, t: (b, t, 0)),
            const((1, d)), const((d, d)), const((d, kvd)), const((kvd, d)), const((d, kvd)),
            pl.BlockSpec((tm, LANES), lambda b, t: (t, 0)),
            pl.BlockSpec((tm, LANES), lambda b, t: (t, 0)),
            pl.BlockSpec((half, tm), lambda b, t: (0, t)),
            pl.BlockSpec((half, tm), lambda b, t: (0, t)),
        ],
        out_specs=(pl.BlockSpec((1, d, tm), lambda b, t: (b, 0, t)),
                   pl.BlockSpec((1, tm, N_KV_HEADS * KPAD), lambda b, t: (b, t, 0)),
                   pl.BlockSpec((1, kvd, tm), lambda b, t: (b, 0, t)),
                   pl.BlockSpec((1, WINDOW, kvd), lambda b, t: (b, 0, 0)),
                   pl.BlockSpec((1, WINDOW, kvd), lambda b, t: (b, 0, 0))),
        compiler_params=_params("arbitrary", "arbitrary"),
        name="qkv_t",
    )(h, g, wq_t, wk, wv.T, wv, cos, sin, cost, sint)


def _attn_t_kernel(sinks_ref, qt_ref, kc_ref, kp_ref, vc_ref, vp_ref, mk_ref, mvt_ref, h_ref, wo_ref,
                   out_ref, ot_ref, *, tq):
    t = pl.program_id(1)
    gqa = qt_ref.shape[1] // (N_KV_HEADS * HEAD_DIM)
    band = QT_TILE + WIN_CHUNKS * CHUNK
    kk = jnp.concatenate([kp_ref[0], kc_ref[0]], axis=0)
    vv = jnp.concatenate([vp_ref[0], vc_ref[0]], axis=1)
    r = lax.broadcasted_iota(jnp.int32, (band, gqa * QT_TILE), 0)
    ln = lax.broadcasted_iota(jnp.int32, (band, gqa * QT_TILE), 1)
    kchunk = r // CHUNK
    qchunk = (ln // CHUNK) % (QT_TILE // CHUNK)
    visible = (kchunk >= qchunk) & (kchunk <= qchunk + WIN_CHUNKS)
    lgroup = lax.broadcasted_iota(jnp.int32, (1, gqa * QT_TILE), 1) // QT_TILE
    for sub in range(tq // QT_TILE):
        c0 = sub * QT_TILE
        if sub == 0:
            mask = visible & ((t > 0) | (r >= WIN_CHUNKS * CHUNK))
        else:
            mask = visible
        krows = kk[c0:c0 + band]
        vcols = vv[:, c0:c0 + band]
        for hk in range(N_KV_HEADS):
            q4 = jnp.concatenate(
                [qt_ref[0, (hk * gqa + g) * HEAD_DIM:(hk * gqa + g + 1) * HEAD_DIM, c0:c0 + QT_TILE]
                 for g in range(gqa)], axis=1)
            sb = jnp.dot(krows[:, hk * KPAD:hk * KPAD + HEAD_DIM], q4, preferred_element_type=F32)
            sm = jnp.dot(mk_ref[:, hk * KPAD:hk * KPAD + HEAD_DIM], q4, preferred_element_type=F32)
            sb = jnp.where(mask, sb, NEG_INF)
            sink = jnp.zeros((1, gqa * QT_TILE), F32)
            for g in range(gqa):
                sink = jnp.where(lgroup == g, sinks_ref[hk * gqa + g], sink)
            m = jnp.maximum(jnp.maximum(jnp.max(sb, axis=0, keepdims=True),
                                        jnp.max(sm, axis=0, keepdims=True)), sink)
            pb = jnp.exp(sb - m)
            pm = jnp.exp(sm - m)
            denom = (jnp.sum(pb, axis=0, keepdims=True) + jnp.sum(pm, axis=0, keepdims=True)
                     + jnp.exp(sink - m))
            hs = slice(hk * HEAD_DIM, (hk + 1) * HEAD_DIM)
            o = (jnp.dot(vcols[hs], pb.astype(BF16), preferred_element_type=F32)
                 + jnp.dot(mvt_ref[hs, :], pm.astype(BF16), preferred_element_type=F32)) / denom
            for g in range(gqa):
                ot_ref[(hk * gqa + g) * HEAD_DIM:(hk * gqa + g + 1) * HEAD_DIM, c0:c0 + QT_TILE] = (
                    o[:, g * QT_TILE:(g + 1) * QT_TILE].astype(BF16))
    proj = lax.dot_general(ot_ref[...], wo_ref[0], (((0,), (0,)), ((), ())), preferred_element_type=F32)
    out_ref[0] = h_ref[0] + proj


def _attn_t(qt, kpad, vt, mkpad, mvt, sinks, h, w_o, layer, *, tq):
    nb, s, d = h.shape
    kvd = vt.shape[1]
    prev = WIN_CHUNKS * CHUNK
    ratio = tq // prev
    const = lambda shape: pl.BlockSpec(shape, lambda b, t: (0,) * len(shape))
    return pl.pallas_call(
        functools.partial(_attn_t_kernel, tq=tq),
        out_shape=jax.ShapeDtypeStruct((nb, s, d), F32),
        grid=(nb, s // tq),
        in_specs=[
            pl.BlockSpec(memory_space=pltpu.SMEM),
            pl.BlockSpec((1, d, tq), lambda b, t: (b, 0, t)),
            pl.BlockSpec((1, tq, N_KV_HEADS * KPAD), lambda b, t: (b, t, 0)),
            pl.BlockSpec((1, prev, N_KV_HEADS * KPAD), lambda b, t: (b, jnp.maximum(t * ratio - 1, 0), 0)),
            pl.BlockSpec((1, kvd, tq), lambda b, t: (b, 0, t)),
            pl.BlockSpec((1, kvd, prev), lambda b, t: (b, 0, jnp.maximum(t * ratio - 1, 0))),
            const((N_META, N_KV_HEADS * KPAD)), const((kvd, N_META)),
            pl.BlockSpec((1, tq, d), lambda b, t: (b, t, 0)),
            _layer_spec((d, d), layer),
        ],
        out_specs=pl.BlockSpec((1, tq, d), lambda b, t: (b, t, 0)),
        scratch_shapes=[pltpu.VMEM((d, tq), BF16)],
        compiler_params=_params("arbitrary", "arbitrary"),
        name="attn_t",
    )(sinks, qt, kpad, kpad, vt, vt, mkpad, mvt, h, w_o)


def _attn_small_kernel(sinks_ref, q_ref, k_ref, v_ref, h_ref, wo_ref, out_ref, o_ref):
    bb, s, d = q_ref.shape
    gqa = d // (N_KV_HEADS * HEAD_DIM)
    for b in range(bb):
        for hk in range(N_KV_HEADS):
            hs = slice(hk * HEAD_DIM, (hk + 1) * HEAD_DIM)
            qh = jnp.concatenate(
                [q_ref[b, :, (hk * gqa + g) * HEAD_DIM:(hk * gqa + g + 1) * HEAD_DIM] for g in range(gqa)],
                axis=0)
            sink = jnp.concatenate([jnp.full((s, 1), sinks_ref[hk * gqa + g], F32) for g in range(gqa)], axis=0)
            sc = _mm_nt(qh, k_ref[b, :, hs], True) * (HEAD_DIM ** -0.5)
            m = jnp.maximum(jnp.max(sc, axis=-1, keepdims=True), sink)
            p = jnp.exp(sc - m)
            denom = jnp.sum(p, axis=-1, keepdims=True) + jnp.exp(sink - m)
            o = _mm(p, v_ref[b, :, hs], True) / denom
            for g in range(gqa):
                o_ref[b * s:(b + 1) * s, (hk * gqa + g) * HEAD_DIM:(hk * gqa + g + 1) * HEAD_DIM] = (
                    o[g * s:(g + 1) * s])
    out_ref[...] = h_ref[...] + _mm(o_ref[...], wo_ref[0], True).reshape(bb, s, d)


def _attn_small(q, keys, vals, sinks, h, w_o, layer):
    nb, s, d = h.shape
    kn, kvd = keys.shape[1:]
    full = lambda shape: pl.BlockSpec(shape, lambda i: (0,) * len(shape))
    return pl.pallas_call(
        _attn_small_kernel,
        out_shape=jax.ShapeDtypeStruct((nb, s, d), F32),
        grid=(1,),
        in_specs=[pl.BlockSpec(memory_space=pltpu.SMEM), full((nb, s, d)), full((nb, kn, kvd)),
                  full((nb, kn, kvd)), full((nb, s, d)), _layer_spec((d, d), layer)],
        out_specs=full((nb, s, d)),
        scratch_shapes=[pltpu.VMEM((nb * s, d), F32)],
        compiler_params=_params("arbitrary"),
        name="attn_small",
    )(sinks, q, keys, vals, h, w_o)


ROUTE_ROWS = 32
MOE_CHUNK = 128
MOE_CHUNK_WIDE = 192
SEG_ALIGN = 16


def _route_t(lt):
    n = lt.shape[1]
    row8 = lax.broadcasted_iota(jnp.int32, (SUBLANES, n), 0)
    lg = jnp.where(row8 < N_GROUPS, lt[0:SUBLANES], NEG_INF)
    gmax = jnp.max(lg, axis=0, keepdims=True)
    g_idx = jnp.min(jnp.where(lg == gmax, row8, SUBLANES), axis=0, keepdims=True)
    g_w = 1.0 / jnp.sum(jnp.exp(lg - gmax), axis=0, keepdims=True)
    le = lt[SUBLANES:SUBLANES + N_EXPERTS]
    row16 = lax.broadcasted_iota(jnp.int32, (N_EXPERTS, n), 0)
    in_group = (row16 // EXPERTS_PER_GROUP) == g_idx
    l1 = jnp.where(in_group, le, NEG_INF)
    m1 = jnp.max(l1, axis=0, keepdims=True)
    i1 = jnp.min(jnp.where(in_group & (l1 == m1), row16, N_EXPERTS), axis=0, keepdims=True)
    rest = in_group & (row16 != i1)
    l2 = jnp.where(rest, le, NEG_INF)
    m2 = jnp.max(l2, axis=0, keepdims=True)
    i2 = jnp.min(jnp.where(rest & (l2 == m2), row16, N_EXPERTS), axis=0, keepdims=True)
    e2 = jnp.exp(m2 - m1)
    p1 = 1.0 / (1.0 + e2)
    comb = g_w * (jnp.where(row16 == i1, p1, 0.0) + jnp.where(row16 == i2, e2 * p1, 0.0))
    c8 = comb[0:SUBLANES] + comb[SUBLANES:2 * SUBLANES]
    return g_idx, comb, c8 + pltpu.roll(c8, EXPERTS_PER_GROUP, 0)


def _route(logits):
    col = lax.broadcasted_iota(jnp.int32, logits.shape, 1)
    lg = jnp.where(col < N_GROUPS, logits, NEG_INF)
    gmax = jnp.max(lg, axis=-1, keepdims=True)
    g_idx = jnp.min(jnp.where(lg == gmax, col, LANES), axis=-1, keepdims=True)
    g_w = 1.0 / jnp.sum(jnp.exp(lg - gmax), axis=-1, keepdims=True)
    ecol = col - N_GROUPS
    in_group = (ecol >= 0) & (ecol < N_EXPERTS) & ((ecol // EXPERTS_PER_GROUP) == g_idx)
    l1 = jnp.where(in_group, logits, NEG_INF)
    m1 = jnp.max(l1, axis=-1, keepdims=True)
    i1 = jnp.min(jnp.where(in_group & (l1 == m1), col, LANES), axis=-1, keepdims=True)
    rest = in_group & (col != i1)
    l2 = jnp.where(rest, logits, NEG_INF)
    m2 = jnp.max(l2, axis=-1, keepdims=True)
    i2 = jnp.min(jnp.where(rest & (l2 == m2), col, LANES), axis=-1, keepdims=True)
    e2 = jnp.exp(m2 - m1)
    p1 = 1.0 / (1.0 + e2)
    return g_w * (jnp.where(col == i1, p1, 0.0) + jnp.where(col == i2, e2 * p1, 0.0))


def _router_logits_t(wr_ref, br_ref, hn_hi, hn_lo):
    a = _dot(wr_ref[0], hn_hi, (((1,), (1,)), ((), ())))
    b = _dot(wr_ref[0, 0:ROUTE_ROWS], hn_lo, (((1,), (1,)), ((), ())))
    return a[0:ROUTE_ROWS] + a[ROUTE_ROWS:2 * ROUTE_ROWS] + b + br_ref[0]


def _moe_small_kernel(h_ref, g_ref, wr_ref, br_ref, wg_ref, wu_ref, wd_ref, gf_ref,
                      out_ref, wgb_ref, wub_ref, wdb_ref, hn_ref, comb_ref, acc_ref, *, final_norm):
    e = pl.program_id(0)

    @pl.when(e == 0)
    def _():
        hn = _rms(h_ref[...], g_ref[...])
        hn_ref[0], hn_ref[1] = _split_bf16(hn, 2)
        comb_ref[...] = _route(_mm(hn, wr_ref[0], True) + br_ref[0])
        acc_ref[...] = jnp.zeros_like(acc_ref)

    dn = (((1,), (0,)), ((), ()))

    def x_times(w_ref, wb_ref):
        w_hi, w_lo = _split_bf16(w_ref[0, 0], 2)
        wb_ref[0] = w_hi
        return _dot(hn_ref[0], w_hi, dn) + (_dot(hn_ref[0], w_lo, dn) + _dot(hn_ref[1], w_hi, dn))

    col = lax.broadcasted_iota(jnp.int32, comb_ref.shape, 1)
    c = jnp.sum(jnp.where(col == e + N_GROUPS, comb_ref[...], 0.0), axis=-1, keepdims=True)
    gate = x_times(wg_ref, wgb_ref)
    up = x_times(wu_ref, wub_ref)
    a_hi, a_lo = _split_bf16(gate * jax.nn.sigmoid(gate) * up * c, 2)
    wd_hi, wd_lo = _split_bf16(wd_ref[0, 0], 2)
    wdb_ref[0] = wd_hi
    acc_ref[...] += _dot(a_hi, wd_hi, dn) + (_dot(a_hi, wd_lo, dn) + _dot(a_lo, wd_hi, dn))

    @pl.when(e == pl.num_programs(0) - 1)
    def _():
        y = h_ref[...] + acc_ref[...]
        if final_norm:
            y = _rms(y, gf_ref[...])
        out_ref[...] = y


def _moe_small(h, g, wr, br, wg, wu, wd, layer, gf, *, final_norm):
    n, d = h.shape
    _, ne, _, de = wg.shape
    const = lambda shape: pl.BlockSpec(shape, lambda e: (0,) * len(shape))
    return pl.pallas_call(
        functools.partial(_moe_small_kernel, final_norm=final_norm),
        out_shape=(jax.ShapeDtypeStruct((n, d), F32), jax.ShapeDtypeStruct((ne, d, de), BF16),
                   jax.ShapeDtypeStruct((ne, d, de), BF16), jax.ShapeDtypeStruct((ne, de, d), BF16)),
        grid=(ne,),
        in_specs=[
            const((n, d)), const((1, d)),
            _layer_spec((d, LANES), layer), _layer_spec((1, LANES), layer),
            pl.BlockSpec((1, 1, d, de), lambda e: (layer, e, 0, 0)),
            pl.BlockSpec((1, 1, d, de), lambda e: (layer, e, 0, 0)),
            pl.BlockSpec((1, 1, de, d), lambda e: (layer, e, 0, 0)),
            const((1, d)),
        ],
        out_specs=(const((n, d)), pl.BlockSpec((1, d, de), lambda e: (e, 0, 0)),
                   pl.BlockSpec((1, d, de), lambda e: (e, 0, 0)), pl.BlockSpec((1, de, d), lambda e: (e, 0, 0))),
        scratch_shapes=[pltpu.VMEM((2, n, d), BF16), pltpu.VMEM((n, LANES), F32), pltpu.VMEM((n, d), F32)],
        compiler_params=_params("arbitrary"),
        name="moe_small",
    )(h, g, wr, br, wg, wu, wd, gf)


def _moe_sparse_kernel(h_ref, g_ref, wr_ref, br_ref, tri_ref, wg_ref, wu_ref, wd_ref, gf_ref, out_ref,
                       xs_ref, ys_ref, p_ref, cs_ref, *, tm, final_norm):
    rows = xs_ref.shape[0]
    h = h_ref[...]
    hn = _rms(h, g_ref[...])
    hn_hi, hn_lo = _split_bf16(hn, 2)
    g_idx, _, comb4 = _route_t(_router_logits_t(wr_ref, br_ref, hn_hi, hn_lo))

    row8 = lax.broadcasted_iota(jnp.int32, (SUBLANES, tm), 0)
    onehot = row8 == g_idx
    incl = jnp.dot(jnp.where(onehot, 1.0, 0.0).astype(BF16), tri_ref[...], preferred_element_type=F32)
    counts = incl[:, tm - 1:tm].astype(jnp.int32)
    n = [counts[g, 0] for g in range(N_GROUPS)]
    starts = [jnp.int32(0)]
    for g in range(N_GROUPS - 1):
        starts.append(starts[-1] + (n[g] + SEG_ALIGN - 1) // SEG_ALIGN * SEG_ALIGN)
    rank = jnp.sum(jnp.where(onehot, incl, 0.0), axis=0, keepdims=True).astype(jnp.int32) - 1
    start_tok = jnp.zeros_like(g_idx)
    for g in range(1, N_GROUPS):
        start_tok = jnp.where(g_idx == g, starts[g], start_tok)
    pos = start_tok + rank
    riota = lax.broadcasted_iota(jnp.int32, (rows, tm), 0)
    p_ref[...] = jnp.where(riota == pos, 1.0, 0.0).astype(BF16)

    xs_ref[...] = jnp.dot(p_ref[...], hn_hi, preferred_element_type=F32).astype(BF16)
    comb_parts = jnp.concatenate(_split_bf16(comb4, 3), axis=0)
    cs = _dot(p_ref[...], comb_parts, (((1,), (1,)), ((), ())))
    cs_ref[...] = cs[:, 0:SUBLANES] + cs[:, SUBLANES:2 * SUBLANES] + cs[:, 2 * SUBLANES:3 * SUBLANES]
    ys_ref[...] = jnp.zeros_like(ys_ref)

    def experts(g, r0, m):
        x = xs_ref[pl.ds(r0, m), :]
        cc = cs_ref[pl.ds(r0, m), :]
        acts = []
        for j in range(EXPERTS_PER_GROUP):
            e = g * EXPERTS_PER_GROUP + j
            gate = jnp.dot(x, wg_ref[e], preferred_element_type=F32)
            up = jnp.dot(x, wu_ref[e], preferred_element_type=F32)
            acts.append((gate * jax.nn.sigmoid(gate) * up * cc[:, j:j + 1]).astype(BF16))
        y = jnp.dot(jnp.concatenate(acts, axis=1), wd_ref[g], preferred_element_type=F32)
        ys_ref[pl.ds(r0, m), :] = y.astype(BF16)

    for g in range(N_GROUPS):
        seg = pl.multiple_of(starts[g], SEG_ALIGN)

        @pl.when((n[g] > 0) & (n[g] <= MOE_CHUNK))
        def _(g=g, seg=seg):
            experts(g, seg, MOE_CHUNK)

        @pl.when((n[g] > MOE_CHUNK) & (n[g] <= MOE_CHUNK_WIDE))
        def _(g=g, seg=seg):
            experts(g, seg, MOE_CHUNK_WIDE)

        @pl.when(n[g] > MOE_CHUNK_WIDE)
        def _(g=g):
            def chunk(c, carry):
                experts(g, pl.multiple_of(starts[g] + c * MOE_CHUNK, SEG_ALIGN), MOE_CHUNK)
                return carry

            lax.fori_loop(0, (n[g] + MOE_CHUNK - 1) // MOE_CHUNK, chunk, 0)

    back = lax.dot_general(p_ref[...], ys_ref[...], (((0,), (0,)), ((), ())), preferred_element_type=F32)
    y = h + back
    if final_norm:
        y = _rms(y, gf_ref[...])
    out_ref[...] = y


def _moe_sparse(h, g, wr2, brt, tri, wg, wu, wd4, layer, gf, *, tm, final_norm):
    n, d = h.shape
    ne, _, de = wg.shape
    rows = tm + N_GROUPS * SEG_ALIGN + MOE_CHUNK
    resident = lambda shape: pl.BlockSpec(shape, lambda i: (0,) * len(shape), pipeline_mode=pl.Buffered(1))
    const = lambda shape: pl.BlockSpec(shape, lambda i: (0,) * len(shape))
    return pl.pallas_call(
        functools.partial(_moe_sparse_kernel, tm=tm, final_norm=final_norm),
        out_shape=jax.ShapeDtypeStruct((n, d), F32),
        grid=(n // tm,),
        in_specs=[
            pl.BlockSpec((tm, d), lambda i: (i, 0)), const((1, d)),
            _layer_spec((2 * ROUTE_ROWS, d), layer), _layer_spec((ROUTE_ROWS, 1), layer),
            const((tm, tm)),
            resident((ne, d, de)), resident((ne, d, de)), resident((N_GROUPS, EXPERTS_PER_GROUP * de, d)),
            const((1, d)),
        ],
        out_specs=pl.BlockSpec((tm, d), lambda i: (i, 0)),
        scratch_shapes=[pltpu.VMEM((rows, d), BF16), pltpu.VMEM((rows, d), BF16),
                        pltpu.VMEM((rows, tm), BF16), pltpu.VMEM((rows, SUBLANES), F32)],
        compiler_params=pltpu.CompilerParams(dimension_semantics=("arbitrary",),
                                             vmem_limit_bytes=MOE_VMEM_LIMIT),
        name="moe_sparse",
    )(h, g, wr2, brt, tri, wg, wu, wd4, gf)


def kernel(x_prompt, x_sample, state_pool, cache_swa_kv, cache_meta_kv, state_conv, meta_tokens, norm_mix, norm_ffn, norm_final, w_pool, pool_scale, w_qkv, w_o, attn_sinks, w_conv_in, conv_w, w_conv_out, w_group, b_group, w_expert_router, b_expert_router, w_gate, w_up, w_down):
    nb, seq, d = x_prompt.shape
    db, dseq, _ = x_sample.shape
    depth = norm_mix.shape[0]
    kvd = N_KV_HEADS * HEAD_DIM
    tm_main = min(512, seq)
    tq_main = min(512, seq)
    halo = POOL_STATE + 1

    row = lambda a: a.reshape(1, -1).astype(F32)
    rpad = lambda a, k: jnp.pad(a, ((0, 0), (0, k)) + ((0, 0),) * (a.ndim - 2))
    wrt = jnp.concatenate([rpad(jnp.swapaxes(w_group, 1, 2), SUBLANES - N_GROUPS),
                           rpad(jnp.swapaxes(w_expert_router, 1, 2), ROUTE_ROWS - SUBLANES - N_EXPERTS)], axis=1)
    wr2 = jnp.concatenate(_split_bf16(wrt, 2), axis=1)
    brt = jnp.concatenate([rpad(b_group, SUBLANES - N_GROUPS),
                           rpad(b_expert_router, ROUTE_ROWS - SUBLANES - N_EXPERTS)], axis=1)[..., None].astype(F32)
    wr = jnp.pad(jnp.concatenate([w_group, w_expert_router], axis=-1).astype(F32),
                 ((0, 0), (0, 0), (0, LANES - N_GROUPS - N_EXPERTS)))
    br = jnp.pad(jnp.concatenate([b_group, b_expert_router], axis=-1).astype(F32),
                 ((0, 0), (0, LANES - N_GROUPS - N_EXPERTS)))[:, None, :]
    tri = jnp.triu(jnp.ones((tm_main, tm_main), BF16))
    bf = lambda a: a.astype(BF16)
    w_pool_b, w_o_b = bf(w_pool), bf(w_o)
    w_conv_in_b, w_conv_out_b = bf(w_conv_in), bf(w_conv_out)

    hm = meta_tokens.astype(F32)[None]
    hp = x_prompt
    hs = x_sample
    pool_p, swa_p, meta_p, conv_p, pool_s, swa_s, conv_s = [], [], [], [], [], [], []
    for i in range(depth):
        j = i // N_MIXERS
        g = row(norm_mix[i])
        if i % N_MIXERS == 0:
            sc = row(pool_scale[j])
            hm, st_m = _pool_mix(hm, jnp.zeros((1, halo, d), F32), g, w_pool, j, sc, bb=1, tm=N_META,
                                 has_history=False, precise=True)
            hp, st_p = _pool_mix(hp, st_m, g, w_pool_b, j, sc, bb=1, tm=tm_main, has_history=True,
                                 precise=False)
            hist_s = jnp.pad(state_pool[j].astype(F32), ((0, 0), (1, 0), (0, 0)))
            hs, st_s = _pool_mix(hs, hist_s, g, w_pool, j, sc, bb=db, tm=dseq, has_history=True,
                                 precise=True)
            pool_p.append(st_p[:, 1:])
            pool_s.append(st_s[:, 1:])
        elif i % N_MIXERS == 1:
            sinks = attn_sinks[j].astype(F32)
            qm, km, vm = _qkv_small(hm, g, w_qkv, j, jnp.arange(N_META))
            qt, kpad, vt, kst, vst = _qkv_t(hp, g, w_qkv[j], N_META + jnp.arange(seq), tm=tm_main)
            qs, ks, vs = _qkv_small(hs, g, w_qkv, j, PAST_LEN + N_META + jnp.arange(dseq))
            hm = _attn_small(qm, km, vm, sinks, hm, w_o, j)
            mkpad = jnp.pad(km[0].reshape(N_META, N_KV_HEADS, HEAD_DIM),
                            ((0, 0), (0, 0), (0, KPAD - HEAD_DIM))).reshape(N_META, -1).astype(BF16)
            hp = _attn_t(qt, kpad, vt, mkpad, vm[0].T.astype(BF16), sinks, hp, w_o_b, j, tq=tq_main)
            flat = lambda a: a.reshape(a.shape[0], a.shape[1], kvd)
            keys = jnp.concatenate([flat(cache_meta_kv[j][:, :, 0]), flat(cache_swa_kv[j][:, :, 0]), ks], axis=1)
            vals = jnp.concatenate([flat(cache_meta_kv[j][:, :, 1]), flat(cache_swa_kv[j][:, :, 1]), vs], axis=1)
            hs = _attn_small(qs, keys, vals, sinks, hs, w_o, j)
            heads = lambda a: a.reshape(a.shape[0], a.shape[1], N_KV_HEADS, HEAD_DIM)
            swa_p.append(jnp.stack([heads(kst), heads(vst)], axis=2))
            meta_kv = jnp.stack([heads(km), heads(vm)], axis=2)
            meta_p.append(jnp.broadcast_to(meta_kv, (nb,) + meta_kv.shape[1:]))
            swa_s.append(jnp.stack([heads(ks), heads(vs)], axis=2))
        else:
            cw = conv_w[j].astype(F32)
            hm, st_m = _conv_mix(hm, jnp.zeros((1, SUBLANES, d), F32), g, w_conv_in, j, cw, w_conv_out,
                                 bb=1, tm=N_META, precise=True)
            hp, st_p = _conv_mix(hp, st_m, g, w_conv_in_b, j, cw, w_conv_out_b, bb=1, tm=tm_main,
                                 precise=False)
            hist_s = jnp.pad(state_conv[j].astype(F32), ((0, 0), (SUBLANES - (CONV_WIDTH - 1), 0), (0, 0)))
            hs, st_s = _conv_mix(hs, hist_s, g, w_conv_in, j, cw, w_conv_out, bb=db, tm=dseq, precise=True)
            conv_p.append(st_p[:, SUBLANES - (CONV_WIDTH - 1):])
            conv_s.append(st_s[:, SUBLANES - (CONV_WIDTH - 1):])

        final = i == depth - 1
        gf = row(norm_final)
        gn = row(norm_ffn[i])
        small = jnp.concatenate([hm.reshape(-1, d), hs.reshape(-1, d)], axis=0)
        small, wg_b, wu_b, wd_b = _moe_small(small, gn, wr, br, w_gate, w_up, w_down, i, gf, final_norm=final)
        hm = small[:N_META].reshape(1, N_META, d)
        hs = small[N_META:].reshape(db, dseq, d)
        hp = _moe_sparse(hp.reshape(-1, d), gn, wr2, brt, tri, wg_b, wu_b, wd_b.reshape(N_GROUPS, -1, d), i, gf,
                         tm=tm_main, final_norm=final).reshape(nb, seq, d)

    return (hp, hs, jnp.stack(pool_p), jnp.stack(swa_p), jnp.stack(meta_p), jnp.stack(conv_p),
            jnp.stack(pool_s), jnp.stack(swa_s), jnp.stack(conv_s))
```

```python
import functools

import jax
import jax.numpy as jnp
from jax import lax
from jax.experimental import pallas as pl
from jax.experimental.pallas import tpu as pltpu

F32 = jnp.float32
BF16 = jnp.bfloat16

CHUNK = 64
N_META = 16
N_MIXERS = 3
POOL_WINDOWS = (2, 4, 8, 16)
POOL_STATE = max(POOL_WINDOWS) - 1
HEAD_DIM = 64
N_KV_HEADS = 4
WINDOW = 128
WIN_CHUNKS = WINDOW // CHUNK
ROPE_THETA = 10000.0
CONV_WIDTH = 3
N_GROUPS = 4
EXPERTS_PER_GROUP = 4
N_EXPERTS = N_GROUPS * EXPERTS_PER_GROUP
PAST_LEN = 2048
EPS = 1e-6

LANES = 128
SUBLANES = 8
VMEM_LIMIT = 48 * 1024 * 1024
MOE_VMEM_LIMIT = 60 * 1024 * 1024
NEG_INF = float("-inf")


def _params(*sem):
    return pltpu.CompilerParams(dimension_semantics=sem, vmem_limit_bytes=VMEM_LIMIT)


def _split_bf16(x, parts):
    out = []
    x = x.astype(F32)
    for _ in range(parts):
        hi = x.astype(BF16)
        out.append(hi)
        x = x - hi.astype(F32)
    return out


def _dot(a, b, dn):
    return lax.dot_general(a, b, dn, preferred_element_type=F32)


def _mm_dn(a, b, dn, precise):
    if not precise:
        return _dot(a.astype(BF16), b.astype(BF16), dn)
    a_hi, a_lo = _split_bf16(a, 2)
    b_hi, b_lo = _split_bf16(b, 2)
    return _dot(a_hi, b_hi, dn) + (_dot(a_hi, b_lo, dn) + _dot(a_lo, b_hi, dn))


def _mm(a, b, precise):
    return _mm_dn(a, b, (((1,), (0,)), ((), ())), precise)


def _mm_nt(a, b, precise):
    return _mm_dn(a, b, (((1,), (1,)), ((), ())), precise)


def _rms(x, g):
    ms = jnp.mean(x * x, axis=-1, keepdims=True)
    return x * lax.rsqrt(ms + EPS) * g


def _layer_spec(shape, layer):
    nd = len(shape)
    return pl.BlockSpec((1,) + tuple(shape), lambda *_: (layer,) + (0,) * nd)


POOL_HALO = 16
POOL_LEAD = 16


def _pool_kernel(h_ref, hist_ref, g_ref, w_ref, scale_ref, out_ref, state_ref, buf_ref, sa_ref, sb_ref, *,
                 tm, has_history, precise):
    t = pl.program_id(1)
    base = POOL_LEAD + POOL_HALO
    bb, _, d = h_ref.shape
    pg = d // len(POOL_WINDOWS)
    end = base + tm

    @pl.when(t == 0)
    def _():
        buf_ref[:, 0:POOL_LEAD, :] = jnp.zeros((bb, POOL_LEAD, d), F32)
        buf_ref[:, POOL_LEAD:base, :] = jnp.broadcast_to(hist_ref[...], (bb, POOL_HALO, d))

    h = h_ref[...]
    hn = _rms(h, g_ref[...])
    buf_ref[:, base:end, :] = hn
    if not has_history:
        pos = t * tm + lax.broadcasted_iota(jnp.int32, (1, tm, 1), 1)
    ys = []
    for gi, w in enumerate(POOL_WINDOWS):
        c0, c1 = gi * pg, (gi + 1) * pg
        src, dst, shift, lo = buf_ref, sa_ref, 1, SUBLANES
        while shift < w:
            last = 2 * shift == w
            lo_k = base if last else lo
            s = src[:, lo_k:end, c0:c1] + src[:, lo_k - shift:end - shift, c0:c1]
            if last:
                win = s
            else:
                dst[:, lo_k:end, c0:c1] = s
                src, dst = dst, (sb_ref if dst is sa_ref else sa_ref)
            shift, lo = 2 * shift, lo + SUBLANES
        if has_history:
            mean = win * (1.0 / w)
        else:
            mean = win / jnp.minimum(pos + 1, w).astype(F32)
        ys.append(_mm((mean - hn[:, :, c0:c1]).reshape(bb * tm, pg), w_ref[0, gi], precise))
    y = jnp.concatenate(ys, axis=1).reshape(bb, tm, d) * scale_ref[...]
    out_ref[...] = h + y
    tail = buf_ref[:, end - POOL_HALO:end, :]
    buf_ref[:, POOL_LEAD:base, :] = tail

    @pl.when(t == pl.num_programs(1) - 1)
    def _():
        state_ref[...] = tail


def _pool_mix(h, hist, g, w, layer, scale, *, bb, tm, has_history, precise):
    nb, s, d = h.shape
    halo = POOL_HALO
    pg = d // len(POOL_WINDOWS)
    hist_map = (lambda b, t: (b, 0, 0)) if hist.shape[0] == nb and nb > 1 else (lambda b, t: (0, 0, 0))
    hb = bb if hist.shape[0] == nb and nb > 1 else 1
    out, state = pl.pallas_call(
        functools.partial(_pool_kernel, tm=tm, has_history=has_history, precise=precise),
        out_shape=(jax.ShapeDtypeStruct((nb, s, d), F32), jax.ShapeDtypeStruct((nb, halo, d), F32)),
        grid=(nb // bb, s // tm),
        in_specs=[
            pl.BlockSpec((bb, tm, d), lambda b, t: (b, t, 0)),
            pl.BlockSpec((hb, halo, d), hist_map),
            pl.BlockSpec((1, d), lambda b, t: (0, 0)),
            _layer_spec((len(POOL_WINDOWS), pg, pg), layer),
            pl.BlockSpec((1, d), lambda b, t: (0, 0)),
        ],
        out_specs=(pl.BlockSpec((bb, tm, d), lambda b, t: (b, t, 0)),
                   pl.BlockSpec((bb, halo, d), lambda b, t: (b, 0, 0))),
        scratch_shapes=[pltpu.VMEM((bb, POOL_LEAD + halo + tm, d), F32)] * 3,
        compiler_params=_params("arbitrary", "arbitrary"),
        name="pool_mix",
    )(h, hist, g, w, scale)
    return out, state


def _conv_kernel(h_ref, hist_ref, g_ref, win_ref, cw_ref, wout_ref, out_ref, state_ref, buf_ref, *,
                 tm, precise):
    t = pl.program_id(1)
    bb, _, d = h_ref.shape

    @pl.when(t == 0)
    def _():
        buf_ref[:, 0:SUBLANES, :] = jnp.broadcast_to(hist_ref[...], (bb, SUBLANES, d))

    h = h_ref[...]
    hn = _rms(h, g_ref[...])
    z = _mm(hn.reshape(bb * tm, d), win_ref[0], precise)
    gate_b = z[:, 0:d]
    buf_ref[:, SUBLANES:SUBLANES + tm, :] = (z[:, d:2 * d] * z[:, 2 * d:3 * d]).reshape(bb, tm, d)
    first = SUBLANES - (CONV_WIDTH - 1)
    acc = buf_ref[:, first:first + tm, :] * cw_ref[0:1, :]
    for k in range(1, CONV_WIDTH):
        acc = acc + buf_ref[:, first + k:first + k + tm, :] * cw_ref[k:k + 1, :]
    y = _mm(gate_b * acc.reshape(bb * tm, d), wout_ref[0], precise)
    out_ref[...] = h + y.reshape(bb, tm, d)
    tail = buf_ref[:, tm:tm + SUBLANES, :]
    buf_ref[:, 0:SUBLANES, :] = tail

    @pl.when(t == pl.num_programs(1) - 1)
    def _():
        state_ref[...] = tail


def _conv_mix(h, hist, g, w_in, layer, cw, w_out, *, bb, tm, precise):
    nb, s, d = h.shape
    per_batch = hist.shape[0] == nb and nb > 1
    hist_map = (lambda b, t: (b, 0, 0)) if per_batch else (lambda b, t: (0, 0, 0))
    out, state = pl.pallas_call(
        functools.partial(_conv_kernel, tm=tm, precise=precise),
        out_shape=(jax.ShapeDtypeStruct((nb, s, d), F32), jax.ShapeDtypeStruct((nb, SUBLANES, d), F32)),
        grid=(nb // bb, s // tm),
        in_specs=[
            pl.BlockSpec((bb, tm, d), lambda b, t: (b, t, 0)),
            pl.BlockSpec((bb if per_batch else 1, SUBLANES, d), hist_map),
            pl.BlockSpec((1, d), lambda b, t: (0, 0)),
            _layer_spec((d, 3 * d), layer),
            pl.BlockSpec((CONV_WIDTH, d), lambda b, t: (0, 0)),
            _layer_spec((d, d), layer),
        ],
        out_specs=(pl.BlockSpec((bb, tm, d), lambda b, t: (b, t, 0)),
                   pl.BlockSpec((bb, SUBLANES, d), lambda b, t: (b, 0, 0))),
        scratch_shapes=[pltpu.VMEM((bb, tm + SUBLANES, d), F32)],
        compiler_params=_params("arbitrary", "arbitrary"),
        name="conv_mix",
    )(h, hist, g, w_in, cw, w_out)
    return out, state


def _rope_tables(pos):
    half = HEAD_DIM // 2
    inv = ROPE_THETA ** (-jnp.arange(half, dtype=F32) / half)
    ang = pos.astype(F32)[:, None] * inv[None, :]
    cos, sin = jnp.cos(ang), jnp.sin(ang)
    reps = LANES // HEAD_DIM
    return (jnp.tile(jnp.concatenate([cos, cos], axis=1), (1, reps)),
            jnp.tile(jnp.concatenate([-sin, sin], axis=1), (1, reps)))


def _rope_block(blk, cos, sin):
    half = HEAD_DIM // 2
    lane = lax.broadcasted_iota(jnp.int32, (1, LANES), 1)
    partner = jnp.where((lane % HEAD_DIM) < half, pltpu.roll(blk, LANES - half, 1), pltpu.roll(blk, half, 1))
    return blk * cos + partner * sin


def _qkv_small_kernel(h_ref, g_ref, w_ref, cos_ref, sin_ref, q_ref, k_ref, v_ref):
    bb, s, d = h_ref.shape
    kvd = k_ref.shape[-1]
    hn = _rms(h_ref[...], g_ref[...]).reshape(bb * s, d)
    z = _mm(hn, w_ref[0], True)
    cos, sin = cos_ref[...], sin_ref[...]
    q = [_rope_block(z[:, j * LANES:(j + 1) * LANES], cos, sin) for j in range(d // LANES)]
    k = [_rope_block(z[:, d + j * LANES:d + (j + 1) * LANES], cos, sin) for j in range(kvd // LANES)]
    q_ref[...] = jnp.concatenate(q, axis=1).reshape(bb, s, d)
    k_ref[...] = jnp.concatenate(k, axis=1).reshape(bb, s, kvd)
    v_ref[...] = z[:, d + kvd:d + 2 * kvd].reshape(bb, s, kvd)


def _qkv_small(h, g, w_qkv, layer, pos):
    nb, s, d = h.shape
    kvd = N_KV_HEADS * HEAD_DIM
    cos, sin = _rope_tables(pos)
    cos, sin = jnp.tile(cos, (nb, 1)), jnp.tile(sin, (nb, 1))
    full = lambda shape: pl.BlockSpec(shape, lambda i: (0,) * len(shape))
    return pl.pallas_call(
        _qkv_small_kernel,
        out_shape=(jax.ShapeDtypeStruct((nb, s, d), F32), jax.ShapeDtypeStruct((nb, s, kvd), F32),
                   jax.ShapeDtypeStruct((nb, s, kvd), F32)),
        grid=(1,),
        in_specs=[full((nb, s, d)), full((1, d)), _layer_spec((d, d + 2 * kvd), layer),
                  full((nb * s, LANES)), full((nb * s, LANES))],
        out_specs=(full((nb, s, d)), full((nb, s, kvd)), full((nb, s, kvd))),
        compiler_params=_params("arbitrary"),
        name="qkv_small",
    )(h, g, w_qkv, cos, sin)


QT_TILE = 128
KPAD = LANES


def _qkv_t_kernel(h_ref, g_ref, wqt_ref, wk_ref, wvt_ref, wv_ref, cos_ref, sin_ref, cost_ref, sint_ref,
                  qt_ref, kpad_ref, vt_ref, kst_ref, vst_ref):
    t = pl.program_id(1)
    tm = h_ref.shape[1]
    half = HEAD_DIM // 2
    hb = _rms(h_ref[0], g_ref[...]).astype(BF16)
    zq = _mm_nt(wqt_ref[...], hb, False)
    cost, sint = cost_ref[...], sint_ref[...]
    for hd in range(zq.shape[0] // HEAD_DIM):
        x1 = zq[hd * HEAD_DIM:hd * HEAD_DIM + half]
        x2 = zq[hd * HEAD_DIM + half:(hd + 1) * HEAD_DIM]
        qt_ref[0, hd * HEAD_DIM:hd * HEAD_DIM + half, :] = (x1 * cost - x2 * sint).astype(BF16)
        qt_ref[0, hd * HEAD_DIM + half:(hd + 1) * HEAD_DIM, :] = (x2 * cost + x1 * sint).astype(BF16)
    vt_ref[0] = _mm_nt(wvt_ref[...], hb, False).astype(BF16)

    zk = jnp.dot(hb, wk_ref[...], preferred_element_type=F32)
    lane = lax.broadcasted_iota(jnp.int32, (1, LANES), 1)
    kr = []
    for j in range(zk.shape[1] // LANES):
        blk = _rope_block(zk[:, j * LANES:(j + 1) * LANES], cos_ref[...], sin_ref[...])
        kr.append(blk)
        for sub in range(LANES // HEAD_DIM):
            hk = j * (LANES // HEAD_DIM) + sub
            shifted = blk if sub == 0 else pltpu.roll(blk, LANES - sub * HEAD_DIM, 1)
            kpad_ref[0, :, hk * KPAD:(hk + 1) * KPAD] = jnp.where(lane < HEAD_DIM, shifted, 0.0).astype(BF16)

    @pl.when(t == pl.num_programs(1) - 1)
    def _():
        kst_ref[0] = jnp.concatenate(kr, axis=1)[tm - WINDOW:]
        vst_ref[0] = jnp.dot(hb[tm - WINDOW:], wv_ref[...], preferred_element_type=F32)


def _qkv_t(h, g, w_qkv, pos, *, tm):
    nb, s, d = h.shape
    kvd = N_KV_HEADS * HEAD_DIM
    half = HEAD_DIM // 2
    cos, sin = _rope_tables(pos)
    inv = ROPE_THETA ** (-jnp.arange(half, dtype=F32) / half)
    ang = inv[:, None] * pos.astype(F32)[None, :]
    cost, sint = jnp.cos(ang), jnp.sin(ang)
    wq_t = (w_qkv[:, :d] * (HEAD_DIM ** -0.5)).T.astype(BF16)
    wk = w_qkv[:, d:d + kvd].astype(BF16)
    wv = w_qkv[:, d + kvd:].astype(BF16)
    const = lambda shape: pl.BlockSpec(shape, lambda b, t: (0,) * len(shape))
    return pl.pallas_call(
        _qkv_t_kernel,
        out_shape=(jax.ShapeDtypeStruct((nb, d, s), BF16),
                   jax.ShapeDtypeStruct((nb, s, N_KV_HEADS * KPAD), BF16),
                   jax.ShapeDtypeStruct((nb, kvd, s), BF16),
                   jax.ShapeDtypeStruct((nb, WINDOW, kvd), F32),
                   jax.ShapeDtypeStruct((nb, WINDOW, kvd), F32)),
        grid=(nb, s // tm),
        in_specs=[
            pl.BlockSpec((1, tm, d), lambda b, t: (b, t, 0)),
            const((1, d)), const((d, d)), const((d, kvd)), const((kvd, d)), const((d, kvd)),
            pl.BlockSpec((tm, LANES), lambda b, t: (t, 0)),
            pl.BlockSpec((tm, LANES), lambda b, t: (t, 0)),
            pl.BlockSpec((half, tm), lambda b, t: (0, t)),
            pl.BlockSpec((half, tm), lambda b, t: (0, t)),
        ],
        out_specs=(pl.BlockSpec((1, d, tm), lambda b, t: (b, 0, t)),
                   pl.BlockSpec((1, tm, N_KV_HEADS * KPAD), lambda b, t: (b, t, 0)),
                   pl.BlockSpec((1, kvd, tm), lambda b, t: (b, 0, t)),
                   pl.BlockSpec((1, WINDOW, kvd), lambda b, t: (b, 0, 0)),
                   pl.BlockSpec((1, WINDOW, kvd), lambda b, t: (b, 0, 0))),
        compiler_params=_params("arbitrary", "arbitrary"),
        name="qkv_t",
    )(h, g, wq_t, wk, wv.T, wv, cos, sin, cost, sint)


def _attn_t_kernel(sinks_ref, qt_ref, kc_ref, kp_ref, vc_ref, vp_ref, mk_ref, mvt_ref, h_ref, wo_ref,
                   out_ref, ot_ref, *, tq):
    t = pl.program_id(1)
    gqa = qt_ref.shape[1] // (N_KV_HEADS * HEAD_DIM)
    band = QT_TILE + WIN_CHUNKS * CHUNK
    kk = jnp.concatenate([kp_ref[0], kc_ref[0]], axis=0)
    vv = jnp.concatenate([vp_ref[0], vc_ref[0]], axis=1)
    r = lax.broadcasted_iota(jnp.int32, (band, gqa * QT_TILE), 0)
    ln = lax.broadcasted_iota(jnp.int32, (band, gqa * QT_TILE), 1)
    kchunk = r // CHUNK
    qchunk = (ln // CHUNK) % (QT_TILE // CHUNK)
    visible = (kchunk >= qchunk) & (kchunk <= qchunk + WIN_CHUNKS)
    lgroup = lax.broadcasted_iota(jnp.int32, (1, gqa * QT_TILE), 1) // QT_TILE
    for sub in range(tq // QT_TILE):
        c0 = sub * QT_TILE
        if sub == 0:
            mask = visible & ((t > 0) | (r >= WIN_CHUNKS * CHUNK))
        else:
            mask = visible
        krows = kk[c0:c0 + band]
        vcols = vv[:, c0:c0 + band]
        for hk in range(N_KV_HEADS):
            q4 = jnp.concatenate(
                [qt_ref[0, (hk * gqa + g) * HEAD_DIM:(hk * gqa + g + 1) * HEAD_DIM, c0:c0 + QT_TILE]
                 for g in range(gqa)], axis=1)
            sb = jnp.dot(krows[:, hk * KPAD:hk * KPAD + HEAD_DIM], q4, preferred_element_type=F32)
            sm = jnp.dot(mk_ref[:, hk * KPAD:hk * KPAD + HEAD_DIM], q4, preferred_element_type=F32)
            sb = jnp.where(mask, sb, NEG_INF)
            sink = jnp.zeros((1, gqa * QT_TILE), F32)
            for g in range(gqa):
                sink = jnp.where(lgroup == g, sinks_ref[hk * gqa + g], sink)
            m = jnp.maximum(jnp.maximum(jnp.max(sb, axis=0, keepdims=True),
                                        jnp.max(sm, axis=0, keepdims=True)), sink)
            pb = jnp.exp(sb - m)
            pm = jnp.exp(sm - m)
            denom = (jnp.sum(pb, axis=0, keepdims=True) + jnp.sum(pm, axis=0, keepdims=True)
                     + jnp.exp(sink - m))
            hs = slice(hk * HEAD_DIM, (hk + 1) * HEAD_DIM)
            o = (jnp.dot(vcols[hs], pb.astype(BF16), preferred_element_type=F32)
                 + jnp.dot(mvt_ref[hs, :], pm.astype(BF16), preferred_element_type=F32)) / denom
            for g in range(gqa):
                ot_ref[(hk * gqa + g) * HEAD_DIM:(hk * gqa + g + 1) * HEAD_DIM, c0:c0 + QT_TILE] = (
                    o[:, g * QT_TILE:(g + 1) * QT_TILE].astype(BF16))
    proj = lax.dot_general(ot_ref[...], wo_ref[0], (((0,), (0,)), ((), ())), preferred_element_type=F32)
    out_ref[0] = h_ref[0] + proj


def _attn_t(qt, kpad, vt, mkpad, mvt, sinks, h, w_o, layer, *, tq):
    nb, s, d = h.shape
    kvd = vt.shape[1]
    prev = WIN_CHUNKS * CHUNK
    ratio = tq // prev
    const = lambda shape: pl.BlockSpec(shape, lambda b, t: (0,) * len(shape))
    return pl.pallas_call(
        functools.partial(_attn_t_kernel, tq=tq),
        out_shape=jax.ShapeDtypeStruct((nb, s, d), F32),
        grid=(nb, s // tq),
        in_specs=[
            pl.BlockSpec(memory_space=pltpu.SMEM),
            pl.BlockSpec((1, d, tq), lambda b, t: (b, 0, t)),
            pl.BlockSpec((1, tq, N_KV_HEADS * KPAD), lambda b, t: (b, t, 0)),
            pl.BlockSpec((1, prev, N_KV_HEADS * KPAD), lambda b, t: (b, jnp.maximum(t * ratio - 1, 0), 0)),
            pl.BlockSpec((1, kvd, tq), lambda b, t: (b, 0, t)),
            pl.BlockSpec((1, kvd, prev), lambda b, t: (b, 0, jnp.maximum(t * ratio - 1, 0))),
            const((N_META, N_KV_HEADS * KPAD)), const((kvd, N_META)),
            pl.BlockSpec((1, tq, d), lambda b, t: (b, t, 0)),
            _layer_spec((d, d), layer),
        ],
        out_specs=pl.BlockSpec((1, tq, d), lambda b, t: (b, t, 0)),
        scratch_shapes=[pltpu.VMEM((d, tq), BF16)],
        compiler_params=_params("arbitrary", "arbitrary"),
        name="attn_t",
    )(sinks, qt, kpad, kpad, vt, vt, mkpad, mvt, h, w_o)


def _attn_small_kernel(sinks_ref, q_ref, k_ref, v_ref, h_ref, wo_ref, out_ref, o_ref):
    bb, s, d = q_ref.shape
    gqa = d // (N_KV_HEADS * HEAD_DIM)
    for b in range(bb):
        for hk in range(N_KV_HEADS):
            hs = slice(hk * HEAD_DIM, (hk + 1) * HEAD_DIM)
            qh = jnp.concatenate(
                [q_ref[b, :, (hk * gqa + g) * HEAD_DIM:(hk * gqa + g + 1) * HEAD_DIM] for g in range(gqa)],
                axis=0)
            sink = jnp.concatenate([jnp.full((s, 1), sinks_ref[hk * gqa + g], F32) for g in range(gqa)], axis=0)
            sc = _mm_nt(qh, k_ref[b, :, hs], True) * (HEAD_DIM ** -0.5)
            m = jnp.maximum(jnp.max(sc, axis=-1, keepdims=True), sink)
            p = jnp.exp(sc - m)
            denom = jnp.sum(p, axis=-1, keepdims=True) + jnp.exp(sink - m)
            o = _mm(p, v_ref[b, :, hs], True) / denom
            for g in range(gqa):
                o_ref[b * s:(b + 1) * s, (hk * gqa + g) * HEAD_DIM:(hk * gqa + g + 1) * HEAD_DIM] = (
                    o[g * s:(g + 1) * s])
    out_ref[...] = h_ref[...] + _mm(o_ref[...], wo_ref[0], True).reshape(bb, s, d)


def _attn_small(q, keys, vals, sinks, h, w_o, layer):
    nb, s, d = h.shape
    kn, kvd = keys.shape[1:]
    full = lambda shape: pl.BlockSpec(shape, lambda i: (0,) * len(shape))
    return pl.pallas_call(
        _attn_small_kernel,
        out_shape=jax.ShapeDtypeStruct((nb, s, d), F32),
        grid=(1,),
        in_specs=[pl.BlockSpec(memory_space=pltpu.SMEM), full((nb, s, d)), full((nb, kn, kvd)),
                  full((nb, kn, kvd)), full((nb, s, d)), _layer_spec((d, d), layer)],
        out_specs=full((nb, s, d)),
        scratch_shapes=[pltpu.VMEM((nb * s, d), F32)],
        compiler_params=_params("arbitrary"),
        name="attn_small",
    )(sinks, q, keys, vals, h, w_o)


ROUTE_ROWS = 32
MOE_CHUNK = 128
MOE_CHUNK_WIDE = 192
SEG_ALIGN = 16


def _route_t(lt):
    n = lt.shape[1]
    row8 = lax.broadcasted_iota(jnp.int32, (SUBLANES, n), 0)
    lg = jnp.where(row8 < N_GROUPS, lt[0:SUBLANES], NEG_INF)
    gmax = jnp.max(lg, axis=0, keepdims=True)
    g_idx = jnp.min(jnp.where(lg == gmax, row8, SUBLANES), axis=0, keepdims=True)
    g_w = 1.0 / jnp.sum(jnp.exp(lg - gmax), axis=0, keepdims=True)
    le = lt[SUBLANES:SUBLANES + N_EXPERTS]
    row16 = lax.broadcasted_iota(jnp.int32, (N_EXPERTS, n), 0)
    in_group = (row16 // EXPERTS_PER_GROUP) == g_idx
    l1 = jnp.where(in_group, le, NEG_INF)
    m1 = jnp.max(l1, axis=0, keepdims=True)
    i1 = jnp.min(jnp.where(in_group & (l1 == m1), row16, N_EXPERTS), axis=0, keepdims=True)
    rest = in_group & (row16 != i1)
    l2 = jnp.where(rest, le, NEG_INF)
    m2 = jnp.max(l2, axis=0, keepdims=True)
    i2 = jnp.min(jnp.where(rest & (l2 == m2), row16, N_EXPERTS), axis=0, keepdims=True)
    e2 = jnp.exp(m2 - m1)
    p1 = 1.0 / (1.0 + e2)
    comb = g_w * (jnp.where(row16 == i1, p1, 0.0) + jnp.where(row16 == i2, e2 * p1, 0.0))
    c8 = comb[0:SUBLANES] + comb[SUBLANES:2 * SUBLANES]
    return g_idx, comb, c8 + pltpu.roll(c8, EXPERTS_PER_GROUP, 0)


def _route(logits):
    col = lax.broadcasted_iota(jnp.int32, logits.shape, 1)
    lg = jnp.where(col < N_GROUPS, logits, NEG_INF)
    gmax = jnp.max(lg, axis=-1, keepdims=True)
    g_idx = jnp.min(jnp.where(lg == gmax, col, LANES), axis=-1, keepdims=True)
    g_w = 1.0 / jnp.sum(jnp.exp(lg - gmax), axis=-1, keepdims=True)
    ecol = col - N_GROUPS
    in_group = (ecol >= 0) & (ecol < N_EXPERTS) & ((ecol // EXPERTS_PER_GROUP) == g_idx)
    l1 = jnp.where(in_group, logits, NEG_INF)
    m1 = jnp.max(l1, axis=-1, keepdims=True)
    i1 = jnp.min(jnp.where(in_group & (l1 == m1), col, LANES), axis=-1, keepdims=True)
    rest = in_group & (col != i1)
    l2 = jnp.where(rest, logits, NEG_INF)
    m2 = jnp.max(l2, axis=-1, keepdims=True)
    i2 = jnp.min(jnp.where(rest & (l2 == m2), col, LANES), axis=-1, keepdims=True)
    e2 = jnp.exp(m2 - m1)
    p1 = 1.0 / (1.0 + e2)
    return g_w * (jnp.where(col == i1, p1, 0.0) + jnp.where(col == i2, e2 * p1, 0.0))


def _router_logits_t(wr_ref, br_ref, hn_hi, hn_lo):
    a = _dot(wr_ref[0], hn_hi, (((1,), (1,)), ((), ())))
    b = _dot(wr_ref[0, 0:ROUTE_ROWS], hn_lo, (((1,), (1,)), ((), ())))
    return a[0:ROUTE_ROWS] + a[ROUTE_ROWS:2 * ROUTE_ROWS] + b + br_ref[0]


def _moe_small_kernel(h_ref, g_ref, wr_ref, br_ref, wg_ref, wu_ref, wd_ref, gf_ref,
                      out_ref, wgb_ref, wub_ref, wdb_ref, hn_ref, comb_ref, acc_ref, *, final_norm):
    e = pl.program_id(0)

    @pl.when(e == 0)
    def _():
        hn = _rms(h_ref[...], g_ref[...])
        hn_ref[0], hn_ref[1] = _split_bf16(hn, 2)
        comb_ref[...] = _route(_mm(hn, wr_ref[0], True) + br_ref[0])
        acc_ref[...] = jnp.zeros_like(acc_ref)

    dn = (((1,), (0,)), ((), ()))

    def x_times(w_ref, wb_ref):
        w_hi, w_lo = _split_bf16(w_ref[0, 0], 2)
        wb_ref[0] = w_hi
        return _dot(hn_ref[0], w_hi, dn) + (_dot(hn_ref[0], w_lo, dn) + _dot(hn_ref[1], w_hi, dn))

    col = lax.broadcasted_iota(jnp.int32, comb_ref.shape, 1)
    c = jnp.sum(jnp.where(col == e + N_GROUPS, comb_ref[...], 0.0), axis=-1, keepdims=True)
    gate = x_times(wg_ref, wgb_ref)
    up = x_times(wu_ref, wub_ref)
    a_hi, a_lo = _split_bf16(gate * jax.nn.sigmoid(gate) * up * c, 2)
    wd_hi, wd_lo = _split_bf16(wd_ref[0, 0], 2)
    wdb_ref[0] = wd_hi
    acc_ref[...] += _dot(a_hi, wd_hi, dn) + (_dot(a_hi, wd_lo, dn) + _dot(a_lo, wd_hi, dn))

    @pl.when(e == pl.num_programs(0) - 1)
    def _():
        y = h_ref[...] + acc_ref[...]
        if final_norm:
            y = _rms(y, gf_ref[...])
        out_ref[...] = y


def _moe_small(h, g, wr, br, wg, wu, wd, layer, gf, *, final_norm):
    n, d = h.shape
    _, ne, _, de = wg.shape
    const = lambda shape: pl.BlockSpec(shape, lambda e: (0,) * len(shape))
    return pl.pallas_call(
        functools.partial(_moe_small_kernel, final_norm=final_norm),
        out_shape=(jax.ShapeDtypeStruct((n, d), F32), jax.ShapeDtypeStruct((ne, d, de), BF16),
                   jax.ShapeDtypeStruct((ne, d, de), BF16), jax.ShapeDtypeStruct((ne, de, d), BF16)),
        grid=(ne,),
        in_specs=[
            const((n, d)), const((1, d)),
            _layer_spec((d, LANES), layer), _layer_spec((1, LANES), layer),
            pl.BlockSpec((1, 1, d, de), lambda e: (layer, e, 0, 0)),
            pl.BlockSpec((1, 1, d, de), lambda e: (layer, e, 0, 0)),
            pl.BlockSpec((1, 1, de, d), lambda e: (layer, e, 0, 0)),
            const((1, d)),
        ],
        out_specs=(const((n, d)), pl.BlockSpec((1, d, de), lambda e: (e, 0, 0)),
                   pl.BlockSpec((1, d, de), lambda e: (e, 0, 0)), pl.BlockSpec((1, de, d), lambda e: (e, 0, 0))),
        scratch_shapes=[pltpu.VMEM((2, n, d), BF16), pltpu.VMEM((n, LANES), F32), pltpu.VMEM((n, d), F32)],
        compiler_params=_params("arbitrary"),
        name="moe_small",
    )(h, g, wr, br, wg, wu, wd, gf)


def _moe_sparse_kernel(h_ref, g_ref, wr_ref, br_ref, tri_ref, wg_ref, wu_ref, wd_ref, gf_ref, out_ref,
                       xs_ref, ys_ref, p_ref, cs_ref, *, tm, nsub, final_norm):
    rows = p_ref.shape[1]

    @pl.when(pl.program_id(0) == 0)
    def _():
        xs_ref[:, rows:, :] = jnp.zeros((nsub, xs_ref.shape[1] - rows, xs_ref.shape[2]), BF16)
        cs_ref[:, rows:, :] = jnp.zeros((nsub, cs_ref.shape[1] - rows, cs_ref.shape[2]), F32)

    def prep():
        tiles = range(nsub)
        row8 = lax.broadcasted_iota(jnp.int32, (SUBLANES, tm), 0)
        riota = lax.broadcasted_iota(jnp.int32, (rows, tm), 0)
        hn = [_rms(h_ref[t * tm:(t + 1) * tm, :], g_ref[...]) for t in tiles]
        hn_split = [_split_bf16(x, 2) for x in hn]
        logits = [_router_logits_t(wr_ref, br_ref, hi, lo) for hi, lo in hn_split]
        routed = [_route_t(lt) for lt in logits]
        onehot = [row8 == g_idx for g_idx, _, _ in routed]
        incl = [jnp.dot(jnp.where(oh, 1.0, 0.0).astype(BF16), tri_ref[...], preferred_element_type=F32)
                for oh in onehot]
        counts = [x[:, tm - 1:tm].astype(jnp.int32) for x in incl]
        plans = []
        for t in tiles:
            n = [counts[t][g, 0] for g in range(N_GROUPS)]
            starts = [jnp.int32(0)]
            for g in range(N_GROUPS - 1):
                starts.append(starts[-1] + (n[g] + SEG_ALIGN - 1) // SEG_ALIGN * SEG_ALIGN)
            plans.append((n, starts))
        for t in tiles:
            g_idx = routed[t][0]
            rank = jnp.sum(jnp.where(onehot[t], incl[t], 0.0), axis=0, keepdims=True).astype(jnp.int32) - 1
            start_tok = jnp.zeros_like(g_idx)
            for g in range(1, N_GROUPS):
                start_tok = jnp.where(g_idx == g, plans[t][1][g], start_tok)
            p_ref[t] = jnp.where(riota == start_tok + rank, 1.0, 0.0).astype(BF16)
        for t in tiles:
            xs_ref[t, 0:rows, :] = jnp.dot(p_ref[t], hn_split[t][0], preferred_element_type=F32).astype(BF16)
            comb_parts = jnp.concatenate(_split_bf16(routed[t][2], 3), axis=0)
            cs = _dot(p_ref[t], comb_parts, (((1,), (1,)), ((), ())))
            cs_ref[t, 0:rows, :] = (cs[:, 0:SUBLANES] + cs[:, SUBLANES:2 * SUBLANES]
                                    + cs[:, 2 * SUBLANES:3 * SUBLANES])
            ys_ref[t, 0:rows, :] = jnp.zeros((rows, ys_ref.shape[2]), BF16)
        return plans

    def experts(t, g, r0, m):
        x = xs_ref[t, pl.ds(r0, m), :]
        cc = cs_ref[t, pl.ds(r0, m), :]
        acts = []
        for j in range(EXPERTS_PER_GROUP):
            e = g * EXPERTS_PER_GROUP + j
            gate = jnp.dot(x, wg_ref[e], preferred_element_type=F32)
            up = jnp.dot(x, wu_ref[e], preferred_element_type=F32)
            acts.append((gate * jax.nn.sigmoid(gate) * up * cc[:, j:j + 1]).astype(BF16))
        y = jnp.dot(jnp.concatenate(acts, axis=1), wd_ref[g], preferred_element_type=F32)
        ys_ref[t, pl.ds(r0, m), :] = y.astype(BF16)

    plans = prep()

    for t, (n, starts) in enumerate(plans):
        for g in range(N_GROUPS):
            seg = pl.multiple_of(starts[g], SEG_ALIGN)

            @pl.when((n[g] > 0) & (n[g] <= MOE_CHUNK))
            def _(t=t, g=g, seg=seg):
                experts(t, g, seg, MOE_CHUNK)

            @pl.when((n[g] > MOE_CHUNK) & (n[g] <= MOE_CHUNK_WIDE))
            def _(t=t, g=g, seg=seg):
                experts(t, g, seg, MOE_CHUNK_WIDE)

            @pl.when(n[g] > MOE_CHUNK_WIDE)
            def _(t=t, g=g, n=n, starts=starts):
                def chunk(c, carry):
                    experts(t, g, pl.multiple_of(starts[g] + c * MOE_CHUNK, SEG_ALIGN), MOE_CHUNK)
                    return carry

                lax.fori_loop(0, (n[g] + MOE_CHUNK - 1) // MOE_CHUNK, chunk, 0)

    for t in range(nsub):
        back = lax.dot_general(p_ref[t], ys_ref[t, 0:rows, :], (((0,), (0,)), ((), ())),
                               preferred_element_type=F32)
        y = h_ref[t * tm:(t + 1) * tm, :] + back
        if final_norm:
            y = _rms(y, gf_ref[...])
        out_ref[t * tm:(t + 1) * tm, :] = y


def _moe_sparse(h, g, wr2, brt, tri, wg, wu, wd4, layer, gf, *, tm, nsub, final_norm):
    n, d = h.shape
    ne, _, de = wg.shape
    rows = tm + N_GROUPS * SEG_ALIGN
    over = rows + MOE_CHUNK
    resident = lambda shape: pl.BlockSpec(shape, lambda i: (0,) * len(shape), pipeline_mode=pl.Buffered(1))
    const = lambda shape: pl.BlockSpec(shape, lambda i: (0,) * len(shape))
    return pl.pallas_call(
        functools.partial(_moe_sparse_kernel, tm=tm, nsub=nsub, final_norm=final_norm),
        out_shape=jax.ShapeDtypeStruct((n, d), F32),
        grid=(n // (tm * nsub),),
        in_specs=[
            pl.BlockSpec((tm * nsub, d), lambda i: (i, 0)), const((1, d)),
            _layer_spec((2 * ROUTE_ROWS, d), layer), _layer_spec((ROUTE_ROWS, 1), layer),
            const((tm, tm)),
            resident((ne, d, de)), resident((ne, d, de)), resident((N_GROUPS, EXPERTS_PER_GROUP * de, d)),
            const((1, d)),
        ],
        out_specs=pl.BlockSpec((tm * nsub, d), lambda i: (i, 0)),
        scratch_shapes=[pltpu.VMEM((nsub, over, d), BF16), pltpu.VMEM((nsub, over, d), BF16),
                        pltpu.VMEM((nsub, rows, tm), BF16), pltpu.VMEM((nsub, over, SUBLANES), F32)],
        compiler_params=pltpu.CompilerParams(dimension_semantics=("arbitrary",),
                                             vmem_limit_bytes=MOE_VMEM_LIMIT),
        name="moe_sparse",
    )(h, g, wr2, brt, tri, wg, wu, wd4, gf)


def kernel(x_prompt, x_sample, state_pool, cache_swa_kv, cache_meta_kv, state_conv, meta_tokens, norm_mix, norm_ffn, norm_final, w_pool, pool_scale, w_qkv, w_o, attn_sinks, w_conv_in, conv_w, w_conv_out, w_group, b_group, w_expert_router, b_expert_router, w_gate, w_up, w_down):
    nb, seq, d = x_prompt.shape
    db, dseq, _ = x_sample.shape
    depth = norm_mix.shape[0]
    kvd = N_KV_HEADS * HEAD_DIM
    tm_main = min(512, seq)
    tq_main = min(512, seq)
    moe_sub = 2 if (nb * seq) % (2 * tm_main) == 0 else 1
    halo = POOL_STATE + 1

    row = lambda a: a.reshape(1, -1).astype(F32)
    rpad = lambda a, k: jnp.pad(a, ((0, 0), (0, k)) + ((0, 0),) * (a.ndim - 2))
    wrt = jnp.concatenate([rpad(jnp.swapaxes(w_group, 1, 2), SUBLANES - N_GROUPS),
                           rpad(jnp.swapaxes(w_expert_router, 1, 2), ROUTE_ROWS - SUBLANES - N_EXPERTS)], axis=1)
    wr2 = jnp.concatenate(_split_bf16(wrt, 2), axis=1)
    brt = jnp.concatenate([rpad(b_group, SUBLANES - N_GROUPS),
                           rpad(b_expert_router, ROUTE_ROWS - SUBLANES - N_EXPERTS)], axis=1)[..., None].astype(F32)
    wr = jnp.pad(jnp.concatenate([w_group, w_expert_router], axis=-1).astype(F32),
                 ((0, 0), (0, 0), (0, LANES - N_GROUPS - N_EXPERTS)))
    br = jnp.pad(jnp.concatenate([b_group, b_expert_router], axis=-1).astype(F32),
                 ((0, 0), (0, LANES - N_GROUPS - N_EXPERTS)))[:, None, :]
    tri = jnp.triu(jnp.ones((tm_main, tm_main), BF16))
    bf = lambda a: a.astype(BF16)
    w_pool_b, w_o_b = bf(w_pool), bf(w_o)
    w_conv_in_b, w_conv_out_b = bf(w_conv_in), bf(w_conv_out)

    hm = meta_tokens.astype(F32)[None]
    hp = x_prompt
    hs = x_sample
    pool_p, swa_p, meta_p, conv_p, pool_s, swa_s, conv_s = [], [], [], [], [], [], []
    for i in range(depth):
        j = i // N_MIXERS
        g = row(norm_mix[i])
        if i % N_MIXERS == 0:
            sc = row(pool_scale[j])
            hm, st_m = _pool_mix(hm, jnp.zeros((1, halo, d), F32), g, w_pool, j, sc, bb=1, tm=N_META,
                                 has_history=False, precise=True)
            hp, st_p = _pool_mix(hp, st_m, g, w_pool_b, j, sc, bb=1, tm=tm_main, has_history=True,
                                 precise=False)
            hist_s = jnp.pad(state_pool[j].astype(F32), ((0, 0), (1, 0), (0, 0)))
            hs, st_s = _pool_mix(hs, hist_s, g, w_pool, j, sc, bb=db, tm=dseq, has_history=True,
                                 precise=True)
            pool_p.append(st_p[:, 1:])
            pool_s.append(st_s[:, 1:])
        elif i % N_MIXERS == 1:
            sinks = attn_sinks[j].astype(F32)
            qm, km, vm = _qkv_small(hm, g, w_qkv, j, jnp.arange(N_META))
            qt, kpad, vt, kst, vst = _qkv_t(hp, g, w_qkv[j], N_META + jnp.arange(seq), tm=tm_main)
            qs, ks, vs = _qkv_small(hs, g, w_qkv, j, PAST_LEN + N_META + jnp.arange(dseq))
            hm = _attn_small(qm, km, vm, sinks, hm, w_o, j)
            mkpad = jnp.pad(km[0].reshape(N_META, N_KV_HEADS, HEAD_DIM),
                            ((0, 0), (0, 0), (0, KPAD - HEAD_DIM))).reshape(N_META, -1).astype(BF16)
            hp = _attn_t(qt, kpad, vt, mkpad, vm[0].T.astype(BF16), sinks, hp, w_o_b, j, tq=tq_main)
            flat = lambda a: a.reshape(a.shape[0], a.shape[1], kvd)
            keys = jnp.concatenate([flat(cache_meta_kv[j][:, :, 0]), flat(cache_swa_kv[j][:, :, 0]), ks], axis=1)
            vals = jnp.concatenate([flat(cache_meta_kv[j][:, :, 1]), flat(cache_swa_kv[j][:, :, 1]), vs], axis=1)
            hs = _attn_small(qs, keys, vals, sinks, hs, w_o, j)
            heads = lambda a: a.reshape(a.shape[0], a.shape[1], N_KV_HEADS, HEAD_DIM)
            swa_p.append(jnp.stack([heads(kst), heads(vst)], axis=2))
            meta_kv = jnp.stack([heads(km), heads(vm)], axis=2)
            meta_p.append(jnp.broadcast_to(meta_kv, (nb,) + meta_kv.shape[1:]))
            swa_s.append(jnp.stack([heads(ks), heads(vs)], axis=2))
        else:
            cw = conv_w[j].astype(F32)
            hm, st_m = _conv_mix(hm, jnp.zeros((1, SUBLANES, d), F32), g, w_conv_in, j, cw, w_conv_out,
                                 bb=1, tm=N_META, precise=True)
            hp, st_p = _conv_mix(hp, st_m, g, w_conv_in_b, j, cw, w_conv_out_b, bb=1, tm=tm_main,
                                 precise=False)
            hist_s = jnp.pad(state_conv[j].astype(F32), ((0, 0), (SUBLANES - (CONV_WIDTH - 1), 0), (0, 0)))
            hs, st_s = _conv_mix(hs, hist_s, g, w_conv_in, j, cw, w_conv_out, bb=db, tm=dseq, precise=True)
            conv_p.append(st_p[:, SUBLANES - (CONV_WIDTH - 1):])
            conv_s.append(st_s[:, SUBLANES - (CONV_WIDTH - 1):])

        final = i == depth - 1
        gf = row(norm_final)
        gn = row(norm_ffn[i])
        small = jnp.concatenate([hm.reshape(-1, d), hs.reshape(-1, d)], axis=0)
        small, wg_b, wu_b, wd_b = _moe_small(small, gn, wr, br, w_gate, w_up, w_down, i, gf, final_norm=final)
        hm = small[:N_META].reshape(1, N_META, d)
        hs = small[N_META:].reshape(db, dseq, d)
        hp = _moe_sparse(hp.reshape(-1, d), gn, wr2, brt, tri, wg_b, wu_b, wd_b.reshape(N_GROUPS, -1, d), i, gf,
                         tm=tm_main, nsub=moe_sub, final_norm=final).reshape(nb, seq, d)

    return (hp, hs, jnp.stack(pool_p), jnp.stack(swa_p), jnp.stack(meta_p), jnp.stack(conv_p),
            jnp.stack(pool_s), jnp.stack(swa_s), jnp.stack(conv_s))
```

```python
import functools

import jax
import jax.numpy as jnp
from jax import lax
from jax.experimental import pallas as pl
from jax.experimental.pallas import tpu as pltpu

F32 = jnp.float32
BF16 = jnp.bfloat16

CHUNK = 64
N_META = 16
N_MIXERS = 3
POOL_WINDOWS = (2, 4, 8, 16)
POOL_STATE = max(POOL_WINDOWS) - 1
HEAD_DIM = 64
N_KV_HEADS = 4
WINDOW = 128
WIN_CHUNKS = WINDOW // CHUNK
ROPE_THETA = 10000.0
CONV_WIDTH = 3
N_GROUPS = 4
EXPERTS_PER_GROUP = 4
N_EXPERTS = N_GROUPS * EXPERTS_PER_GROUP
PAST_LEN = 2048
EPS = 1e-6

LANES = 128
SUBLANES = 8
VMEM_LIMIT = 48 * 1024 * 1024
MOE_VMEM_LIMIT = 60 * 1024 * 1024
NEG_INF = float("-inf")


def _params(*sem):
    return pltpu.CompilerParams(dimension_semantics=sem, vmem_limit_bytes=VMEM_LIMIT)


def _split_bf16(x, parts):
    out = []
    x = x.astype(F32)
    for _ in range(parts):
        hi = x.astype(BF16)
        out.append(hi)
        x = x - hi.astype(F32)
    return out


def _dot(a, b, dn):
    return lax.dot_general(a, b, dn, preferred_element_type=F32)


def _mm_dn(a, b, dn, precise):
    if not precise:
        return _dot(a.astype(BF16), b.astype(BF16), dn)
    a_hi, a_lo = _split_bf16(a, 2)
    b_hi, b_lo = _split_bf16(b, 2)
    return _dot(a_hi, b_hi, dn) + (_dot(a_hi, b_lo, dn) + _dot(a_lo, b_hi, dn))


def _mm(a, b, precise):
    return _mm_dn(a, b, (((1,), (0,)), ((), ())), precise)


def _mm_nt(a, b, precise):
    return _mm_dn(a, b, (((1,), (1,)), ((), ())), precise)


def _rms(x, g):
    ms = jnp.mean(x * x, axis=-1, keepdims=True)
    return x * lax.rsqrt(ms + EPS) * g


def _layer_spec(shape, layer):
    nd = len(shape)
    return pl.BlockSpec((1,) + tuple(shape), lambda *_: (layer,) + (0,) * nd)


POOL_HALO = 16
POOL_LEAD = 16


def _pool_kernel(h_ref, hist_ref, g_ref, w_ref, scale_ref, out_ref, state_ref, buf_ref, sa_ref, sb_ref, *,
                 tm, has_history, precise):
    t = pl.program_id(1)
    base = POOL_LEAD + POOL_HALO
    bb, _, d = h_ref.shape
    pg = d // len(POOL_WINDOWS)
    end = base + tm

    @pl.when(t == 0)
    def _():
        buf_ref[:, 0:POOL_LEAD, :] = jnp.zeros((bb, POOL_LEAD, d), F32)
        buf_ref[:, POOL_LEAD:base, :] = jnp.broadcast_to(hist_ref[...], (bb, POOL_HALO, d))

    h = h_ref[...]
    hn = _rms(h, g_ref[...])
    buf_ref[:, base:end, :] = hn
    if not has_history:
        pos = t * tm + lax.broadcasted_iota(jnp.int32, (1, tm, 1), 1)
    ys = []
    for gi, w in enumerate(POOL_WINDOWS):
        c0, c1 = gi * pg, (gi + 1) * pg
        src, dst, shift, lo = buf_ref, sa_ref, 1, SUBLANES
        while shift < w:
            last = 2 * shift == w
            lo_k = base if last else lo
            s = src[:, lo_k:end, c0:c1] + src[:, lo_k - shift:end - shift, c0:c1]
            if last:
                win = s
            else:
                dst[:, lo_k:end, c0:c1] = s
                src, dst = dst, (sb_ref if dst is sa_ref else sa_ref)
            shift, lo = 2 * shift, lo + SUBLANES
        if has_history:
            mean = win * (1.0 / w)
        else:
            mean = win / jnp.minimum(pos + 1, w).astype(F32)
        ys.append(_mm((mean - hn[:, :, c0:c1]).reshape(bb * tm, pg), w_ref[0, gi], precise))
    y = jnp.concatenate(ys, axis=1).reshape(bb, tm, d) * scale_ref[...]
    out_ref[...] = h + y
    tail = buf_ref[:, end - POOL_HALO:end, :]
    buf_ref[:, POOL_LEAD:base, :] = tail

    @pl.when(t == pl.num_programs(1) - 1)
    def _():
        state_ref[...] = tail


def _pool_mix(h, hist, g, w, layer, scale, *, bb, tm, has_history, precise):
    nb, s, d = h.shape
    halo = POOL_HALO
    pg = d // len(POOL_WINDOWS)
    hist_map = (lambda b, t: (b, 0, 0)) if hist.shape[0] == nb and nb > 1 else (lambda b, t: (0, 0, 0))
    hb = bb if hist.shape[0] == nb and nb > 1 else 1
    out, state = pl.pallas_call(
        functools.partial(_pool_kernel, tm=tm, has_history=has_history, precise=precise),
        out_shape=(jax.ShapeDtypeStruct((nb, s, d), F32), jax.ShapeDtypeStruct((nb, halo, d), F32)),
        grid=(nb // bb, s // tm),
        in_specs=[
            pl.BlockSpec((bb, tm, d), lambda b, t: (b, t, 0)),
            pl.BlockSpec((hb, halo, d), hist_map),
            pl.BlockSpec((1, d), lambda b, t: (0, 0)),
            _layer_spec((len(POOL_WINDOWS), pg, pg), layer),
            pl.BlockSpec((1, d), lambda b, t: (0, 0)),
        ],
        out_specs=(pl.BlockSpec((bb, tm, d), lambda b, t: (b, t, 0)),
                   pl.BlockSpec((bb, halo, d), lambda b, t: (b, 0, 0))),
        scratch_shapes=[pltpu.VMEM((bb, POOL_LEAD + halo + tm, d), F32)] * 3,
        compiler_params=_params("arbitrary", "arbitrary"),
        name="pool_mix",
    )(h, hist, g, w, scale)
    return out, state


def _conv_kernel(h_ref, hist_ref, g_ref, win_ref, cw_ref, wout_ref, out_ref, state_ref, buf_ref, *,
                 tm, precise):
    t = pl.program_id(1)
    bb, _, d = h_ref.shape

    @pl.when(t == 0)
    def _():
        buf_ref[:, 0:SUBLANES, :] = jnp.broadcast_to(hist_ref[...], (bb, SUBLANES, d))

    h = h_ref[...]
    hn = _rms(h, g_ref[...])
    z = _mm(hn.reshape(bb * tm, d), win_ref[0], precise)
    gate_b = z[:, 0:d]
    buf_ref[:, SUBLANES:SUBLANES + tm, :] = (z[:, d:2 * d] * z[:, 2 * d:3 * d]).reshape(bb, tm, d)
    first = SUBLANES - (CONV_WIDTH - 1)
    acc = buf_ref[:, first:first + tm, :] * cw_ref[0:1, :]
    for k in range(1, CONV_WIDTH):
        acc = acc + buf_ref[:, first + k:first + k + tm, :] * cw_ref[k:k + 1, :]
    y = _mm(gate_b * acc.reshape(bb * tm, d), wout_ref[0], precise)
    out_ref[...] = h + y.reshape(bb, tm, d)
    tail = buf_ref[:, tm:tm + SUBLANES, :]
    buf_ref[:, 0:SUBLANES, :] = tail

    @pl.when(t == pl.num_programs(1) - 1)
    def _():
        state_ref[...] = tail


def _conv_mix(h, hist, g, w_in, layer, cw, w_out, *, bb, tm, precise):
    nb, s, d = h.shape
    per_batch = hist.shape[0] == nb and nb > 1
    hist_map = (lambda b, t: (b, 0, 0)) if per_batch else (lambda b, t: (0, 0, 0))
    out, state = pl.pallas_call(
        functools.partial(_conv_kernel, tm=tm, precise=precise),
        out_shape=(jax.ShapeDtypeStruct((nb, s, d), F32), jax.ShapeDtypeStruct((nb, SUBLANES, d), F32)),
        grid=(nb // bb, s // tm),
        in_specs=[
            pl.BlockSpec((bb, tm, d), lambda b, t: (b, t, 0)),
            pl.BlockSpec((bb if per_batch else 1, SUBLANES, d), hist_map),
            pl.BlockSpec((1, d), lambda b, t: (0, 0)),
            _layer_spec((d, 3 * d), layer),
            pl.BlockSpec((CONV_WIDTH, d), lambda b, t: (0, 0)),
            _layer_spec((d, d), layer),
        ],
        out_specs=(pl.BlockSpec((bb, tm, d), lambda b, t: (b, t, 0)),
                   pl.BlockSpec((bb, SUBLANES, d), lambda b, t: (b, 0, 0))),
        scratch_shapes=[pltpu.VMEM((bb, tm + SUBLANES, d), F32)],
        compiler_params=_params("arbitrary", "arbitrary"),
        name="conv_mix",
    )(h, hist, g, w_in, cw, w_out)
    return out, state


def _rope_tables(pos):
    half = HEAD_DIM // 2
    inv = ROPE_THETA ** (-jnp.arange(half, dtype=F32) / half)
    ang = pos.astype(F32)[:, None] * inv[None, :]
    cos, sin = jnp.cos(ang), jnp.sin(ang)
    reps = LANES // HEAD_DIM
    return (jnp.tile(jnp.concatenate([cos, cos], axis=1), (1, reps)),
            jnp.tile(jnp.concatenate([-sin, sin], axis=1), (1, reps)))


def _rope_block(blk, cos, sin):
    half = HEAD_DIM // 2
    lane = lax.broadcasted_iota(jnp.int32, (1, LANES), 1)
    partner = jnp.where((lane % HEAD_DIM) < half, pltpu.roll(blk, LANES - half, 1), pltpu.roll(blk, half, 1))
    return blk * cos + partner * sin


def _qkv_small_kernel(h_ref, g_ref, w_ref, cos_ref, sin_ref, q_ref, k_ref, v_ref):
    bb, s, d = h_ref.shape
    kvd = k_ref.shape[-1]
    hn = _rms(h_ref[...], g_ref[...]).reshape(bb * s, d)
    z = _mm(hn, w_ref[0], True)
    cos, sin = cos_ref[...], sin_ref[...]
    q = [_rope_block(z[:, j * LANES:(j + 1) * LANES], cos, sin) for j in range(d // LANES)]
    k = [_rope_block(z[:, d + j * LANES:d + (j + 1) * LANES], cos, sin) for j in range(kvd // LANES)]
    q_ref[...] = jnp.concatenate(q, axis=1).reshape(bb, s, d)
    k_ref[...] = jnp.concatenate(k, axis=1).reshape(bb, s, kvd)
    v_ref[...] = z[:, d + kvd:d + 2 * kvd].reshape(bb, s, kvd)


def _qkv_small(h, g, w_qkv, layer, pos):
    nb, s, d = h.shape
    kvd = N_KV_HEADS * HEAD_DIM
    cos, sin = _rope_tables(pos)
    cos, sin = jnp.tile(cos, (nb, 1)), jnp.tile(sin, (nb, 1))
    full = lambda shape: pl.BlockSpec(shape, lambda i: (0,) * len(shape))
    return pl.pallas_call(
        _qkv_small_kernel,
        out_shape=(jax.ShapeDtypeStruct((nb, s, d), F32), jax.ShapeDtypeStruct((nb, s, kvd), F32),
                   jax.ShapeDtypeStruct((nb, s, kvd), F32)),
        grid=(1,),
        in_specs=[full((nb, s, d)), full((1, d)), _layer_spec((d, d + 2 * kvd), layer),
                  full((nb * s, LANES)), full((nb * s, LANES))],
        out_specs=(full((nb, s, d)), full((nb, s, kvd)), full((nb, s, kvd))),
        compiler_params=_params("arbitrary"),
        name="qkv_small",
    )(h, g, w_qkv, cos, sin)


QT_TILE = 128
KPAD = LANES


def _qkv_t_kernel(h_ref, g_ref, wqt_ref, wk_ref, wvt_ref, wv_ref, cos_ref, sin_ref, cost_ref, sint_ref,
                  qt_ref, kpad_ref, vt_ref, kst_ref, vst_ref):
    t = pl.program_id(1)
    tm = h_ref.shape[1]
    half = HEAD_DIM // 2
    lane = lax.broadcasted_iota(jnp.int32, (1, LANES), 1)
    nparts = 2 if tm % (2 * LANES) == 0 else 1
    tp = tm // nparts
    parts = [slice(i * tp, (i + 1) * tp) for i in range(nparts)]
    hb = [_rms(h_ref[0, p, :], g_ref[...]).astype(BF16) for p in parts]
    zq = [_mm_nt(wqt_ref[...], x, False) for x in hb]
    zk = [jnp.dot(x, wk_ref[...], preferred_element_type=F32) for x in hb]
    zv = [_mm_nt(wvt_ref[...], x, False) for x in hb]
    kr = []
    for i, p in enumerate(parts):
        cost, sint = cost_ref[:, p], sint_ref[:, p]
        for hd in range(zq[i].shape[0] // HEAD_DIM):
            x1 = zq[i][hd * HEAD_DIM:hd * HEAD_DIM + half]
            x2 = zq[i][hd * HEAD_DIM + half:(hd + 1) * HEAD_DIM]
            qt_ref[0, hd * HEAD_DIM:hd * HEAD_DIM + half, p] = (x1 * cost - x2 * sint).astype(BF16)
            qt_ref[0, hd * HEAD_DIM + half:(hd + 1) * HEAD_DIM, p] = (x2 * cost + x1 * sint).astype(BF16)
        vt_ref[0, :, p] = zv[i].astype(BF16)
        blks = []
        for j in range(zk[i].shape[1] // LANES):
            blk = _rope_block(zk[i][:, j * LANES:(j + 1) * LANES], cos_ref[p, :], sin_ref[p, :])
            blks.append(blk)
            for sub in range(LANES // HEAD_DIM):
                hk = j * (LANES // HEAD_DIM) + sub
                shifted = blk if sub == 0 else pltpu.roll(blk, LANES - sub * HEAD_DIM, 1)
                kpad_ref[0, p, hk * KPAD:(hk + 1) * KPAD] = jnp.where(lane < HEAD_DIM, shifted, 0.0).astype(BF16)
        kr.append(jnp.concatenate(blks, axis=1))

    @pl.when(t == pl.num_programs(1) - 1)
    def _():
        kst_ref[0] = jnp.concatenate(kr, axis=0)[tm - WINDOW:]
        vst_ref[0] = jnp.dot(hb[-1][tp - WINDOW:], wv_ref[...], preferred_element_type=F32)


def _qkv_t(h, g, w_qkv, pos, *, tm):
    nb, s, d = h.shape
    kvd = N_KV_HEADS * HEAD_DIM
    half = HEAD_DIM // 2
    cos, sin = _rope_tables(pos)
    inv = ROPE_THETA ** (-jnp.arange(half, dtype=F32) / half)
    ang = inv[:, None] * pos.astype(F32)[None, :]
    cost, sint = jnp.cos(ang), jnp.sin(ang)
    wq_t = (w_qkv[:, :d] * (HEAD_DIM ** -0.5)).T.astype(BF16)
    wk = w_qkv[:, d:d + kvd].astype(BF16)
    wv = w_qkv[:, d + kvd:].astype(BF16)
    const = lambda shape: pl.BlockSpec(shape, lambda b, t: (0,) * len(shape))
    return pl.pallas_call(
        _qkv_t_kernel,
        out_shape=(jax.ShapeDtypeStruct((nb, d, s), BF16),
                   jax.ShapeDtypeStruct((nb, s, N_KV_HEADS * KPAD), BF16),
                   jax.ShapeDtypeStruct((nb, kvd, s), BF16),
                   jax.ShapeDtypeStruct((nb, WINDOW, kvd), F32),
                   jax.ShapeDtypeStruct((nb, WINDOW, kvd), F32)),
        grid=(nb, s // tm),
        in_specs=[
            pl.BlockSpec((1, tm, d), lambda b, t: (b, t, 0)),
            const((1, d)), const((d, d)), const((d, kvd)), const((kvd, d)), const((d, kvd)),
            pl.BlockSpec((tm, LANES), lambda b, t: (t, 0)),
            pl.BlockSpec((tm, LANES), lambda b, t: (t, 0)),
            pl.BlockSpec((half, tm), lambda b, t: (0, t)),
            pl.BlockSpec((half, tm), lambda b, t: (0, t)),
        ],
        out_specs=(pl.BlockSpec((1, d, tm), lambda b, t: (b, 0, t)),
                   pl.BlockSpec((1, tm, N_KV_HEADS * KPAD), lambda b, t: (b, t, 0)),
                   pl.BlockSpec((1, kvd, tm), lambda b, t: (b, 0, t)),
                   pl.BlockSpec((1, WINDOW, kvd), lambda b, t: (b, 0, 0)),
                   pl.BlockSpec((1, WINDOW, kvd), lambda b, t: (b, 0, 0))),
        compiler_params=_params("arbitrary", "arbitrary"),
        name="qkv_t",
    )(h, g, wq_t, wk, wv.T, wv, cos, sin, cost, sint)


def _attn_t_kernel(sinks_ref, qt_ref, kc_ref, kp_ref, vc_ref, vp_ref, mk_ref, mvt_ref, h_ref, wo_ref,
                   out_ref, ot_ref, *, tq):
    t = pl.program_id(1)
    gqa = qt_ref.shape[1] // (N_KV_HEADS * HEAD_DIM)
    band = QT_TILE + WIN_CHUNKS * CHUNK
    kk = jnp.concatenate([kp_ref[0], kc_ref[0]], axis=0)
    vv = jnp.concatenate([vp_ref[0], vc_ref[0]], axis=1)
    r = lax.broadcasted_iota(jnp.int32, (band, gqa * QT_TILE), 0)
    ln = lax.broadcasted_iota(jnp.int32, (band, gqa * QT_TILE), 1)
    kchunk = r // CHUNK
    qchunk = (ln // CHUNK) % (QT_TILE // CHUNK)
    visible = (kchunk >= qchunk) & (kchunk <= qchunk + WIN_CHUNKS)
    lgroup = lax.broadcasted_iota(jnp.int32, (1, gqa * QT_TILE), 1) // QT_TILE
    for sub in range(tq // QT_TILE):
        c0 = sub * QT_TILE
        if sub == 0:
            mask = visible & ((t > 0) | (r >= WIN_CHUNKS * CHUNK))
        else:
            mask = visible
        krows = kk[c0:c0 + band]
        vcols = vv[:, c0:c0 + band]
        heads = range(N_KV_HEADS)
        q4 = [jnp.concatenate(
            [qt_ref[0, (hk * gqa + g) * HEAD_DIM:(hk * gqa + g + 1) * HEAD_DIM, c0:c0 + QT_TILE]
             for g in range(gqa)], axis=1) for hk in heads]
        sb = [jnp.dot(krows[:, hk * KPAD:hk * KPAD + HEAD_DIM], q4[hk], preferred_element_type=F32)
              for hk in heads]
        sm = [jnp.dot(mk_ref[:, hk * KPAD:hk * KPAD + HEAD_DIM], q4[hk], preferred_element_type=F32)
              for hk in heads]
        sb = [jnp.where(mask, x, NEG_INF) for x in sb]
        sink = []
        for hk in heads:
            row = jnp.zeros((1, gqa * QT_TILE), F32)
            for g in range(gqa):
                row = jnp.where(lgroup == g, sinks_ref[hk * gqa + g], row)
            sink.append(row)
        m = [jnp.maximum(jnp.maximum(jnp.max(sb[hk], axis=0, keepdims=True),
                                     jnp.max(sm[hk], axis=0, keepdims=True)), sink[hk]) for hk in heads]
        pb = [jnp.exp(sb[hk] - m[hk]) for hk in heads]
        pm = [jnp.exp(sm[hk] - m[hk]) for hk in heads]
        denom = [jnp.sum(pb[hk], axis=0, keepdims=True) + jnp.sum(pm[hk], axis=0, keepdims=True)
                 + jnp.exp(sink[hk] - m[hk]) for hk in heads]
        o = [(jnp.dot(vcols[hk * HEAD_DIM:(hk + 1) * HEAD_DIM], pb[hk].astype(BF16), preferred_element_type=F32)
              + jnp.dot(mvt_ref[hk * HEAD_DIM:(hk + 1) * HEAD_DIM, :], pm[hk].astype(BF16),
                        preferred_element_type=F32)) / denom[hk] for hk in heads]
        for hk in heads:
            for g in range(gqa):
                ot_ref[(hk * gqa + g) * HEAD_DIM:(hk * gqa + g + 1) * HEAD_DIM, c0:c0 + QT_TILE] = (
                    o[hk][:, g * QT_TILE:(g + 1) * QT_TILE].astype(BF16))
    proj = lax.dot_general(ot_ref[...], wo_ref[0], (((0,), (0,)), ((), ())), preferred_element_type=F32)
    out_ref[0] = h_ref[0] + proj


def _attn_t(qt, kpad, vt, mkpad, mvt, sinks, h, w_o, layer, *, tq):
    nb, s, d = h.shape
    kvd = vt.shape[1]
    prev = WIN_CHUNKS * CHUNK
    ratio = tq // prev
    const = lambda shape: pl.BlockSpec(shape, lambda b, t: (0,) * len(shape))
    return pl.pallas_call(
        functools.partial(_attn_t_kernel, tq=tq),
        out_shape=jax.ShapeDtypeStruct((nb, s, d), F32),
        grid=(nb, s // tq),
        in_specs=[
            pl.BlockSpec(memory_space=pltpu.SMEM),
            pl.BlockSpec((1, d, tq), lambda b, t: (b, 0, t)),
            pl.BlockSpec((1, tq, N_KV_HEADS * KPAD), lambda b, t: (b, t, 0)),
            pl.BlockSpec((1, prev, N_KV_HEADS * KPAD), lambda b, t: (b, jnp.maximum(t * ratio - 1, 0), 0)),
            pl.BlockSpec((1, kvd, tq), lambda b, t: (b, 0, t)),
            pl.BlockSpec((1, kvd, prev), lambda b, t: (b, 0, jnp.maximum(t * ratio - 1, 0))),
            const((N_META, N_KV_HEADS * KPAD)), const((kvd, N_META)),
            pl.BlockSpec((1, tq, d), lambda b, t: (b, t, 0)),
            _layer_spec((d, d), layer),
        ],
        out_specs=pl.BlockSpec((1, tq, d), lambda b, t: (b, t, 0)),
        scratch_shapes=[pltpu.VMEM((d, tq), BF16)],
        compiler_params=_params("arbitrary", "arbitrary"),
        name="attn_t",
    )(sinks, qt, kpad, kpad, vt, vt, mkpad, mvt, h, w_o)


def _attn_small_kernel(sinks_ref, q_ref, k_ref, v_ref, h_ref, wo_ref, out_ref, o_ref):
    bb, s, d = q_ref.shape
    gqa = d // (N_KV_HEADS * HEAD_DIM)
    for b in range(bb):
        for hk in range(N_KV_HEADS):
            hs = slice(hk * HEAD_DIM, (hk + 1) * HEAD_DIM)
            qh = jnp.concatenate(
                [q_ref[b, :, (hk * gqa + g) * HEAD_DIM:(hk * gqa + g + 1) * HEAD_DIM] for g in range(gqa)],
                axis=0)
            sink = jnp.concatenate([jnp.full((s, 1), sinks_ref[hk * gqa + g], F32) for g in range(gqa)], axis=0)
            sc = _mm_nt(qh, k_ref[b, :, hs], True) * (HEAD_DIM ** -0.5)
            m = jnp.maximum(jnp.max(sc, axis=-1, keepdims=True), sink)
            p = jnp.exp(sc - m)
            denom = jnp.sum(p, axis=-1, keepdims=True) + jnp.exp(sink - m)
            o = _mm(p, v_ref[b, :, hs], True) / denom
            for g in range(gqa):
                o_ref[b * s:(b + 1) * s, (hk * gqa + g) * HEAD_DIM:(hk * gqa + g + 1) * HEAD_DIM] = (
                    o[g * s:(g + 1) * s])
    out_ref[...] = h_ref[...] + _mm(o_ref[...], wo_ref[0], True).reshape(bb, s, d)


def _attn_small(q, keys, vals, sinks, h, w_o, layer):
    nb, s, d = h.shape
    kn, kvd = keys.shape[1:]
    full = lambda shape: pl.BlockSpec(shape, lambda i: (0,) * len(shape))
    return pl.pallas_call(
        _attn_small_kernel,
        out_shape=jax.ShapeDtypeStruct((nb, s, d), F32),
        grid=(1,),
        in_specs=[pl.BlockSpec(memory_space=pltpu.SMEM), full((nb, s, d)), full((nb, kn, kvd)),
                  full((nb, kn, kvd)), full((nb, s, d)), _layer_spec((d, d), layer)],
        out_specs=full((nb, s, d)),
        scratch_shapes=[pltpu.VMEM((nb * s, d), F32)],
        compiler_params=_params("arbitrary"),
        name="attn_small",
    )(sinks, q, keys, vals, h, w_o)


ROUTE_ROWS = 32
MOE_CHUNK = 128
MOE_CHUNK_WIDE = 192
SEG_ALIGN = 16


def _route_t(lt):
    n = lt.shape[1]
    row8 = lax.broadcasted_iota(jnp.int32, (SUBLANES, n), 0)
    lg = jnp.where(row8 < N_GROUPS, lt[0:SUBLANES], NEG_INF)
    gmax = jnp.max(lg, axis=0, keepdims=True)
    g_idx = jnp.min(jnp.where(lg == gmax, row8, SUBLANES), axis=0, keepdims=True)
    g_w = 1.0 / jnp.sum(jnp.exp(lg - gmax), axis=0, keepdims=True)
    le = lt[SUBLANES:SUBLANES + N_EXPERTS]
    row16 = lax.broadcasted_iota(jnp.int32, (N_EXPERTS, n), 0)
    in_group = (row16 // EXPERTS_PER_GROUP) == g_idx
    l1 = jnp.where(in_group, le, NEG_INF)
    m1 = jnp.max(l1, axis=0, keepdims=True)
    i1 = jnp.min(jnp.where(in_group & (l1 == m1), row16, N_EXPERTS), axis=0, keepdims=True)
    rest = in_group & (row16 != i1)
    l2 = jnp.where(rest, le, NEG_INF)
    m2 = jnp.max(l2, axis=0, keepdims=True)
    i2 = jnp.min(jnp.where(rest & (l2 == m2), row16, N_EXPERTS), axis=0, keepdims=True)
    e2 = jnp.exp(m2 - m1)
    p1 = 1.0 / (1.0 + e2)
    comb = g_w * (jnp.where(row16 == i1, p1, 0.0) + jnp.where(row16 == i2, e2 * p1, 0.0))
    c8 = comb[0:SUBLANES] + comb[SUBLANES:2 * SUBLANES]
    return g_idx, comb, c8 + pltpu.roll(c8, EXPERTS_PER_GROUP, 0)


def _route(logits):
    col = lax.broadcasted_iota(jnp.int32, logits.shape, 1)
    lg = jnp.where(col < N_GROUPS, logits, NEG_INF)
    gmax = jnp.max(lg, axis=-1, keepdims=True)
    g_idx = jnp.min(jnp.where(lg == gmax, col, LANES), axis=-1, keepdims=True)
    g_w = 1.0 / jnp.sum(jnp.exp(lg - gmax), axis=-1, keepdims=True)
    ecol = col - N_GROUPS
    in_group = (ecol >= 0) & (ecol < N_EXPERTS) & ((ecol // EXPERTS_PER_GROUP) == g_idx)
    l1 = jnp.where(in_group, logits, NEG_INF)
    m1 = jnp.max(l1, axis=-1, keepdims=True)
    i1 = jnp.min(jnp.where(in_group & (l1 == m1), col, LANES), axis=-1, keepdims=True)
    rest = in_group & (col != i1)
    l2 = jnp.where(rest, logits, NEG_INF)
    m2 = jnp.max(l2, axis=-1, keepdims=True)
    i2 = jnp.min(jnp.where(rest & (l2 == m2), col, LANES), axis=-1, keepdims=True)
    e2 = jnp.exp(m2 - m1)
    p1 = 1.0 / (1.0 + e2)
    return g_w * (jnp.where(col == i1, p1, 0.0) + jnp.where(col == i2, e2 * p1, 0.0))


def _router_logits_t(wr_ref, br_ref, hn_hi, hn_lo):
    a = _dot(wr_ref[0], hn_hi, (((1,), (1,)), ((), ())))
    b = _dot(wr_ref[0, 0:ROUTE_ROWS], hn_lo, (((1,), (1,)), ((), ())))
    return a[0:ROUTE_ROWS] + a[ROUTE_ROWS:2 * ROUTE_ROWS] + b + br_ref[0]


def _moe_small_kernel(h_ref, g_ref, wr_ref, br_ref, wg_ref, wu_ref, wd_ref, gf_ref,
                      out_ref, wgb_ref, wub_ref, wdb_ref, hn_ref, comb_ref, acc_ref, *, final_norm):
    e = pl.program_id(0)

    @pl.when(e == 0)
    def _():
        hn = _rms(h_ref[...], g_ref[...])
        hn_ref[0], hn_ref[1] = _split_bf16(hn, 2)
        comb_ref[...] = _route(_mm(hn, wr_ref[0], True) + br_ref[0])
        acc_ref[...] = jnp.zeros_like(acc_ref)

    dn = (((1,), (0,)), ((), ()))

    def x_times(w_ref, wb_ref):
        w_hi, w_lo = _split_bf16(w_ref[0, 0], 2)
        wb_ref[0] = w_hi
        return _dot(hn_ref[0], w_hi, dn) + (_dot(hn_ref[0], w_lo, dn) + _dot(hn_ref[1], w_hi, dn))

    col = lax.broadcasted_iota(jnp.int32, comb_ref.shape, 1)
    c = jnp.sum(jnp.where(col == e + N_GROUPS, comb_ref[...], 0.0), axis=-1, keepdims=True)
    gate = x_times(wg_ref, wgb_ref)
    up = x_times(wu_ref, wub_ref)
    a_hi, a_lo = _split_bf16(gate * jax.nn.sigmoid(gate) * up * c, 2)
    wd_hi, wd_lo = _split_bf16(wd_ref[0, 0], 2)
    wdb_ref[0] = wd_hi
    acc_ref[...] += _dot(a_hi, wd_hi, dn) + (_dot(a_hi, wd_lo, dn) + _dot(a_lo, wd_hi, dn))

    @pl.when(e == pl.num_programs(0) - 1)
    def _():
        y = h_ref[...] + acc_ref[...]
        if final_norm:
            y = _rms(y, gf_ref[...])
        out_ref[...] = y


def _moe_small(h, g, wr, br, wg, wu, wd, layer, gf, *, final_norm):
    n, d = h.shape
    _, ne, _, de = wg.shape
    const = lambda shape: pl.BlockSpec(shape, lambda e: (0,) * len(shape))
    return pl.pallas_call(
        functools.partial(_moe_small_kernel, final_norm=final_norm),
        out_shape=(jax.ShapeDtypeStruct((n, d), F32), jax.ShapeDtypeStruct((ne, d, de), BF16),
                   jax.ShapeDtypeStruct((ne, d, de), BF16), jax.ShapeDtypeStruct((ne, de, d), BF16)),
        grid=(ne,),
        in_specs=[
            const((n, d)), const((1, d)),
            _layer_spec((d, LANES), layer), _layer_spec((1, LANES), layer),
            pl.BlockSpec((1, 1, d, de), lambda e: (layer, e, 0, 0)),
            pl.BlockSpec((1, 1, d, de), lambda e: (layer, e, 0, 0)),
            pl.BlockSpec((1, 1, de, d), lambda e: (layer, e, 0, 0)),
            const((1, d)),
        ],
        out_specs=(const((n, d)), pl.BlockSpec((1, d, de), lambda e: (e, 0, 0)),
                   pl.BlockSpec((1, d, de), lambda e: (e, 0, 0)), pl.BlockSpec((1, de, d), lambda e: (e, 0, 0))),
        scratch_shapes=[pltpu.VMEM((2, n, d), BF16), pltpu.VMEM((n, LANES), F32), pltpu.VMEM((n, d), F32)],
        compiler_params=_params("arbitrary"),
        name="moe_small",
    )(h, g, wr, br, wg, wu, wd, gf)


def _moe_sparse_kernel(h_ref, g_ref, wr_ref, br_ref, tri_ref, wg_ref, wu_ref, wd_ref, gf_ref, out_ref,
                       xs_ref, ys_ref, p_ref, cs_ref, *, tm, nsub, final_norm):
    rows = p_ref.shape[1]

    @pl.when(pl.program_id(0) == 0)
    def _():
        xs_ref[:, rows:, :] = jnp.zeros((nsub, xs_ref.shape[1] - rows, xs_ref.shape[2]), BF16)
        cs_ref[:, rows:, :] = jnp.zeros((nsub, cs_ref.shape[1] - rows, cs_ref.shape[2]), F32)

    def prep():
        tiles = range(nsub)
        row8 = lax.broadcasted_iota(jnp.int32, (SUBLANES, tm), 0)
        riota = lax.broadcasted_iota(jnp.int32, (rows, tm), 0)
        hn = [_rms(h_ref[t * tm:(t + 1) * tm, :], g_ref[...]) for t in tiles]
        hn_split = [_split_bf16(x, 2) for x in hn]
        logits = [_router_logits_t(wr_ref, br_ref, hi, lo) for hi, lo in hn_split]
        routed = [_route_t(lt) for lt in logits]
        onehot = [row8 == g_idx for g_idx, _, _ in routed]
        incl = [jnp.dot(jnp.where(oh, 1.0, 0.0).astype(BF16), tri_ref[...], preferred_element_type=F32)
                for oh in onehot]
        counts = [x[:, tm - 1:tm].astype(jnp.int32) for x in incl]
        plans = []
        for t in tiles:
            n = [counts[t][g, 0] for g in range(N_GROUPS)]
            starts = [jnp.int32(0)]
            for g in range(N_GROUPS - 1):
                starts.append(starts[-1] + (n[g] + SEG_ALIGN - 1) // SEG_ALIGN * SEG_ALIGN)
            plans.append((n, starts))
        for t in tiles:
            g_idx = routed[t][0]
            rank = jnp.sum(jnp.where(onehot[t], incl[t], 0.0), axis=0, keepdims=True).astype(jnp.int32) - 1
            start_tok = jnp.zeros_like(g_idx)
            for g in range(1, N_GROUPS):
                start_tok = jnp.where(g_idx == g, plans[t][1][g], start_tok)
            p_ref[t] = jnp.where(riota == start_tok + rank, 1.0, 0.0).astype(BF16)
        for t in tiles:
            xs_ref[t, 0:rows, :] = jnp.dot(p_ref[t], hn_split[t][0], preferred_element_type=F32).astype(BF16)
            comb_parts = jnp.concatenate(_split_bf16(routed[t][2], 3), axis=0)
            cs = _dot(p_ref[t], comb_parts, (((1,), (1,)), ((), ())))
            cs_ref[t, 0:rows, :] = (cs[:, 0:SUBLANES] + cs[:, SUBLANES:2 * SUBLANES]
                                    + cs[:, 2 * SUBLANES:3 * SUBLANES])
            ys_ref[t, 0:rows, :] = jnp.zeros((rows, ys_ref.shape[2]), BF16)
        return plans

    def experts(t, g, r0, m):
        x = xs_ref[t, pl.ds(r0, m), :]
        cc = cs_ref[t, pl.ds(r0, m), :]
        acts = []
        for j in range(EXPERTS_PER_GROUP):
            e = g * EXPERTS_PER_GROUP + j
            gate = jnp.dot(x, wg_ref[e], preferred_element_type=F32)
            up = jnp.dot(x, wu_ref[e], preferred_element_type=F32)
            acts.append((gate * jax.nn.sigmoid(gate) * up * cc[:, j:j + 1]).astype(BF16))
        y = jnp.dot(jnp.concatenate(acts, axis=1), wd_ref[g], preferred_element_type=F32)
        ys_ref[t, pl.ds(r0, m), :] = y.astype(BF16)

    plans = prep()

    for t, (n, starts) in enumerate(plans):
        for g in range(N_GROUPS):
            seg = pl.multiple_of(starts[g], SEG_ALIGN)

            @pl.when((n[g] > 0) & (n[g] <= MOE_CHUNK))
            def _(t=t, g=g, seg=seg):
                experts(t, g, seg, MOE_CHUNK)

            @pl.when((n[g] > MOE_CHUNK) & (n[g] <= MOE_CHUNK_WIDE))
            def _(t=t, g=g, seg=seg):
                experts(t, g, seg, MOE_CHUNK_WIDE)

            @pl.when(n[g] > MOE_CHUNK_WIDE)
            def _(t=t, g=g, n=n, starts=starts):
                def chunk(c, carry):
                    experts(t, g, pl.multiple_of(starts[g] + c * MOE_CHUNK, SEG_ALIGN), MOE_CHUNK)
                    return carry

                lax.fori_loop(0, (n[g] + MOE_CHUNK - 1) // MOE_CHUNK, chunk, 0)

    for t in range(nsub):
        back = lax.dot_general(p_ref[t], ys_ref[t, 0:rows, :], (((0,), (0,)), ((), ())),
                               preferred_element_type=F32)
        y = h_ref[t * tm:(t + 1) * tm, :] + back
        if final_norm:
            y = _rms(y, gf_ref[...])
        out_ref[t * tm:(t + 1) * tm, :] = y


def _moe_sparse(h, g, wr2, brt, tri, wg, wu, wd4, layer, gf, *, tm, nsub, final_norm):
    n, d = h.shape
    ne, _, de = wg.shape
    rows = tm + N_GROUPS * SEG_ALIGN
    over = rows + MOE_CHUNK
    resident = lambda shape: pl.BlockSpec(shape, lambda i: (0,) * len(shape), pipeline_mode=pl.Buffered(1))
    const = lambda shape: pl.BlockSpec(shape, lambda i: (0,) * len(shape))
    return pl.pallas_call(
        functools.partial(_moe_sparse_kernel, tm=tm, nsub=nsub, final_norm=final_norm),
        out_shape=jax.ShapeDtypeStruct((n, d), F32),
        grid=(n // (tm * nsub),),
        in_specs=[
            pl.BlockSpec((tm * nsub, d), lambda i: (i, 0)), const((1, d)),
            _layer_spec((2 * ROUTE_ROWS, d), layer), _layer_spec((ROUTE_ROWS, 1), layer),
            const((tm, tm)),
            resident((ne, d, de)), resident((ne, d, de)), resident((N_GROUPS, EXPERTS_PER_GROUP * de, d)),
            const((1, d)),
        ],
        out_specs=pl.BlockSpec((tm * nsub, d), lambda i: (i, 0)),
        scratch_shapes=[pltpu.VMEM((nsub, over, d), BF16), pltpu.VMEM((nsub, over, d), BF16),
                        pltpu.VMEM((nsub, rows, tm), BF16), pltpu.VMEM((nsub, over, SUBLANES), F32)],
        compiler_params=pltpu.CompilerParams(dimension_semantics=("arbitrary",),
                                             vmem_limit_bytes=MOE_VMEM_LIMIT),
        name="moe_sparse",
    )(h, g, wr2, brt, tri, wg, wu, wd4, gf)


def kernel(x_prompt, x_sample, state_pool, cache_swa_kv, cache_meta_kv, state_conv, meta_tokens, norm_mix, norm_ffn, norm_final, w_pool, pool_scale, w_qkv, w_o, attn_sinks, w_conv_in, conv_w, w_conv_out, w_group, b_group, w_expert_router, b_expert_router, w_gate, w_up, w_down):
    nb, seq, d = x_prompt.shape
    db, dseq, _ = x_sample.shape
    depth = norm_mix.shape[0]
    kvd = N_KV_HEADS * HEAD_DIM
    tm_main = min(512, seq)
    tq_main = min(512, seq)
    moe_sub = 2 if (nb * seq) % (2 * tm_main) == 0 else 1
    halo = POOL_STATE + 1

    row = lambda a: a.reshape(1, -1).astype(F32)
    rpad = lambda a, k: jnp.pad(a, ((0, 0), (0, k)) + ((0, 0),) * (a.ndim - 2))
    wrt = jnp.concatenate([rpad(jnp.swapaxes(w_group, 1, 2), SUBLANES - N_GROUPS),
                           rpad(jnp.swapaxes(w_expert_router, 1, 2), ROUTE_ROWS - SUBLANES - N_EXPERTS)], axis=1)
    wr2 = jnp.concatenate(_split_bf16(wrt, 2), axis=1)
    brt = jnp.concatenate([rpad(b_group, SUBLANES - N_GROUPS),
                           rpad(b_expert_router, ROUTE_ROWS - SUBLANES - N_EXPERTS)], axis=1)[..., None].astype(F32)
    wr = jnp.pad(jnp.concatenate([w_group, w_expert_router], axis=-1).astype(F32),
                 ((0, 0), (0, 0), (0, LANES - N_GROUPS - N_EXPERTS)))
    br = jnp.pad(jnp.concatenate([b_group, b_expert_router], axis=-1).astype(F32),
                 ((0, 0), (0, LANES - N_GROUPS - N_EXPERTS)))[:, None, :]
    tri = jnp.triu(jnp.ones((tm_main, tm_main), BF16))
    bf = lambda a: a.astype(BF16)
    w_pool_b, w_o_b = bf(w_pool), bf(w_o)
    w_conv_in_b, w_conv_out_b = bf(w_conv_in), bf(w_conv_out)

    hm = meta_tokens.astype(F32)[None]
    hp = x_prompt
    hs = x_sample
    pool_p, swa_p, meta_p, conv_p, pool_s, swa_s, conv_s = [], [], [], [], [], [], []
    for i in range(depth):
        j = i // N_MIXERS
        g = row(norm_mix[i])
        if i % N_MIXERS == 0:
            sc = row(pool_scale[j])
            hm, st_m = _pool_mix(hm, jnp.zeros((1, halo, d), F32), g, w_pool, j, sc, bb=1, tm=N_META,
                                 has_history=False, precise=True)
            hp, st_p = _pool_mix(hp, st_m, g, w_pool_b, j, sc, bb=1, tm=tm_main, has_history=True,
                                 precise=False)
            hist_s = jnp.pad(state_pool[j].astype(F32), ((0, 0), (1, 0), (0, 0)))
            hs, st_s = _pool_mix(hs, hist_s, g, w_pool, j, sc, bb=db, tm=dseq, has_history=True,
                                 precise=True)
            pool_p.append(st_p[:, 1:])
            pool_s.append(st_s[:, 1:])
        elif i % N_MIXERS == 1:
            sinks = attn_sinks[j].astype(F32)
            qm, km, vm = _qkv_small(hm, g, w_qkv, j, jnp.arange(N_META))
            qt, kpad, vt, kst, vst = _qkv_t(hp, g, w_qkv[j], N_META + jnp.arange(seq), tm=tm_main)
            qs, ks, vs = _qkv_small(hs, g, w_qkv, j, PAST_LEN + N_META + jnp.arange(dseq))
            hm = _attn_small(qm, km, vm, sinks, hm, w_o, j)
            mkpad = jnp.pad(km[0].reshape(N_META, N_KV_HEADS, HEAD_DIM),
                            ((0, 0), (0, 0), (0, KPAD - HEAD_DIM))).reshape(N_META, -1).astype(BF16)
            hp = _attn_t(qt, kpad, vt, mkpad, vm[0].T.astype(BF16), sinks, hp, w_o_b, j, tq=tq_main)
            flat = lambda a: a.reshape(a.shape[0], a.shape[1], kvd)
            keys = jnp.concatenate([flat(cache_meta_kv[j][:, :, 0]), flat(cache_swa_kv[j][:, :, 0]), ks], axis=1)
            vals = jnp.concatenate([flat(cache_meta_kv[j][:, :, 1]), flat(cache_swa_kv[j][:, :, 1]), vs], axis=1)
            hs = _attn_small(qs, keys, vals, sinks, hs, w_o, j)
            heads = lambda a: a.reshape(a.shape[0], a.shape[1], N_KV_HEADS, HEAD_DIM)
            swa_p.append(jnp.stack([heads(kst), heads(vst)], axis=2))
            meta_kv = jnp.stack([heads(km), heads(vm)], axis=2)
            meta_p.append(jnp.broadcast_to(meta_kv, (nb,) + meta_kv.shape[1:]))
            swa_s.append(jnp.stack([heads(ks), heads(vs)], axis=2))
        else:
            cw = conv_w[j].astype(F32)
            hm, st_m = _conv_mix(hm, jnp.zeros((1, SUBLANES, d), F32), g, w_conv_in, j, cw, w_conv_out,
                                 bb=1, tm=N_META, precise=True)
            hp, st_p = _conv_mix(hp, st_m, g, w_conv_in_b, j, cw, w_conv_out_b, bb=1, tm=tm_main,
                                 precise=False)
            hist_s = jnp.pad(state_conv[j].astype(F32), ((0, 0), (SUBLANES - (CONV_WIDTH - 1), 0), (0, 0)))
            hs, st_s = _conv_mix(hs, hist_s, g, w_conv_in, j, cw, w_conv_out, bb=db, tm=dseq, precise=True)
            conv_p.append(st_p[:, SUBLANES - (CONV_WIDTH - 1):])
            conv_s.append(st_s[:, SUBLANES - (CONV_WIDTH - 1):])

        final = i == depth - 1
        gf = row(norm_final)
        gn = row(norm_ffn[i])
        small = jnp.concatenate([hm.reshape(-1, d), hs.reshape(-1, d)], axis=0)
        small, wg_b, wu_b, wd_b = _moe_small(small, gn, wr, br, w_gate, w_up, w_down, i, gf, final_norm=final)
        hm = small[:N_META].reshape(1, N_META, d)
        hs = small[N_META:].reshape(db, dseq, d)
        hp = _moe_sparse(hp.reshape(-1, d), gn, wr2, brt, tri, wg_b, wu_b, wd_b.reshape(N_GROUPS, -1, d), i, gf,
                         tm=tm_main, nsub=moe_sub, final_norm=final).reshape(nb, seq, d)

    return (hp, hs, jnp.stack(pool_p), jnp.stack(swa_p), jnp.stack(meta_p), jnp.stack(conv_p),
            jnp.stack(pool_s), jnp.stack(swa_s), jnp.stack(conv_s))
```

```python
import functools

import jax
import jax.numpy as jnp
from jax import lax
from jax.experimental import pallas as pl
from jax.experimental.pallas import tpu as pltpu

F32 = jnp.float32
BF16 = jnp.bfloat16

CHUNK = 64
N_META = 16
N_MIXERS = 3
POOL_WINDOWS = (2, 4, 8, 16)
POOL_STATE = max(POOL_WINDOWS) - 1
HEAD_DIM = 64
N_KV_HEADS = 4
WINDOW = 128
WIN_CHUNKS = WINDOW // CHUNK
ROPE_THETA = 10000.0
CONV_WIDTH = 3
N_GROUPS = 4
EXPERTS_PER_GROUP = 4
N_EXPERTS = N_GROUPS * EXPERTS_PER_GROUP
PAST_LEN = 2048
EPS = 1e-6

LANES = 128
SUBLANES = 8
VMEM_LIMIT = 48 * 1024 * 1024
MOE_VMEM_LIMIT = 60 * 1024 * 1024
NEG_INF = float("-inf")


def _params(*sem):
    return pltpu.CompilerParams(dimension_semantics=sem, vmem_limit_bytes=VMEM_LIMIT)


def _split_bf16(x, parts):
    out = []
    x = x.astype(F32)
    for _ in range(parts):
        hi = x.astype(BF16)
        out.append(hi)
        x = x - hi.astype(F32)
    return out


def _dot(a, b, dn):
    return lax.dot_general(a, b, dn, preferred_element_type=F32)


def _mm_dn(a, b, dn, precise):
    if not precise:
        return _dot(a.astype(BF16), b.astype(BF16), dn)
    a_hi, a_lo = _split_bf16(a, 2)
    b_hi, b_lo = _split_bf16(b, 2)
    return _dot(a_hi, b_hi, dn) + (_dot(a_hi, b_lo, dn) + _dot(a_lo, b_hi, dn))


def _mm(a, b, precise):
    return _mm_dn(a, b, (((1,), (0,)), ((), ())), precise)


def _mm_nt(a, b, precise):
    return _mm_dn(a, b, (((1,), (1,)), ((), ())), precise)


def _rms(x, g):
    ms = jnp.mean(x * x, axis=-1, keepdims=True)
    return x * lax.rsqrt(ms + EPS) * g


def _layer_spec(shape, layer):
    nd = len(shape)
    return pl.BlockSpec((1,) + tuple(shape), lambda *_: (layer,) + (0,) * nd)


POOL_HALO = 16
POOL_LEAD = 16


def _pool_kernel(h_ref, hist_ref, g_ref, w_ref, scale_ref, out_ref, state_ref, buf_ref, sa_ref, sb_ref, *,
                 tm, has_history, precise):
    t = pl.program_id(1)
    base = POOL_LEAD + POOL_HALO
    bb, _, d = h_ref.shape
    pg = d // len(POOL_WINDOWS)
    end = base + tm

    @pl.when(t == 0)
    def _():
        buf_ref[:, 0:POOL_LEAD, :] = jnp.zeros((bb, POOL_LEAD, d), F32)
        buf_ref[:, POOL_LEAD:base, :] = jnp.broadcast_to(hist_ref[...], (bb, POOL_HALO, d))

    h = h_ref[...]
    hn = _rms(h, g_ref[...])
    buf_ref[:, base:end, :] = hn
    if not has_history:
        pos = t * tm + lax.broadcasted_iota(jnp.int32, (1, tm, 1), 1)
    ys = []
    for gi, w in enumerate(POOL_WINDOWS):
        c0, c1 = gi * pg, (gi + 1) * pg
        src, dst, shift, lo = buf_ref, sa_ref, 1, SUBLANES
        while shift < w:
            last = 2 * shift == w
            lo_k = base if last else lo
            s = src[:, lo_k:end, c0:c1] + src[:, lo_k - shift:end - shift, c0:c1]
            if last:
                win = s
            else:
                dst[:, lo_k:end, c0:c1] = s
                src, dst = dst, (sb_ref if dst is sa_ref else sa_ref)
            shift, lo = 2 * shift, lo + SUBLANES
        if has_history:
            mean = win * (1.0 / w)
        else:
            mean = win / jnp.minimum(pos + 1, w).astype(F32)
        ys.append(_mm((mean - hn[:, :, c0:c1]).reshape(bb * tm, pg), w_ref[0, gi], precise))
    y = jnp.concatenate(ys, axis=1).reshape(bb, tm, d) * scale_ref[...]
    out_ref[...] = h + y
    tail = buf_ref[:, end - POOL_HALO:end, :]
    buf_ref[:, POOL_LEAD:base, :] = tail

    @pl.when(t == pl.num_programs(1) - 1)
    def _():
        state_ref[...] = tail


def _pool_mix(h, hist, g, w, layer, scale, *, bb, tm, has_history, precise):
    nb, s, d = h.shape
    halo = POOL_HALO
    pg = d // len(POOL_WINDOWS)
    hist_map = (lambda b, t: (b, 0, 0)) if hist.shape[0] == nb and nb > 1 else (lambda b, t: (0, 0, 0))
    hb = bb if hist.shape[0] == nb and nb > 1 else 1
    out, state = pl.pallas_call(
        functools.partial(_pool_kernel, tm=tm, has_history=has_history, precise=precise),
        out_shape=(jax.ShapeDtypeStruct((nb, s, d), F32), jax.ShapeDtypeStruct((nb, halo, d), F32)),
        grid=(nb // bb, s // tm),
        in_specs=[
            pl.BlockSpec((bb, tm, d), lambda b, t: (b, t, 0)),
            pl.BlockSpec((hb, halo, d), hist_map),
            pl.BlockSpec((1, d), lambda b, t: (0, 0)),
            _layer_spec((len(POOL_WINDOWS), pg, pg), layer),
            pl.BlockSpec((1, d), lambda b, t: (0, 0)),
        ],
        out_specs=(pl.BlockSpec((bb, tm, d), lambda b, t: (b, t, 0)),
                   pl.BlockSpec((bb, halo, d), lambda b, t: (b, 0, 0))),
        scratch_shapes=[pltpu.VMEM((bb, POOL_LEAD + halo + tm, d), F32)] * 3,
        compiler_params=_params("arbitrary", "arbitrary"),
        name="pool_mix",
    )(h, hist, g, w, scale)
    return out, state


def _conv_kernel(h_ref, hist_ref, g_ref, win_ref, cw_ref, wout_ref, out_ref, state_ref, buf_ref, *,
                 tm, precise):
    t = pl.program_id(1)
    bb, _, d = h_ref.shape

    @pl.when(t == 0)
    def _():
        buf_ref[:, 0:SUBLANES, :] = jnp.broadcast_to(hist_ref[...], (bb, SUBLANES, d))

    h = h_ref[...]
    hn = _rms(h, g_ref[...])
    z = _mm(hn.reshape(bb * tm, d), win_ref[0], precise)
    gate_b = z[:, 0:d]
    buf_ref[:, SUBLANES:SUBLANES + tm, :] = (z[:, d:2 * d] * z[:, 2 * d:3 * d]).reshape(bb, tm, d)
    first = SUBLANES - (CONV_WIDTH - 1)
    acc = buf_ref[:, first:first + tm, :] * cw_ref[0:1, :]
    for k in range(1, CONV_WIDTH):
        acc = acc + buf_ref[:, first + k:first + k + tm, :] * cw_ref[k:k + 1, :]
    y = _mm(gate_b * acc.reshape(bb * tm, d), wout_ref[0], precise)
    out_ref[...] = h + y.reshape(bb, tm, d)
    tail = buf_ref[:, tm:tm + SUBLANES, :]
    buf_ref[:, 0:SUBLANES, :] = tail

    @pl.when(t == pl.num_programs(1) - 1)
    def _():
        state_ref[...] = tail


def _conv_mix(h, hist, g, w_in, layer, cw, w_out, *, bb, tm, precise):
    nb, s, d = h.shape
    per_batch = hist.shape[0] == nb and nb > 1
    hist_map = (lambda b, t: (b, 0, 0)) if per_batch else (lambda b, t: (0, 0, 0))
    out, state = pl.pallas_call(
        functools.partial(_conv_kernel, tm=tm, precise=precise),
        out_shape=(jax.ShapeDtypeStruct((nb, s, d), F32), jax.ShapeDtypeStruct((nb, SUBLANES, d), F32)),
        grid=(nb // bb, s // tm),
        in_specs=[
            pl.BlockSpec((bb, tm, d), lambda b, t: (b, t, 0)),
            pl.BlockSpec((bb if per_batch else 1, SUBLANES, d), hist_map),
            pl.BlockSpec((1, d), lambda b, t: (0, 0)),
            _layer_spec((d, 3 * d), layer),
            pl.BlockSpec((CONV_WIDTH, d), lambda b, t: (0, 0)),
            _layer_spec((d, d), layer),
        ],
        out_specs=(pl.BlockSpec((bb, tm, d), lambda b, t: (b, t, 0)),
                   pl.BlockSpec((bb, SUBLANES, d), lambda b, t: (b, 0, 0))),
        scratch_shapes=[pltpu.VMEM((bb, tm + SUBLANES, d), F32)],
        compiler_params=_params("arbitrary", "arbitrary"),
        name="conv_mix",
    )(h, hist, g, w_in, cw, w_out)
    return out, state


def _rope_tables(pos):
    half = HEAD_DIM // 2
    inv = ROPE_THETA ** (-jnp.arange(half, dtype=F32) / half)
    ang = pos.astype(F32)[:, None] * inv[None, :]
    cos, sin = jnp.cos(ang), jnp.sin(ang)
    reps = LANES // HEAD_DIM
    return (jnp.tile(jnp.concatenate([cos, cos], axis=1), (1, reps)),
            jnp.tile(jnp.concatenate([-sin, sin], axis=1), (1, reps)))


def _rope_block(blk, cos, sin):
    half = HEAD_DIM // 2
    lane = lax.broadcasted_iota(jnp.int32, (1, LANES), 1)
    partner = jnp.where((lane % HEAD_DIM) < half, pltpu.roll(blk, LANES - half, 1), pltpu.roll(blk, half, 1))
    return blk * cos + partner * sin


def _qkv_small_kernel(h_ref, g_ref, w_ref, cos_ref, sin_ref, q_ref, k_ref, v_ref):
    bb, s, d = h_ref.shape
    kvd = k_ref.shape[-1]
    hn = _rms(h_ref[...], g_ref[...]).reshape(bb * s, d)
    z = _mm(hn, w_ref[0], True)
    cos, sin = cos_ref[...], sin_ref[...]
    q = [_rope_block(z[:, j * LANES:(j + 1) * LANES], cos, sin) for j in range(d // LANES)]
    k = [_rope_block(z[:, d + j * LANES:d + (j + 1) * LANES], cos, sin) for j in range(kvd // LANES)]
    q_ref[...] = jnp.concatenate(q, axis=1).reshape(bb, s, d)
    k_ref[...] = jnp.concatenate(k, axis=1).reshape(bb, s, kvd)
    v_ref[...] = z[:, d + kvd:d + 2 * kvd].reshape(bb, s, kvd)


def _qkv_small(h, g, w_qkv, layer, pos):
    nb, s, d = h.shape
    kvd = N_KV_HEADS * HEAD_DIM
    cos, sin = _rope_tables(pos)
    cos, sin = jnp.tile(cos, (nb, 1)), jnp.tile(sin, (nb, 1))
    full = lambda shape: pl.BlockSpec(shape, lambda i: (0,) * len(shape))
    return pl.pallas_call(
        _qkv_small_kernel,
        out_shape=(jax.ShapeDtypeStruct((nb, s, d), F32), jax.ShapeDtypeStruct((nb, s, kvd), F32),
                   jax.ShapeDtypeStruct((nb, s, kvd), F32)),
        grid=(1,),
        in_specs=[full((nb, s, d)), full((1, d)), _layer_spec((d, d + 2 * kvd), layer),
                  full((nb * s, LANES)), full((nb * s, LANES))],
        out_specs=(full((nb, s, d)), full((nb, s, kvd)), full((nb, s, kvd))),
        compiler_params=_params("arbitrary"),
        name="qkv_small",
    )(h, g, w_qkv, cos, sin)


QT_TILE = 128
KPAD = LANES


def _qkv_t_kernel(h_ref, g_ref, wqt_ref, wk_ref, wvt_ref, wv_ref, cos_ref, sin_ref, cost_ref, sint_ref,
                  qt_ref, kpad_ref, vt_ref, kst_ref, vst_ref):
    t = pl.program_id(1)
    tm = h_ref.shape[1]
    half = HEAD_DIM // 2
    lane = lax.broadcasted_iota(jnp.int32, (1, LANES), 1)
    nparts = 2 if tm % (2 * LANES) == 0 else 1
    tp = tm // nparts
    parts = [slice(i * tp, (i + 1) * tp) for i in range(nparts)]
    hb = [_rms(h_ref[0, p, :], g_ref[...]).astype(BF16) for p in parts]
    zq = [_mm_nt(wqt_ref[...], x, False) for x in hb]
    zk = [jnp.dot(x, wk_ref[...], preferred_element_type=F32) for x in hb]
    zv = [_mm_nt(wvt_ref[...], x, False) for x in hb]
    kr = []
    for i, p in enumerate(parts):
        cost, sint = cost_ref[:, p], sint_ref[:, p]
        for hd in range(zq[i].shape[0] // HEAD_DIM):
            x1 = zq[i][hd * HEAD_DIM:hd * HEAD_DIM + half]
            x2 = zq[i][hd * HEAD_DIM + half:(hd + 1) * HEAD_DIM]
            qt_ref[0, hd * HEAD_DIM:hd * HEAD_DIM + half, p] = (x1 * cost - x2 * sint).astype(BF16)
            qt_ref[0, hd * HEAD_DIM + half:(hd + 1) * HEAD_DIM, p] = (x2 * cost + x1 * sint).astype(BF16)
        vt_ref[0, :, p] = zv[i].astype(BF16)
        blks = []
        for j in range(zk[i].shape[1] // LANES):
            blk = _rope_block(zk[i][:, j * LANES:(j + 1) * LANES], cos_ref[p, :], sin_ref[p, :])
            blks.append(blk)
            for sub in range(LANES // HEAD_DIM):
                hk = j * (LANES // HEAD_DIM) + sub
                shifted = blk if sub == 0 else pltpu.roll(blk, LANES - sub * HEAD_DIM, 1)
                kpad_ref[0, p, hk * KPAD:(hk + 1) * KPAD] = jnp.where(lane < HEAD_DIM, shifted, 0.0).astype(BF16)
        kr.append(jnp.concatenate(blks, axis=1))

    @pl.when(t == pl.num_programs(1) - 1)
    def _():
        kst_ref[0] = jnp.concatenate(kr, axis=0)[tm - WINDOW:]
        vst_ref[0] = jnp.dot(hb[-1][tp - WINDOW:], wv_ref[...], preferred_element_type=F32)


def _qkv_t(h, g, w_qkv, pos, *, tm):
    nb, s, d = h.shape
    kvd = N_KV_HEADS * HEAD_DIM
    half = HEAD_DIM // 2
    cos, sin = _rope_tables(pos)
    inv = ROPE_THETA ** (-jnp.arange(half, dtype=F32) / half)
    ang = inv[:, None] * pos.astype(F32)[None, :]
    cost, sint = jnp.cos(ang), jnp.sin(ang)
    wq_t = (w_qkv[:, :d] * (HEAD_DIM ** -0.5)).T.astype(BF16)
    wk = w_qkv[:, d:d + kvd].astype(BF16)
    wv = w_qkv[:, d + kvd:].astype(BF16)
    const = lambda shape: pl.BlockSpec(shape, lambda b, t: (0,) * len(shape))
    return pl.pallas_call(
        _qkv_t_kernel,
        out_shape=(jax.ShapeDtypeStruct((nb, d, s), BF16),
                   jax.ShapeDtypeStruct((nb, s, N_KV_HEADS * KPAD), BF16),
                   jax.ShapeDtypeStruct((nb, kvd, s), BF16),
                   jax.ShapeDtypeStruct((nb, WINDOW, kvd), F32),
                   jax.ShapeDtypeStruct((nb, WINDOW, kvd), F32)),
        grid=(nb, s // tm),
        in_specs=[
            pl.BlockSpec((1, tm, d), lambda b, t: (b, t, 0)),
            const((1, d)), const((d, d)), const((d, kvd)), const((kvd, d)), const((d, kvd)),
            pl.BlockSpec((tm, LANES), lambda b, t: (t, 0)),
            pl.BlockSpec((tm, LANES), lambda b, t: (t, 0)),
            pl.BlockSpec((half, tm), lambda b, t: (0, t)),
            pl.BlockSpec((half, tm), lambda b, t: (0, t)),
        ],
        out_specs=(pl.BlockSpec((1, d, tm), lambda b, t: (b, 0, t)),
                   pl.BlockSpec((1, tm, N_KV_HEADS * KPAD), lambda b, t: (b, t, 0)),
                   pl.BlockSpec((1, kvd, tm), lambda b, t: (b, 0, t)),
                   pl.BlockSpec((1, WINDOW, kvd), lambda b, t: (b, 0, 0)),
                   pl.BlockSpec((1, WINDOW, kvd), lambda b, t: (b, 0, 0))),
        compiler_params=_params("arbitrary", "arbitrary"),
        name="qkv_t",
    )(h, g, wq_t, wk, wv.T, wv, cos, sin, cost, sint)


def _attn_t_kernel(sinks_ref, qt_ref, kc_ref, kp_ref, vc_ref, vp_ref, mk_ref, mvt_ref, h_ref, wo_ref,
                   out_ref, ot_ref, *, tq):
    t = pl.program_id(1)
    gqa = qt_ref.shape[1] // (N_KV_HEADS * HEAD_DIM)
    band = QT_TILE + WIN_CHUNKS * CHUNK
    kk = jnp.concatenate([kp_ref[0], kc_ref[0]], axis=0)
    vv = jnp.concatenate([vp_ref[0], vc_ref[0]], axis=1)
    r = lax.broadcasted_iota(jnp.int32, (band, gqa * QT_TILE), 0)
    ln = lax.broadcasted_iota(jnp.int32, (band, gqa * QT_TILE), 1)
    kchunk = r // CHUNK
    qchunk = (ln // CHUNK) % (QT_TILE // CHUNK)
    visible = (kchunk >= qchunk) & (kchunk <= qchunk + WIN_CHUNKS)
    lgroup = lax.broadcasted_iota(jnp.int32, (1, gqa * QT_TILE), 1) // QT_TILE

    def project(p0, p1):
        proj = lax.dot_general(ot_ref[:, p0:p1], wo_ref[0], (((0,), (0,)), ((), ())), preferred_element_type=F32)
        out_ref[0, p0:p1, :] = h_ref[0, p0:p1, :] + proj

    projected = done = 0
    for sub in range(tq // QT_TILE):
        c0 = sub * QT_TILE
        if sub == 0:
            mask = visible & ((t > 0) | (r >= WIN_CHUNKS * CHUNK))
        else:
            mask = visible
        krows = kk[c0:c0 + band]
        vcols = vv[:, c0:c0 + band]
        heads = range(N_KV_HEADS)
        q4 = [jnp.concatenate(
            [qt_ref[0, (hk * gqa + g) * HEAD_DIM:(hk * gqa + g + 1) * HEAD_DIM, c0:c0 + QT_TILE]
             for g in range(gqa)], axis=1) for hk in heads]
        sb = [jnp.dot(krows[:, hk * KPAD:hk * KPAD + HEAD_DIM], q4[hk], preferred_element_type=F32)
              for hk in heads]
        sm = [jnp.dot(mk_ref[:, hk * KPAD:hk * KPAD + HEAD_DIM], q4[hk], preferred_element_type=F32)
              for hk in heads]
        sb = [jnp.where(mask, x, NEG_INF) for x in sb]
        if done - projected >= QT_TILE:
            project(projected, done)
            projected = done
        sink = []
        for hk in heads:
            row = jnp.zeros((1, gqa * QT_TILE), F32)
            for g in range(gqa):
                row = jnp.where(lgroup == g, sinks_ref[hk * gqa + g], row)
            sink.append(row)
        m = [jnp.maximum(jnp.maximum(jnp.max(sb[hk], axis=0, keepdims=True),
                                     jnp.max(sm[hk], axis=0, keepdims=True)), sink[hk]) for hk in heads]
        pb = [jnp.exp(sb[hk] - m[hk]) for hk in heads]
        pm = [jnp.exp(sm[hk] - m[hk]) for hk in heads]
        denom = [jnp.sum(pb[hk], axis=0, keepdims=True) + jnp.sum(pm[hk], axis=0, keepdims=True)
                 + jnp.exp(sink[hk] - m[hk]) for hk in heads]
        o = [(jnp.dot(vcols[hk * HEAD_DIM:(hk + 1) * HEAD_DIM], pb[hk].astype(BF16), preferred_element_type=F32)
              + jnp.dot(mvt_ref[hk * HEAD_DIM:(hk + 1) * HEAD_DIM, :], pm[hk].astype(BF16),
                        preferred_element_type=F32)) / denom[hk] for hk in heads]
        for hk in heads:
            for g in range(gqa):
                ot_ref[(hk * gqa + g) * HEAD_DIM:(hk * gqa + g + 1) * HEAD_DIM, c0:c0 + QT_TILE] = (
                    o[hk][:, g * QT_TILE:(g + 1) * QT_TILE].astype(BF16))
        done = (sub + 1) * QT_TILE
    project(projected, done)


def _attn_t(qt, kpad, vt, mkpad, mvt, sinks, h, w_o, layer, *, tq):
    nb, s, d = h.shape
    kvd = vt.shape[1]
    prev = WIN_CHUNKS * CHUNK
    ratio = tq // prev
    const = lambda shape: pl.BlockSpec(shape, lambda b, t: (0,) * len(shape))
    return pl.pallas_call(
        functools.partial(_attn_t_kernel, tq=tq),
        out_shape=jax.ShapeDtypeStruct((nb, s, d), F32),
        grid=(nb, s // tq),
        in_specs=[
            pl.BlockSpec(memory_space=pltpu.SMEM),
            pl.BlockSpec((1, d, tq), lambda b, t: (b, 0, t)),
            pl.BlockSpec((1, tq, N_KV_HEADS * KPAD), lambda b, t: (b, t, 0)),
            pl.BlockSpec((1, prev, N_KV_HEADS * KPAD), lambda b, t: (b, jnp.maximum(t * ratio - 1, 0), 0)),
            pl.BlockSpec((1, kvd, tq), lambda b, t: (b, 0, t)),
            pl.BlockSpec((1, kvd, prev), lambda b, t: (b, 0, jnp.maximum(t * ratio - 1, 0))),
            const((N_META, N_KV_HEADS * KPAD)), const((kvd, N_META)),
            pl.BlockSpec((1, tq, d), lambda b, t: (b, t, 0)),
            _layer_spec((d, d), layer),
        ],
        out_specs=pl.BlockSpec((1, tq, d), lambda b, t: (b, t, 0)),
        scratch_shapes=[pltpu.VMEM((d, tq), BF16)],
        compiler_params=_params("arbitrary", "arbitrary"),
        name="attn_t",
    )(sinks, qt, kpad, kpad, vt, vt, mkpad, mvt, h, w_o)


def _attn_small_kernel(sinks_ref, q_ref, k_ref, v_ref, h_ref, wo_ref, out_ref, o_ref):
    bb, s, d = q_ref.shape
    gqa = d // (N_KV_HEADS * HEAD_DIM)
    for b in range(bb):
        for hk in range(N_KV_HEADS):
            hs = slice(hk * HEAD_DIM, (hk + 1) * HEAD_DIM)
            qh = jnp.concatenate(
                [q_ref[b, :, (hk * gqa + g) * HEAD_DIM:(hk * gqa + g + 1) * HEAD_DIM] for g in range(gqa)],
                axis=0)
            sink = jnp.concatenate([jnp.full((s, 1), sinks_ref[hk * gqa + g], F32) for g in range(gqa)], axis=0)
            sc = _mm_nt(qh, k_ref[b, :, hs], True) * (HEAD_DIM ** -0.5)
            m = jnp.maximum(jnp.max(sc, axis=-1, keepdims=True), sink)
            p = jnp.exp(sc - m)
            denom = jnp.sum(p, axis=-1, keepdims=True) + jnp.exp(sink - m)
            o = _mm(p, v_ref[b, :, hs], True) / denom
            for g in range(gqa):
                o_ref[b * s:(b + 1) * s, (hk * gqa + g) * HEAD_DIM:(hk * gqa + g + 1) * HEAD_DIM] = (
                    o[g * s:(g + 1) * s])
    out_ref[...] = h_ref[...] + _mm(o_ref[...], wo_ref[0], True).reshape(bb, s, d)


def _attn_small(q, keys, vals, sinks, h, w_o, layer):
    nb, s, d = h.shape
    kn, kvd = keys.shape[1:]
    full = lambda shape: pl.BlockSpec(shape, lambda i: (0,) * len(shape))
    return pl.pallas_call(
        _attn_small_kernel,
        out_shape=jax.ShapeDtypeStruct((nb, s, d), F32),
        grid=(1,),
        in_specs=[pl.BlockSpec(memory_space=pltpu.SMEM), full((nb, s, d)), full((nb, kn, kvd)),
                  full((nb, kn, kvd)), full((nb, s, d)), _layer_spec((d, d), layer)],
        out_specs=full((nb, s, d)),
        scratch_shapes=[pltpu.VMEM((nb * s, d), F32)],
        compiler_params=_params("arbitrary"),
        name="attn_small",
    )(sinks, q, keys, vals, h, w_o)


ROUTE_ROWS = 32
MOE_CHUNK = 128
MOE_CALL_ROWS = (128, 144, 160)
SEG_ALIGN = 16


def _route_t(lt):
    n = lt.shape[1]
    row8 = lax.broadcasted_iota(jnp.int32, (SUBLANES, n), 0)
    lg = jnp.where(row8 < N_GROUPS, lt[0:SUBLANES], NEG_INF)
    gmax = jnp.max(lg, axis=0, keepdims=True)
    g_idx = jnp.min(jnp.where(lg == gmax, row8, SUBLANES), axis=0, keepdims=True)
    g_w = 1.0 / jnp.sum(jnp.exp(lg - gmax), axis=0, keepdims=True)
    le = lt[SUBLANES:SUBLANES + N_EXPERTS]
    row16 = lax.broadcasted_iota(jnp.int32, (N_EXPERTS, n), 0)
    in_group = (row16 // EXPERTS_PER_GROUP) == g_idx
    l1 = jnp.where(in_group, le, NEG_INF)
    m1 = jnp.max(l1, axis=0, keepdims=True)
    i1 = jnp.min(jnp.where(in_group & (l1 == m1), row16, N_EXPERTS), axis=0, keepdims=True)
    rest = in_group & (row16 != i1)
    l2 = jnp.where(rest, le, NEG_INF)
    m2 = jnp.max(l2, axis=0, keepdims=True)
    i2 = jnp.min(jnp.where(rest & (l2 == m2), row16, N_EXPERTS), axis=0, keepdims=True)
    e2 = jnp.exp(m2 - m1)
    p1 = 1.0 / (1.0 + e2)
    comb = g_w * (jnp.where(row16 == i1, p1, 0.0) + jnp.where(row16 == i2, e2 * p1, 0.0))
    c8 = comb[0:SUBLANES] + comb[SUBLANES:2 * SUBLANES]
    return g_idx, comb, c8 + pltpu.roll(c8, EXPERTS_PER_GROUP, 0)


def _route(logits):
    col = lax.broadcasted_iota(jnp.int32, logits.shape, 1)
    lg = jnp.where(col < N_GROUPS, logits, NEG_INF)
    gmax = jnp.max(lg, axis=-1, keepdims=True)
    g_idx = jnp.min(jnp.where(lg == gmax, col, LANES), axis=-1, keepdims=True)
    g_w = 1.0 / jnp.sum(jnp.exp(lg - gmax), axis=-1, keepdims=True)
    ecol = col - N_GROUPS
    in_group = (ecol >= 0) & (ecol < N_EXPERTS) & ((ecol // EXPERTS_PER_GROUP) == g_idx)
    l1 = jnp.where(in_group, logits, NEG_INF)
    m1 = jnp.max(l1, axis=-1, keepdims=True)
    i1 = jnp.min(jnp.where(in_group & (l1 == m1), col, LANES), axis=-1, keepdims=True)
    rest = in_group & (col != i1)
    l2 = jnp.where(rest, logits, NEG_INF)
    m2 = jnp.max(l2, axis=-1, keepdims=True)
    i2 = jnp.min(jnp.where(rest & (l2 == m2), col, LANES), axis=-1, keepdims=True)
    e2 = jnp.exp(m2 - m1)
    p1 = 1.0 / (1.0 + e2)
    return g_w * (jnp.where(col == i1, p1, 0.0) + jnp.where(col == i2, e2 * p1, 0.0))


def _router_logits_t(wr_ref, br_ref, hn_hi, hn_lo):
    a = _dot(wr_ref[0], hn_hi, (((1,), (1,)), ((), ())))
    b = _dot(wr_ref[0, 0:ROUTE_ROWS], hn_lo, (((1,), (1,)), ((), ())))
    return a[0:ROUTE_ROWS] + a[ROUTE_ROWS:2 * ROUTE_ROWS] + b + br_ref[0]


def _moe_small_kernel(h_ref, g_ref, wr_ref, br_ref, wg_ref, wu_ref, wd_ref, gf_ref,
                      out_ref, wgb_ref, wub_ref, wdb_ref, hn_ref, comb_ref, acc_ref, *, final_norm):
    e = pl.program_id(0)

    @pl.when(e == 0)
    def _():
        hn = _rms(h_ref[...], g_ref[...])
        hn_ref[0], hn_ref[1] = _split_bf16(hn, 2)
        comb_ref[...] = _route(_mm(hn, wr_ref[0], True) + br_ref[0])
        acc_ref[...] = jnp.zeros_like(acc_ref)

    dn = (((1,), (0,)), ((), ()))

    def x_times(w_ref, wb_ref):
        w_hi, w_lo = _split_bf16(w_ref[0, 0], 2)
        wb_ref[0] = w_hi
        return _dot(hn_ref[0], w_hi, dn) + (_dot(hn_ref[0], w_lo, dn) + _dot(hn_ref[1], w_hi, dn))

    col = lax.broadcasted_iota(jnp.int32, comb_ref.shape, 1)
    c = jnp.sum(jnp.where(col == e + N_GROUPS, comb_ref[...], 0.0), axis=-1, keepdims=True)
    gate = x_times(wg_ref, wgb_ref)
    up = x_times(wu_ref, wub_ref)
    a_hi, a_lo = _split_bf16(gate * jax.nn.sigmoid(gate) * up * c, 2)
    wd_hi, wd_lo = _split_bf16(wd_ref[0, 0], 2)
    wdb_ref[0] = wd_hi
    acc_ref[...] += _dot(a_hi, wd_hi, dn) + (_dot(a_hi, wd_lo, dn) + _dot(a_lo, wd_hi, dn))

    @pl.when(e == pl.num_programs(0) - 1)
    def _():
        y = h_ref[...] + acc_ref[...]
        if final_norm:
            y = _rms(y, gf_ref[...])
        out_ref[...] = y


def _moe_small(h, g, wr, br, wg, wu, wd, layer, gf, *, final_norm):
    n, d = h.shape
    _, ne, _, de = wg.shape
    const = lambda shape: pl.BlockSpec(shape, lambda e: (0,) * len(shape))
    return pl.pallas_call(
        functools.partial(_moe_small_kernel, final_norm=final_norm),
        out_shape=(jax.ShapeDtypeStruct((n, d), F32), jax.ShapeDtypeStruct((ne, d, de), BF16),
                   jax.ShapeDtypeStruct((ne, d, de), BF16), jax.ShapeDtypeStruct((ne, de, d), BF16)),
        grid=(ne,),
        in_specs=[
            const((n, d)), const((1, d)),
            _layer_spec((d, LANES), layer), _layer_spec((1, LANES), layer),
            pl.BlockSpec((1, 1, d, de), lambda e: (layer, e, 0, 0)),
            pl.BlockSpec((1, 1, d, de), lambda e: (layer, e, 0, 0)),
            pl.BlockSpec((1, 1, de, d), lambda e: (layer, e, 0, 0)),
            const((1, d)),
        ],
        out_specs=(const((n, d)), pl.BlockSpec((1, d, de), lambda e: (e, 0, 0)),
                   pl.BlockSpec((1, d, de), lambda e: (e, 0, 0)), pl.BlockSpec((1, de, d), lambda e: (e, 0, 0))),
        scratch_shapes=[pltpu.VMEM((2, n, d), BF16), pltpu.VMEM((n, LANES), F32), pltpu.VMEM((n, d), F32)],
        compiler_params=_params("arbitrary"),
        name="moe_small",
    )(h, g, wr, br, wg, wu, wd, gf)


def _moe_sparse_kernel(h_ref, g_ref, wr_ref, br_ref, tri_ref, wg_ref, wu_ref, wd_ref, gf_ref, out_ref,
                       xs_ref, ys_ref, p_ref, cs_ref, *, tm, nsub, final_norm):
    rows = p_ref.shape[1]

    @pl.when(pl.program_id(0) == 0)
    def _():
        xs_ref[:, rows:, :] = jnp.zeros((nsub, xs_ref.shape[1] - rows, xs_ref.shape[2]), BF16)
        cs_ref[:, rows:, :] = jnp.zeros((nsub, cs_ref.shape[1] - rows, cs_ref.shape[2]), F32)

    def prep():
        tiles = range(nsub)
        row8 = lax.broadcasted_iota(jnp.int32, (SUBLANES, tm), 0)
        riota = lax.broadcasted_iota(jnp.int32, (rows, tm), 0)
        hn = [_rms(h_ref[t * tm:(t + 1) * tm, :], g_ref[...]) for t in tiles]
        hn_split = [_split_bf16(x, 2) for x in hn]
        logits = [_router_logits_t(wr_ref, br_ref, hi, lo) for hi, lo in hn_split]
        routed = [_route_t(lt) for lt in logits]
        onehot = [row8 == g_idx for g_idx, _, _ in routed]
        incl = [jnp.dot(jnp.where(oh, 1.0, 0.0).astype(BF16), tri_ref[...], preferred_element_type=F32)
                for oh in onehot]
        counts = [x[:, tm - 1:tm].astype(jnp.int32) for x in incl]
        plans = []
        for t in tiles:
            n = [counts[t][g, 0] for g in range(N_GROUPS)]
            starts = [jnp.int32(0)]
            for g in range(N_GROUPS - 1):
                starts.append(starts[-1] + (n[g] + SEG_ALIGN - 1) // SEG_ALIGN * SEG_ALIGN)
            plans.append((n, starts))
        for t in tiles:
            g_idx = routed[t][0]
            rank = jnp.sum(jnp.where(onehot[t], incl[t], 0.0), axis=0, keepdims=True).astype(jnp.int32) - 1
            start_tok = jnp.zeros_like(g_idx)
            for g in range(1, N_GROUPS):
                start_tok = jnp.where(g_idx == g, plans[t][1][g], start_tok)
            p_ref[t] = jnp.where(riota == start_tok + rank, 1.0, 0.0).astype(BF16)
        for t in tiles:
            xs_ref[t, 0:rows, :] = jnp.dot(p_ref[t], hn_split[t][0], preferred_element_type=F32).astype(BF16)
            comb_parts = jnp.concatenate(_split_bf16(routed[t][2], 3), axis=0)
            cs = _dot(p_ref[t], comb_parts, (((1,), (1,)), ((), ())))
            cs_ref[t, 0:rows, :] = (cs[:, 0:SUBLANES] + cs[:, SUBLANES:2 * SUBLANES]
                                    + cs[:, 2 * SUBLANES:3 * SUBLANES])
            ys_ref[t, 0:rows, :] = jnp.zeros((rows, ys_ref.shape[2]), BF16)
        return plans

    def experts(t, g, r0, m):
        x = xs_ref[t, pl.ds(r0, m), :]
        cc = cs_ref[t, pl.ds(r0, m), :]
        acts = []
        for j in range(EXPERTS_PER_GROUP):
            e = g * EXPERTS_PER_GROUP + j
            gate = jnp.dot(x, wg_ref[e], preferred_element_type=F32)
            up = jnp.dot(x, wu_ref[e], preferred_element_type=F32)
            acts.append((gate * jax.nn.sigmoid(gate) * up * cc[:, j:j + 1]).astype(BF16))
        y = jnp.dot(jnp.concatenate(acts, axis=1), wd_ref[g], preferred_element_type=F32)
        ys_ref[t, pl.ds(r0, m), :] = y.astype(BF16)

    plans = prep()

    for t, (n, starts) in enumerate(plans):
        for g in range(N_GROUPS):
            seg = pl.multiple_of(starts[g], SEG_ALIGN)
            lo = 0
            for m in MOE_CALL_ROWS:
                @pl.when((n[g] > lo) & (n[g] <= m))
                def _(t=t, g=g, seg=seg, m=m):
                    experts(t, g, seg, m)

                lo = m

            @pl.when(n[g] > MOE_CALL_ROWS[-1])
            def _(t=t, g=g, n=n, starts=starts):
                def chunk(c, carry):
                    experts(t, g, pl.multiple_of(starts[g] + c * MOE_CHUNK, SEG_ALIGN), MOE_CHUNK)
                    return carry

                lax.fori_loop(0, (n[g] + MOE_CHUNK - 1) // MOE_CHUNK, chunk, 0)

    for t in range(nsub):
        back = lax.dot_general(p_ref[t], ys_ref[t, 0:rows, :], (((0,), (0,)), ((), ())),
                               preferred_element_type=F32)
        y = h_ref[t * tm:(t + 1) * tm, :] + back
        if final_norm:
            y = _rms(y, gf_ref[...])
        out_ref[t * tm:(t + 1) * tm, :] = y


def _moe_sparse(h, g, wr2, brt, tri, wg, wu, wd4, layer, gf, *, tm, nsub, final_norm):
    n, d = h.shape
    ne, _, de = wg.shape
    rows = tm + N_GROUPS * SEG_ALIGN
    over = rows + MOE_CHUNK
    resident = lambda shape: pl.BlockSpec(shape, lambda i: (0,) * len(shape), pipeline_mode=pl.Buffered(1))
    const = lambda shape: pl.BlockSpec(shape, lambda i: (0,) * len(shape))
    return pl.pallas_call(
        functools.partial(_moe_sparse_kernel, tm=tm, nsub=nsub, final_norm=final_norm),
        out_shape=jax.ShapeDtypeStruct((n, d), F32),
        grid=(n // (tm * nsub),),
        in_specs=[
            pl.BlockSpec((tm * nsub, d), lambda i: (i, 0)), const((1, d)),
            _layer_spec((2 * ROUTE_ROWS, d), layer), _layer_spec((ROUTE_ROWS, 1), layer),
            const((tm, tm)),
            resident((ne, d, de)), resident((ne, d, de)), resident((N_GROUPS, EXPERTS_PER_GROUP * de, d)),
            const((1, d)),
        ],
        out_specs=pl.BlockSpec((tm * nsub, d), lambda i: (i, 0)),
        scratch_shapes=[pltpu.VMEM((nsub, over, d), BF16), pltpu.VMEM((nsub, over, d), BF16),
                        pltpu.VMEM((nsub, rows, tm), BF16), pltpu.VMEM((nsub, over, SUBLANES), F32)],
        compiler_params=pltpu.CompilerParams(dimension_semantics=("arbitrary",),
                                             vmem_limit_bytes=MOE_VMEM_LIMIT),
        name="moe_sparse",
    )(h, g, wr2, brt, tri, wg, wu, wd4, gf)


def kernel(x_prompt, x_sample, state_pool, cache_swa_kv, cache_meta_kv, state_conv, meta_tokens, norm_mix, norm_ffn, norm_final, w_pool, pool_scale, w_qkv, w_o, attn_sinks, w_conv_in, conv_w, w_conv_out, w_group, b_group, w_expert_router, b_expert_router, w_gate, w_up, w_down):
    nb, seq, d = x_prompt.shape
    db, dseq, _ = x_sample.shape
    depth = norm_mix.shape[0]
    kvd = N_KV_HEADS * HEAD_DIM
    tm_main = min(512, seq)
    tq_main = min(512, seq)
    moe_sub = 2 if (nb * seq) % (2 * tm_main) == 0 else 1
    halo = POOL_STATE + 1

    row = lambda a: a.reshape(1, -1).astype(F32)
    rpad = lambda a, k: jnp.pad(a, ((0, 0), (0, k)) + ((0, 0),) * (a.ndim - 2))
    wrt = jnp.concatenate([rpad(jnp.swapaxes(w_group, 1, 2), SUBLANES - N_GROUPS),
                           rpad(jnp.swapaxes(w_expert_router, 1, 2), ROUTE_ROWS - SUBLANES - N_EXPERTS)], axis=1)
    wr2 = jnp.concatenate(_split_bf16(wrt, 2), axis=1)
    brt = jnp.concatenate([rpad(b_group, SUBLANES - N_GROUPS),
                           rpad(b_expert_router, ROUTE_ROWS - SUBLANES - N_EXPERTS)], axis=1)[..., None].astype(F32)
    wr = jnp.pad(jnp.concatenate([w_group, w_expert_router], axis=-1).astype(F32),
                 ((0, 0), (0, 0), (0, LANES - N_GROUPS - N_EXPERTS)))
    br = jnp.pad(jnp.concatenate([b_group, b_expert_router], axis=-1).astype(F32),
                 ((0, 0), (0, LANES - N_GROUPS - N_EXPERTS)))[:, None, :]
    tri = jnp.triu(jnp.ones((tm_main, tm_main), BF16))
    bf = lambda a: a.astype(BF16)
    w_pool_b, w_o_b = bf(w_pool), bf(w_o)
    w_conv_in_b, w_conv_out_b = bf(w_conv_in), bf(w_conv_out)

    hm = meta_tokens.astype(F32)[None]
    hp = x_prompt
    hs = x_sample
    pool_p, swa_p, meta_p, conv_p, pool_s, swa_s, conv_s = [], [], [], [], [], [], []
    for i in range(depth):
        j = i // N_MIXERS
        g = row(norm_mix[i])
        if i % N_MIXERS == 0:
            sc = row(pool_scale[j])
            hm, st_m = _pool_mix(hm, jnp.zeros((1, halo, d), F32), g, w_pool, j, sc, bb=1, tm=N_META,
                                 has_history=False, precise=True)
            hp, st_p = _pool_mix(hp, st_m, g, w_pool_b, j, sc, bb=1, tm=tm_main, has_history=True,
                                 precise=False)
            hist_s = jnp.pad(state_pool[j].astype(F32), ((0, 0), (1, 0), (0, 0)))
            hs, st_s = _pool_mix(hs, hist_s, g, w_pool, j, sc, bb=db, tm=dseq, has_history=True,
                                 precise=True)
            pool_p.append(st_p[:, 1:])
            pool_s.append(st_s[:, 1:])
        elif i % N_MIXERS == 1:
            sinks = attn_sinks[j].astype(F32)
            qm, km, vm = _qkv_small(hm, g, w_qkv, j, jnp.arange(N_META))
            qt, kpad, vt, kst, vst = _qkv_t(hp, g, w_qkv[j], N_META + jnp.arange(seq), tm=tm_main)
            qs, ks, vs = _qkv_small(hs, g, w_qkv, j, PAST_LEN + N_META + jnp.arange(dseq))
            hm = _attn_small(qm, km, vm, sinks, hm, w_o, j)
            mkpad = jnp.pad(km[0].reshape(N_META, N_KV_HEADS, HEAD_DIM),
                            ((0, 0), (0, 0), (0, KPAD - HEAD_DIM))).reshape(N_META, -1).astype(BF16)
            hp = _attn_t(qt, kpad, vt, mkpad, vm[0].T.astype(BF16), sinks, hp, w_o_b, j, tq=tq_main)
            flat = lambda a: a.reshape(a.shape[0], a.shape[1], kvd)
            keys = jnp.concatenate([flat(cache_meta_kv[j][:, :, 0]), flat(cache_swa_kv[j][:, :, 0]), ks], axis=1)
            vals = jnp.concatenate([flat(cache_meta_kv[j][:, :, 1]), flat(cache_swa_kv[j][:, :, 1]), vs], axis=1)
            hs = _attn_small(qs, keys, vals, sinks, hs, w_o, j)
            heads = lambda a: a.reshape(a.shape[0], a.shape[1], N_KV_HEADS, HEAD_DIM)
            swa_p.append(jnp.stack([heads(kst), heads(vst)], axis=2))
            meta_kv = jnp.stack([heads(km), heads(vm)], axis=2)
            meta_p.append(jnp.broadcast_to(meta_kv, (nb,) + meta_kv.shape[1:]))
            swa_s.append(jnp.stack([heads(ks), heads(vs)], axis=2))
        else:
            cw = conv_w[j].astype(F32)
            hm, st_m = _conv_mix(hm, jnp.zeros((1, SUBLANES, d), F32), g, w_conv_in, j, cw, w_conv_out,
                                 bb=1, tm=N_META, precise=True)
            hp, st_p = _conv_mix(hp, st_m, g, w_conv_in_b, j, cw, w_conv_out_b, bb=1, tm=tm_main,
                                 precise=False)
            hist_s = jnp.pad(state_conv[j].astype(F32), ((0, 0), (SUBLANES - (CONV_WIDTH - 1), 0), (0, 0)))
            hs, st_s = _conv_mix(hs, hist_s, g, w_conv_in, j, cw, w_conv_out, bb=db, tm=dseq, precise=True)
            conv_p.append(st_p[:, SUBLANES - (CONV_WIDTH - 1):])
            conv_s.append(st_s[:, SUBLANES - (CONV_WIDTH - 1):])

        final = i == depth - 1
        gf = row(norm_final)
        gn = row(norm_ffn[i])
        small = jnp.concatenate([hm.reshape(-1, d), hs.reshape(-1, d)], axis=0)
        small, wg_b, wu_b, wd_b = _moe_small(small, gn, wr, br, w_gate, w_up, w_down, i, gf, final_norm=final)
        hm = small[:N_META].reshape(1, N_META, d)
        hs = small[N_META:].reshape(db, dseq, d)
        hp = _moe_sparse(hp.reshape(-1, d), gn, wr2, brt, tri, wg_b, wu_b, wd_b.reshape(N_GROUPS, -1, d), i, gf,
                         tm=tm_main, nsub=moe_sub, final_norm=final).reshape(nb, seq, d)

    return (hp, hs, jnp.stack(pool_p), jnp.stack(swa_p), jnp.stack(meta_p), jnp.stack(conv_p),
            jnp.stack(pool_s), jnp.stack(swa_s), jnp.stack(conv_s))
```

```python
import functools

import jax
import jax.numpy as jnp
from jax import lax
from jax.experimental import pallas as pl
from jax.experimental.pallas import tpu as pltpu

F32 = jnp.float32
BF16 = jnp.bfloat16

CHUNK = 64
N_META = 16
N_MIXERS = 3
POOL_WINDOWS = (2, 4, 8, 16)
POOL_STATE = max(POOL_WINDOWS) - 1
HEAD_DIM = 64
N_KV_HEADS = 4
WINDOW = 128
WIN_CHUNKS = WINDOW // CHUNK
ROPE_THETA = 10000.0
CONV_WIDTH = 3
N_GROUPS = 4
EXPERTS_PER_GROUP = 4
N_EXPERTS = N_GROUPS * EXPERTS_PER_GROUP
PAST_LEN = 2048
EPS = 1e-6

LANES = 128
SUBLANES = 8
VMEM_LIMIT = 48 * 1024 * 1024
MOE_VMEM_LIMIT = 60 * 1024 * 1024
NEG_INF = float("-inf")


def _params(*sem):
    return pltpu.CompilerParams(dimension_semantics=sem, vmem_limit_bytes=VMEM_LIMIT)


def _split_bf16(x, parts):
    out = []
    x = x.astype(F32)
    for _ in range(parts):
        hi = x.astype(BF16)
        out.append(hi)
        x = x - hi.astype(F32)
    return out


def _dot(a, b, dn):
    return lax.dot_general(a, b, dn, preferred_element_type=F32)


def _mm_dn(a, b, dn, precise):
    if not precise:
        return _dot(a.astype(BF16), b.astype(BF16), dn)
    a_hi, a_lo = _split_bf16(a, 2)
    b_hi, b_lo = _split_bf16(b, 2)
    return _dot(a_hi, b_hi, dn) + (_dot(a_hi, b_lo, dn) + _dot(a_lo, b_hi, dn))


def _mm(a, b, precise):
    return _mm_dn(a, b, (((1,), (0,)), ((), ())), precise)


def _mm_nt(a, b, precise):
    return _mm_dn(a, b, (((1,), (1,)), ((), ())), precise)


def _rms(x, g):
    ms = jnp.mean(x * x, axis=-1, keepdims=True)
    return x * lax.rsqrt(ms + EPS) * g


def _layer_spec(shape, layer):
    nd = len(shape)
    return pl.BlockSpec((1,) + tuple(shape), lambda *_: (layer,) + (0,) * nd)


POOL_HALO = 16
POOL_LEAD = 16


def _pool_kernel(h_ref, hist_ref, g_ref, w_ref, scale_ref, out_ref, state_ref, buf_ref, sa_ref, sb_ref, *,
                 tm, has_history, precise):
    t = pl.program_id(1)
    base = POOL_LEAD + POOL_HALO
    bb, _, d = h_ref.shape
    pg = d // len(POOL_WINDOWS)
    end = base + tm

    @pl.when(t == 0)
    def _():
        buf_ref[:, 0:POOL_LEAD, :] = jnp.zeros((bb, POOL_LEAD, d), F32)
        buf_ref[:, POOL_LEAD:base, :] = jnp.broadcast_to(hist_ref[...], (bb, POOL_HALO, d))

    h = h_ref[...]
    hn = _rms(h, g_ref[...])
    buf_ref[:, base:end, :] = hn
    if not has_history:
        pos = t * tm + lax.broadcasted_iota(jnp.int32, (1, tm, 1), 1)
    ys = []
    for gi, w in enumerate(POOL_WINDOWS):
        c0, c1 = gi * pg, (gi + 1) * pg
        src, dst, shift, lo = buf_ref, sa_ref, 1, SUBLANES
        while shift < w:
            last = 2 * shift == w
            lo_k = base if last else lo
            s = src[:, lo_k:end, c0:c1] + src[:, lo_k - shift:end - shift, c0:c1]
            if last:
                win = s
            else:
                dst[:, lo_k:end, c0:c1] = s
                src, dst = dst, (sb_ref if dst is sa_ref else sa_ref)
            shift, lo = 2 * shift, lo + SUBLANES
        if has_history:
            mean = win * (1.0 / w)
        else:
            mean = win / jnp.minimum(pos + 1, w).astype(F32)
        ys.append(_mm((mean - hn[:, :, c0:c1]).reshape(bb * tm, pg), w_ref[0, gi], precise))
    y = jnp.concatenate(ys, axis=1).reshape(bb, tm, d) * scale_ref[...]
    out_ref[...] = h + y
    tail = buf_ref[:, end - POOL_HALO:end, :]
    buf_ref[:, POOL_LEAD:base, :] = tail

    @pl.when(t == pl.num_programs(1) - 1)
    def _():
        state_ref[...] = tail


def _pool_mix(h, hist, g, w, layer, scale, *, bb, tm, has_history, precise):
    nb, s, d = h.shape
    halo = POOL_HALO
    pg = d // len(POOL_WINDOWS)
    hist_map = (lambda b, t: (b, 0, 0)) if hist.shape[0] == nb and nb > 1 else (lambda b, t: (0, 0, 0))
    hb = bb if hist.shape[0] == nb and nb > 1 else 1
    out, state = pl.pallas_call(
        functools.partial(_pool_kernel, tm=tm, has_history=has_history, precise=precise),
        out_shape=(jax.ShapeDtypeStruct((nb, s, d), F32), jax.ShapeDtypeStruct((nb, halo, d), F32)),
        grid=(nb // bb, s // tm),
        in_specs=[
            pl.BlockSpec((bb, tm, d), lambda b, t: (b, t, 0)),
            pl.BlockSpec((hb, halo, d), hist_map),
            pl.BlockSpec((1, d), lambda b, t: (0, 0)),
            _layer_spec((len(POOL_WINDOWS), pg, pg), layer),
            pl.BlockSpec((1, d), lambda b, t: (0, 0)),
        ],
        out_specs=(pl.BlockSpec((bb, tm, d), lambda b, t: (b, t, 0)),
                   pl.BlockSpec((bb, halo, d), lambda b, t: (b, 0, 0))),
        scratch_shapes=[pltpu.VMEM((bb, POOL_LEAD + halo + tm, d), F32)] * 3,
        compiler_params=_params("arbitrary", "arbitrary"),
        name="pool_mix",
    )(h, hist, g, w, scale)
    return out, state


def _conv_kernel(h_ref, hist_ref, g_ref, win_ref, cw_ref, wout_ref, out_ref, state_ref, buf_ref, *,
                 tm, precise):
    t = pl.program_id(1)
    bb, _, d = h_ref.shape

    @pl.when(t == 0)
    def _():
        buf_ref[:, 0:SUBLANES, :] = jnp.broadcast_to(hist_ref[...], (bb, SUBLANES, d))

    h = h_ref[...]
    hn = _rms(h, g_ref[...])
    z = _mm(hn.reshape(bb * tm, d), win_ref[0], precise)
    gate_b = z[:, 0:d]
    buf_ref[:, SUBLANES:SUBLANES + tm, :] = (z[:, d:2 * d] * z[:, 2 * d:3 * d]).reshape(bb, tm, d)
    first = SUBLANES - (CONV_WIDTH - 1)
    acc = buf_ref[:, first:first + tm, :] * cw_ref[0:1, :]
    for k in range(1, CONV_WIDTH):
        acc = acc + buf_ref[:, first + k:first + k + tm, :] * cw_ref[k:k + 1, :]
    y = _mm(gate_b * acc.reshape(bb * tm, d), wout_ref[0], precise)
    out_ref[...] = h + y.reshape(bb, tm, d)
    tail = buf_ref[:, tm:tm + SUBLANES, :]
    buf_ref[:, 0:SUBLANES, :] = tail

    @pl.when(t == pl.num_programs(1) - 1)
    def _():
        state_ref[...] = tail


def _conv_mix(h, hist, g, w_in, layer, cw, w_out, *, bb, tm, precise):
    nb, s, d = h.shape
    per_batch = hist.shape[0] == nb and nb > 1
    hist_map = (lambda b, t: (b, 0, 0)) if per_batch else (lambda b, t: (0, 0, 0))
    out, state = pl.pallas_call(
        functools.partial(_conv_kernel, tm=tm, precise=precise),
        out_shape=(jax.ShapeDtypeStruct((nb, s, d), F32), jax.ShapeDtypeStruct((nb, SUBLANES, d), F32)),
        grid=(nb // bb, s // tm),
        in_specs=[
            pl.BlockSpec((bb, tm, d), lambda b, t: (b, t, 0)),
            pl.BlockSpec((bb if per_batch else 1, SUBLANES, d), hist_map),
            pl.BlockSpec((1, d), lambda b, t: (0, 0)),
            _layer_spec((d, 3 * d), layer),
            pl.BlockSpec((CONV_WIDTH, d), lambda b, t: (0, 0)),
            _layer_spec((d, d), layer),
        ],
        out_specs=(pl.BlockSpec((bb, tm, d), lambda b, t: (b, t, 0)),
                   pl.BlockSpec((bb, SUBLANES, d), lambda b, t: (b, 0, 0))),
        scratch_shapes=[pltpu.VMEM((bb, tm + SUBLANES, d), F32)],
        compiler_params=_params("arbitrary", "arbitrary"),
        name="conv_mix",
    )(h, hist, g, w_in, cw, w_out)
    return out, state


def _rope_tables(pos):
    half = HEAD_DIM // 2
    inv = ROPE_THETA ** (-jnp.arange(half, dtype=F32) / half)
    ang = pos.astype(F32)[:, None] * inv[None, :]
    cos, sin = jnp.cos(ang), jnp.sin(ang)
    reps = LANES // HEAD_DIM
    return (jnp.tile(jnp.concatenate([cos, cos], axis=1), (1, reps)),
            jnp.tile(jnp.concatenate([-sin, sin], axis=1), (1, reps)))


def _rope_block(blk, cos, sin):
    half = HEAD_DIM // 2
    lane = lax.broadcasted_iota(jnp.int32, (1, LANES), 1)
    partner = jnp.where((lane % HEAD_DIM) < half, pltpu.roll(blk, LANES - half, 1), pltpu.roll(blk, half, 1))
    return blk * cos + partner * sin


def _qkv_small_kernel(h_ref, g_ref, w_ref, cos_ref, sin_ref, q_ref, k_ref, v_ref):
    bb, s, d = h_ref.shape
    kvd = k_ref.shape[-1]
    hn = _rms(h_ref[...], g_ref[...]).reshape(bb * s, d)
    z = _mm(hn, w_ref[0], True)
    cos, sin = cos_ref[...], sin_ref[...]
    q = [_rope_block(z[:, j * LANES:(j + 1) * LANES], cos, sin) for j in range(d // LANES)]
    k = [_rope_block(z[:, d + j * LANES:d + (j + 1) * LANES], cos, sin) for j in range(kvd // LANES)]
    q_ref[...] = jnp.concatenate(q, axis=1).reshape(bb, s, d)
    k_ref[...] = jnp.concatenate(k, axis=1).reshape(bb, s, kvd)
    v_ref[...] = z[:, d + kvd:d + 2 * kvd].reshape(bb, s, kvd)


def _qkv_small(h, g, w_qkv, layer, pos):
    nb, s, d = h.shape
    kvd = N_KV_HEADS * HEAD_DIM
    cos, sin = _rope_tables(pos)
    cos, sin = jnp.tile(cos, (nb, 1)), jnp.tile(sin, (nb, 1))
    full = lambda shape: pl.BlockSpec(shape, lambda i: (0,) * len(shape))
    return pl.pallas_call(
        _qkv_small_kernel,
        out_shape=(jax.ShapeDtypeStruct((nb, s, d), F32), jax.ShapeDtypeStruct((nb, s, kvd), F32),
                   jax.ShapeDtypeStruct((nb, s, kvd), F32)),
        grid=(1,),
        in_specs=[full((nb, s, d)), full((1, d)), _layer_spec((d, d + 2 * kvd), layer),
                  full((nb * s, LANES)), full((nb * s, LANES))],
        out_specs=(full((nb, s, d)), full((nb, s, kvd)), full((nb, s, kvd))),
        compiler_params=_params("arbitrary"),
        name="qkv_small",
    )(h, g, w_qkv, cos, sin)


QT_TILE = 128
KPAD = LANES


def _qkv_t_kernel(h_ref, g_ref, wqt_ref, wk_ref, wvt_ref, wv_ref, cos_ref, sin_ref, cost_ref, sint_ref,
                  qt_ref, kpad_ref, vt_ref, kst_ref, vst_ref):
    t = pl.program_id(1)
    tm = h_ref.shape[1]
    half = HEAD_DIM // 2
    lane = lax.broadcasted_iota(jnp.int32, (1, LANES), 1)
    nparts = 2 if tm % (2 * LANES) == 0 else 1
    tp = tm // nparts
    parts = [slice(i * tp, (i + 1) * tp) for i in range(nparts)]
    hb = [_rms(h_ref[0, p, :], g_ref[...]).astype(BF16) for p in parts]
    zq = [_mm_nt(wqt_ref[...], x, False) for x in hb]
    zk = [jnp.dot(x, wk_ref[...], preferred_element_type=F32) for x in hb]
    zv = [_mm_nt(wvt_ref[...], x, False) for x in hb]
    kr = []
    for i, p in enumerate(parts):
        cost, sint = cost_ref[:, p], sint_ref[:, p]
        for hd in range(zq[i].shape[0] // HEAD_DIM):
            x1 = zq[i][hd * HEAD_DIM:hd * HEAD_DIM + half]
            x2 = zq[i][hd * HEAD_DIM + half:(hd + 1) * HEAD_DIM]
            qt_ref[0, hd * HEAD_DIM:hd * HEAD_DIM + half, p] = (x1 * cost - x2 * sint).astype(BF16)
            qt_ref[0, hd * HEAD_DIM + half:(hd + 1) * HEAD_DIM, p] = (x2 * cost + x1 * sint).astype(BF16)
        vt_ref[0, :, p] = zv[i].astype(BF16)
        blks = []
        for j in range(zk[i].shape[1] // LANES):
            blk = _rope_block(zk[i][:, j * LANES:(j + 1) * LANES], cos_ref[p, :], sin_ref[p, :])
            blks.append(blk)
            for sub in range(LANES // HEAD_DIM):
                hk = j * (LANES // HEAD_DIM) + sub
                shifted = blk if sub == 0 else pltpu.roll(blk, LANES - sub * HEAD_DIM, 1)
                kpad_ref[0, p, hk * KPAD:(hk + 1) * KPAD] = jnp.where(lane < HEAD_DIM, shifted, 0.0).astype(BF16)
        kr.append(jnp.concatenate(blks, axis=1))

    @pl.when(t == pl.num_programs(1) - 1)
    def _():
        kst_ref[0] = jnp.concatenate(kr, axis=0)[tm - WINDOW:]
        vst_ref[0] = jnp.dot(hb[-1][tp - WINDOW:], wv_ref[...], preferred_element_type=F32)


def _qkv_t(h, g, w_qkv, pos, *, tm):
    nb, s, d = h.shape
    kvd = N_KV_HEADS * HEAD_DIM
    half = HEAD_DIM // 2
    cos, sin = _rope_tables(pos)
    inv = ROPE_THETA ** (-jnp.arange(half, dtype=F32) / half)
    ang = inv[:, None] * pos.astype(F32)[None, :]
    cost, sint = jnp.cos(ang), jnp.sin(ang)
    wq_t = (w_qkv[:, :d] * (HEAD_DIM ** -0.5)).T.astype(BF16)
    wk = w_qkv[:, d:d + kvd].astype(BF16)
    wv = w_qkv[:, d + kvd:].astype(BF16)
    const = lambda shape: pl.BlockSpec(shape, lambda b, t: (0,) * len(shape))
    return pl.pallas_call(
        _qkv_t_kernel,
        out_shape=(jax.ShapeDtypeStruct((nb, d, s), BF16),
                   jax.ShapeDtypeStruct((nb, s, N_KV_HEADS * KPAD), BF16),
                   jax.ShapeDtypeStruct((nb, kvd, s), BF16),
                   jax.ShapeDtypeStruct((nb, WINDOW, kvd), F32),
                   jax.ShapeDtypeStruct((nb, WINDOW, kvd), F32)),
        grid=(nb, s // tm),
        in_specs=[
            pl.BlockSpec((1, tm, d), lambda b, t: (b, t, 0)),
            const((1, d)), const((d, d)), const((d, kvd)), const((kvd, d)), const((d, kvd)),
            pl.BlockSpec((tm, LANES), lambda b, t: (t, 0)),
            pl.BlockSpec((tm, LANES), lambda b, t: (t, 0)),
            pl.BlockSpec((half, tm), lambda b, t: (0, t)),
            pl.BlockSpec((half, tm), lambda b, t: (0, t)),
        ],
        out_specs=(pl.BlockSpec((1, d, tm), lambda b, t: (b, 0, t)),
                   pl.BlockSpec((1, tm, N_KV_HEADS * KPAD), lambda b, t: (b, t, 0)),
                   pl.BlockSpec((1, kvd, tm), lambda b, t: (b, 0, t)),
                   pl.BlockSpec((1, WINDOW, kvd), lambda b, t: (b, 0, 0)),
                   pl.BlockSpec((1, WINDOW, kvd), lambda b, t: (b, 0, 0))),
        compiler_params=_params("arbitrary", "arbitrary"),
        name="qkv_t",
    )(h, g, wq_t, wk, wv.T, wv, cos, sin, cost, sint)


def _attn_t_kernel(sinks_ref, qt_ref, kc_ref, kp_ref, vc_ref, vp_ref, mk_ref, mvt_ref, h_ref, wo_ref,
                   out_ref, ot_ref, *, tq):
    t = pl.program_id(1)
    gqa = qt_ref.shape[1] // (N_KV_HEADS * HEAD_DIM)
    band = QT_TILE + WIN_CHUNKS * CHUNK
    kk = jnp.concatenate([kp_ref[0], kc_ref[0]], axis=0)
    vv = jnp.concatenate([vp_ref[0], vc_ref[0]], axis=1)
    r = lax.broadcasted_iota(jnp.int32, (band, gqa * QT_TILE), 0)
    ln = lax.broadcasted_iota(jnp.int32, (band, gqa * QT_TILE), 1)
    kchunk = r // CHUNK
    qchunk = (ln // CHUNK) % (QT_TILE // CHUNK)
    visible = (kchunk >= qchunk) & (kchunk <= qchunk + WIN_CHUNKS)
    lgroup = lax.broadcasted_iota(jnp.int32, (1, gqa * QT_TILE), 1) // QT_TILE

    def project(p0, p1):
        proj = lax.dot_general(ot_ref[:, p0:p1], wo_ref[0], (((0,), (0,)), ((), ())), preferred_element_type=F32)
        out_ref[0, p0:p1, :] = h_ref[0, p0:p1, :] + proj

    projected = done = 0
    for sub in range(tq // QT_TILE):
        c0 = sub * QT_TILE
        if sub == 0:
            mask = visible & ((t > 0) | (r >= WIN_CHUNKS * CHUNK))
        else:
            mask = visible
        krows = kk[c0:c0 + band]
        vcols = vv[:, c0:c0 + band]
        heads = range(N_KV_HEADS)
        q4 = [jnp.concatenate(
            [qt_ref[0, (hk * gqa + g) * HEAD_DIM:(hk * gqa + g + 1) * HEAD_DIM, c0:c0 + QT_TILE]
             for g in range(gqa)], axis=1) for hk in heads]
        sb = [jnp.dot(krows[:, hk * KPAD:hk * KPAD + HEAD_DIM], q4[hk], preferred_element_type=F32)
              for hk in heads]
        sm = [jnp.dot(mk_ref[:, hk * KPAD:hk * KPAD + HEAD_DIM], q4[hk], preferred_element_type=F32)
              for hk in heads]
        sb = [jnp.where(mask, x, NEG_INF) for x in sb]
        if done - projected >= QT_TILE:
            project(projected, done)
            projected = done
        sink = []
        for hk in heads:
            row = jnp.zeros((1, gqa * QT_TILE), F32)
            for g in range(gqa):
                row = jnp.where(lgroup == g, sinks_ref[hk * gqa + g], row)
            sink.append(row)
        m = [jnp.maximum(jnp.maximum(jnp.max(sb[hk], axis=0, keepdims=True),
                                     jnp.max(sm[hk], axis=0, keepdims=True)), sink[hk]) for hk in heads]
        pb = [jnp.exp(sb[hk] - m[hk]) for hk in heads]
        pm = [jnp.exp(sm[hk] - m[hk]) for hk in heads]
        denom = [jnp.sum(pb[hk], axis=0, keepdims=True) + jnp.sum(pm[hk], axis=0, keepdims=True)
                 + jnp.exp(sink[hk] - m[hk]) for hk in heads]
        o = [(jnp.dot(vcols[hk * HEAD_DIM:(hk + 1) * HEAD_DIM], pb[hk].astype(BF16), preferred_element_type=F32)
              + jnp.dot(mvt_ref[hk * HEAD_DIM:(hk + 1) * HEAD_DIM, :], pm[hk].astype(BF16),
                        preferred_element_type=F32)) / denom[hk] for hk in heads]
        for hk in heads:
            for g in range(gqa):
                ot_ref[(hk * gqa + g) * HEAD_DIM:(hk * gqa + g + 1) * HEAD_DIM, c0:c0 + QT_TILE] = (
                    o[hk][:, g * QT_TILE:(g + 1) * QT_TILE].astype(BF16))
        done = (sub + 1) * QT_TILE
    project(projected, done)


def _attn_t(qt, kpad, vt, mkpad, mvt, sinks, h, w_o, layer, *, tq):
    nb, s, d = h.shape
    kvd = vt.shape[1]
    prev = WIN_CHUNKS * CHUNK
    ratio = tq // prev
    const = lambda shape: pl.BlockSpec(shape, lambda b, t: (0,) * len(shape))
    return pl.pallas_call(
        functools.partial(_attn_t_kernel, tq=tq),
        out_shape=jax.ShapeDtypeStruct((nb, s, d), F32),
        grid=(nb, s // tq),
        in_specs=[
            pl.BlockSpec(memory_space=pltpu.SMEM),
            pl.BlockSpec((1, d, tq), lambda b, t: (b, 0, t)),
            pl.BlockSpec((1, tq, N_KV_HEADS * KPAD), lambda b, t: (b, t, 0)),
            pl.BlockSpec((1, prev, N_KV_HEADS * KPAD), lambda b, t: (b, jnp.maximum(t * ratio - 1, 0), 0)),
            pl.BlockSpec((1, kvd, tq), lambda b, t: (b, 0, t)),
            pl.BlockSpec((1, kvd, prev), lambda b, t: (b, 0, jnp.maximum(t * ratio - 1, 0))),
            const((N_META, N_KV_HEADS * KPAD)), const((kvd, N_META)),
            pl.BlockSpec((1, tq, d), lambda b, t: (b, t, 0)),
            _layer_spec((d, d), layer),
        ],
        out_specs=pl.BlockSpec((1, tq, d), lambda b, t: (b, t, 0)),
        scratch_shapes=[pltpu.VMEM((d, tq), BF16)],
        compiler_params=_params("arbitrary", "arbitrary"),
        name="attn_t",
    )(sinks, qt, kpad, kpad, vt, vt, mkpad, mvt, h, w_o)


def _attn_small_kernel(sinks_ref, q_ref, k_ref, v_ref, h_ref, wo_ref, out_ref, o_ref):
    bb, s, d = q_ref.shape
    gqa = d // (N_KV_HEADS * HEAD_DIM)
    for b in range(bb):
        for hk in range(N_KV_HEADS):
            hs = slice(hk * HEAD_DIM, (hk + 1) * HEAD_DIM)
            qh = jnp.concatenate(
                [q_ref[b, :, (hk * gqa + g) * HEAD_DIM:(hk * gqa + g + 1) * HEAD_DIM] for g in range(gqa)],
                axis=0)
            sink = jnp.concatenate([jnp.full((s, 1), sinks_ref[hk * gqa + g], F32) for g in range(gqa)], axis=0)
            sc = _mm_nt(qh, k_ref[b, :, hs], True) * (HEAD_DIM ** -0.5)
            m = jnp.maximum(jnp.max(sc, axis=-1, keepdims=True), sink)
            p = jnp.exp(sc - m)
            denom = jnp.sum(p, axis=-1, keepdims=True) + jnp.exp(sink - m)
            o = _mm(p, v_ref[b, :, hs], True) / denom
            for g in range(gqa):
                o_ref[b * s:(b + 1) * s, (hk * gqa + g) * HEAD_DIM:(hk * gqa + g + 1) * HEAD_DIM] = (
                    o[g * s:(g + 1) * s])
    out_ref[...] = h_ref[...] + _mm(o_ref[...], wo_ref[0], True).reshape(bb, s, d)


def _attn_small(q, keys, vals, sinks, h, w_o, layer):
    nb, s, d = h.shape
    kn, kvd = keys.shape[1:]
    full = lambda shape: pl.BlockSpec(shape, lambda i: (0,) * len(shape))
    return pl.pallas_call(
        _attn_small_kernel,
        out_shape=jax.ShapeDtypeStruct((nb, s, d), F32),
        grid=(1,),
        in_specs=[pl.BlockSpec(memory_space=pltpu.SMEM), full((nb, s, d)), full((nb, kn, kvd)),
                  full((nb, kn, kvd)), full((nb, s, d)), _layer_spec((d, d), layer)],
        out_specs=full((nb, s, d)),
        scratch_shapes=[pltpu.VMEM((nb * s, d), F32)],
        compiler_params=_params("arbitrary"),
        name="attn_small",
    )(sinks, q, keys, vals, h, w_o)


ROUTE_ROWS = 32
MOE_CHUNK = 128
MOE_CALL_ROWS = (128, 144, 160)
SEG_ALIGN = 16


def _route_t(lt):
    n = lt.shape[1]
    row8 = lax.broadcasted_iota(jnp.int32, (SUBLANES, n), 0)
    lg = jnp.where(row8 < N_GROUPS, lt[0:SUBLANES], NEG_INF)
    gmax = jnp.max(lg, axis=0, keepdims=True)
    g_idx = jnp.min(jnp.where(lg == gmax, row8, SUBLANES), axis=0, keepdims=True)
    g_w = 1.0 / jnp.sum(jnp.exp(lg - gmax), axis=0, keepdims=True)
    le = lt[SUBLANES:SUBLANES + N_EXPERTS]
    row16 = lax.broadcasted_iota(jnp.int32, (N_EXPERTS, n), 0)
    in_group = (row16 // EXPERTS_PER_GROUP) == g_idx
    l1 = jnp.where(in_group, le, NEG_INF)
    m1 = jnp.max(l1, axis=0, keepdims=True)
    i1 = jnp.min(jnp.where(in_group & (l1 == m1), row16, N_EXPERTS), axis=0, keepdims=True)
    rest = in_group & (row16 != i1)
    l2 = jnp.where(rest, le, NEG_INF)
    m2 = jnp.max(l2, axis=0, keepdims=True)
    i2 = jnp.min(jnp.where(rest & (l2 == m2), row16, N_EXPERTS), axis=0, keepdims=True)
    e2 = jnp.exp(m2 - m1)
    p1 = 1.0 / (1.0 + e2)
    comb = g_w * (jnp.where(row16 == i1, p1, 0.0) + jnp.where(row16 == i2, e2 * p1, 0.0))
    c8 = comb[0:SUBLANES] + comb[SUBLANES:2 * SUBLANES]
    return g_idx, comb, c8 + pltpu.roll(c8, EXPERTS_PER_GROUP, 0)


def _route(logits):
    col = lax.broadcasted_iota(jnp.int32, logits.shape, 1)
    lg = jnp.where(col < N_GROUPS, logits, NEG_INF)
    gmax = jnp.max(lg, axis=-1, keepdims=True)
    g_idx = jnp.min(jnp.where(lg == gmax, col, LANES), axis=-1, keepdims=True)
    g_w = 1.0 / jnp.sum(jnp.exp(lg - gmax), axis=-1, keepdims=True)
    ecol = col - N_GROUPS
    in_group = (ecol >= 0) & (ecol < N_EXPERTS) & ((ecol // EXPERTS_PER_GROUP) == g_idx)
    l1 = jnp.where(in_group, logits, NEG_INF)
    m1 = jnp.max(l1, axis=-1, keepdims=True)
    i1 = jnp.min(jnp.where(in_group & (l1 == m1), col, LANES), axis=-1, keepdims=True)
    rest = in_group & (col != i1)
    l2 = jnp.where(rest, logits, NEG_INF)
    m2 = jnp.max(l2, axis=-1, keepdims=True)
    i2 = jnp.min(jnp.where(rest & (l2 == m2), col, LANES), axis=-1, keepdims=True)
    e2 = jnp.exp(m2 - m1)
    p1 = 1.0 / (1.0 + e2)
    return g_w * (jnp.where(col == i1, p1, 0.0) + jnp.where(col == i2, e2 * p1, 0.0))


def _router_logits_t(wr_ref, br_ref, hn_hi, hn_lo):
    a = _dot(wr_ref[0], hn_hi, (((1,), (1,)), ((), ())))
    b = _dot(wr_ref[0, 0:ROUTE_ROWS], hn_lo, (((1,), (1,)), ((), ())))
    return a[0:ROUTE_ROWS] + a[ROUTE_ROWS:2 * ROUTE_ROWS] + b + br_ref[0]


SMALL_EXPERTS_PER_STEP = 2


def _moe_small_kernel(h_ref, g_ref, wr_ref, br_ref, wg_ref, wu_ref, wd_ref, gf_ref,
                      out_ref, wgb_ref, wub_ref, wdb_ref, hn_ref, comb_ref, acc_ref, *, precise, final_norm):
    step = pl.program_id(0)
    dn = (((1,), (0,)), ((), ()))

    @pl.when(step == 0)
    def _():
        hn = _rms(h_ref[...], g_ref[...])
        hn_ref[0], hn_ref[1] = _split_bf16(hn, 2)
        comb_ref[...] = _route(_mm(hn, wr_ref[0], True) + br_ref[0])
        acc_ref[...] = jnp.zeros_like(acc_ref)

    def times(a_hi, a_lo, w):
        if not precise:
            w_hi = w.astype(BF16)
            return _dot(a_hi, w_hi, dn), w_hi
        w_hi, w_lo = _split_bf16(w, 2)
        return _dot(a_hi, w_hi, dn) + (_dot(a_hi, w_lo, dn) + _dot(a_lo, w_hi, dn)), w_hi

    js = range(SMALL_EXPERTS_PER_STEP)
    col = lax.broadcasted_iota(jnp.int32, comb_ref.shape, 1)
    c = [jnp.sum(jnp.where(col == step * SMALL_EXPERTS_PER_STEP + j + N_GROUPS, comb_ref[...], 0.0),
                 axis=-1, keepdims=True) for j in js]
    gates = [times(hn_ref[0], hn_ref[1], wg_ref[0, j]) for j in js]
    ups = [times(hn_ref[0], hn_ref[1], wu_ref[0, j]) for j in js]
    acts = [_split_bf16(gates[j][0] * jax.nn.sigmoid(gates[j][0]) * ups[j][0] * c[j], 2) for j in js]
    downs = [times(acts[j][0], acts[j][1], wd_ref[0, j]) for j in js]
    for j in js:
        wgb_ref[j], wub_ref[j], wdb_ref[j] = gates[j][1], ups[j][1], downs[j][1]
    acc_ref[...] += sum(downs[j][0] for j in js)

    @pl.when(step == pl.num_programs(0) - 1)
    def _():
        y = h_ref[...] + acc_ref[...]
        if final_norm:
            y = _rms(y, gf_ref[...])
        out_ref[...] = y


def _moe_small(h, g, wr, br, wg, wu, wd, layer, gf, *, precise, final_norm):
    n, d = h.shape
    _, ne, _, de = wg.shape
    k = SMALL_EXPERTS_PER_STEP
    const = lambda shape: pl.BlockSpec(shape, lambda e: (0,) * len(shape))
    return pl.pallas_call(
        functools.partial(_moe_small_kernel, precise=precise, final_norm=final_norm),
        out_shape=(jax.ShapeDtypeStruct((n, d), F32), jax.ShapeDtypeStruct((ne, d, de), BF16),
                   jax.ShapeDtypeStruct((ne, d, de), BF16), jax.ShapeDtypeStruct((ne, de, d), BF16)),
        grid=(ne // k,),
        in_specs=[
            const((n, d)), const((1, d)),
            _layer_spec((d, LANES), layer), _layer_spec((1, LANES), layer),
            pl.BlockSpec((1, k, d, de), lambda e: (layer, e, 0, 0)),
            pl.BlockSpec((1, k, d, de), lambda e: (layer, e, 0, 0)),
            pl.BlockSpec((1, k, de, d), lambda e: (layer, e, 0, 0)),
            const((1, d)),
        ],
        out_specs=(const((n, d)), pl.BlockSpec((k, d, de), lambda e: (e, 0, 0)),
                   pl.BlockSpec((k, d, de), lambda e: (e, 0, 0)), pl.BlockSpec((k, de, d), lambda e: (e, 0, 0))),
        scratch_shapes=[pltpu.VMEM((2, n, d), BF16), pltpu.VMEM((n, LANES), F32), pltpu.VMEM((n, d), F32)],
        compiler_params=_params("arbitrary"),
        name="moe_small",
    )(h, g, wr, br, wg, wu, wd, gf)


def _moe_sparse_kernel(h_ref, g_ref, wr_ref, br_ref, tri_ref, wg_ref, wu_ref, wd_ref, gf_ref, out_ref,
                       xs_ref, ys_ref, p_ref, cs_ref, *, tm, nsub, final_norm):
    rows = p_ref.shape[1]

    @pl.when(pl.program_id(0) == 0)
    def _():
        xs_ref[:, rows:, :] = jnp.zeros((nsub, xs_ref.shape[1] - rows, xs_ref.shape[2]), BF16)
        cs_ref[:, rows:, :] = jnp.zeros((nsub, cs_ref.shape[1] - rows, cs_ref.shape[2]), F32)

    def prep():
        tiles = range(nsub)
        row8 = lax.broadcasted_iota(jnp.int32, (SUBLANES, tm), 0)
        riota = lax.broadcasted_iota(jnp.int32, (rows, tm), 0)
        hn = [_rms(h_ref[t * tm:(t + 1) * tm, :], g_ref[...]) for t in tiles]
        hn_split = [_split_bf16(x, 2) for x in hn]
        logits = [_router_logits_t(wr_ref, br_ref, hi, lo) for hi, lo in hn_split]
        routed = [_route_t(lt) for lt in logits]
        onehot = [row8 == g_idx for g_idx, _, _ in routed]
        incl = [jnp.dot(jnp.where(oh, 1.0, 0.0).astype(BF16), tri_ref[...], preferred_element_type=F32)
                for oh in onehot]
        counts = [x[:, tm - 1:tm].astype(jnp.int32) for x in incl]
        plans = []
        for t in tiles:
            n = [counts[t][g, 0] for g in range(N_GROUPS)]
            starts = [jnp.int32(0)]
            for g in range(N_GROUPS - 1):
                starts.append(starts[-1] + (n[g] + SEG_ALIGN - 1) // SEG_ALIGN * SEG_ALIGN)
            plans.append((n, starts))
        for t in tiles:
            g_idx = routed[t][0]
            rank = jnp.sum(jnp.where(onehot[t], incl[t], 0.0), axis=0, keepdims=True).astype(jnp.int32) - 1
            start_tok = jnp.zeros_like(g_idx)
            for g in range(1, N_GROUPS):
                start_tok = jnp.where(g_idx == g, plans[t][1][g], start_tok)
            p_ref[t] = jnp.where(riota == start_tok + rank, 1.0, 0.0).astype(BF16)
        for t in tiles:
            xs_ref[t, 0:rows, :] = jnp.dot(p_ref[t], hn_split[t][0], preferred_element_type=F32).astype(BF16)
            comb_parts = jnp.concatenate(_split_bf16(routed[t][2], 3), axis=0)
            cs = _dot(p_ref[t], comb_parts, (((1,), (1,)), ((), ())))
            cs_ref[t, 0:rows, :] = (cs[:, 0:SUBLANES] + cs[:, SUBLANES:2 * SUBLANES]
                                    + cs[:, 2 * SUBLANES:3 * SUBLANES])
            ys_ref[t, 0:rows, :] = jnp.zeros((rows, ys_ref.shape[2]), BF16)
        return plans

    def experts(t, g, r0, m):
        x = xs_ref[t, pl.ds(r0, m), :]
        cc = cs_ref[t, pl.ds(r0, m), :]
        acts = []
        for j in range(EXPERTS_PER_GROUP):
            e = g * EXPERTS_PER_GROUP + j
            gate = jnp.dot(x, wg_ref[e], preferred_element_type=F32)
            up = jnp.dot(x, wu_ref[e], preferred_element_type=F32)
            acts.append((gate * jax.nn.sigmoid(gate) * up * cc[:, j:j + 1]).astype(BF16))
        y = jnp.dot(jnp.concatenate(acts, axis=1), wd_ref[g], preferred_element_type=F32)
        ys_ref[t, pl.ds(r0, m), :] = y.astype(BF16)

    plans = prep()

    for t, (n, starts) in enumerate(plans):
        for g in range(N_GROUPS):
            seg = pl.multiple_of(starts[g], SEG_ALIGN)
            lo = 0
            for m in MOE_CALL_ROWS:
                @pl.when((n[g] > lo) & (n[g] <= m))
                def _(t=t, g=g, seg=seg, m=m):
                    experts(t, g, seg, m)

                lo = m

            @pl.when(n[g] > MOE_CALL_ROWS[-1])
            def _(t=t, g=g, n=n, starts=starts):
                def chunk(c, carry):
                    experts(t, g, pl.multiple_of(starts[g] + c * MOE_CHUNK, SEG_ALIGN), MOE_CHUNK)
                    return carry

                lax.fori_loop(0, (n[g] + MOE_CHUNK - 1) // MOE_CHUNK, chunk, 0)

    for t in range(nsub):
        back = lax.dot_general(p_ref[t], ys_ref[t, 0:rows, :], (((0,), (0,)), ((), ())),
                               preferred_element_type=F32)
        y = h_ref[t * tm:(t + 1) * tm, :] + back
        if final_norm:
            y = _rms(y, gf_ref[...])
        out_ref[t * tm:(t + 1) * tm, :] = y


def _moe_sparse(h, g, wr2, brt, tri, wg, wu, wd4, layer, gf, *, tm, nsub, final_norm):
    n, d = h.shape
    ne, _, de = wg.shape
    rows = tm + N_GROUPS * SEG_ALIGN
    over = rows + MOE_CHUNK
    resident = lambda shape: pl.BlockSpec(shape, lambda i: (0,) * len(shape), pipeline_mode=pl.Buffered(1))
    const = lambda shape: pl.BlockSpec(shape, lambda i: (0,) * len(shape))
    return pl.pallas_call(
        functools.partial(_moe_sparse_kernel, tm=tm, nsub=nsub, final_norm=final_norm),
        out_shape=jax.ShapeDtypeStruct((n, d), F32),
        grid=(n // (tm * nsub),),
        in_specs=[
            pl.BlockSpec((tm * nsub, d), lambda i: (i, 0)), const((1, d)),
            _layer_spec((2 * ROUTE_ROWS, d), layer), _layer_spec((ROUTE_ROWS, 1), layer),
            const((tm, tm)),
            resident((ne, d, de)), resident((ne, d, de)), resident((N_GROUPS, EXPERTS_PER_GROUP * de, d)),
            const((1, d)),
        ],
        out_specs=pl.BlockSpec((tm * nsub, d), lambda i: (i, 0)),
        scratch_shapes=[pltpu.VMEM((nsub, over, d), BF16), pltpu.VMEM((nsub, over, d), BF16),
                        pltpu.VMEM((nsub, rows, tm), BF16), pltpu.VMEM((nsub, over, SUBLANES), F32)],
        compiler_params=pltpu.CompilerParams(dimension_semantics=("arbitrary",),
                                             vmem_limit_bytes=MOE_VMEM_LIMIT),
        name="moe_sparse",
    )(h, g, wr2, brt, tri, wg, wu, wd4, gf)


def kernel(x_prompt, x_sample, state_pool, cache_swa_kv, cache_meta_kv, state_conv, meta_tokens, norm_mix, norm_ffn, norm_final, w_pool, pool_scale, w_qkv, w_o, attn_sinks, w_conv_in, conv_w, w_conv_out, w_group, b_group, w_expert_router, b_expert_router, w_gate, w_up, w_down):
    nb, seq, d = x_prompt.shape
    db, dseq, _ = x_sample.shape
    depth = norm_mix.shape[0]
    kvd = N_KV_HEADS * HEAD_DIM
    tm_main = min(512, seq)
    tq_main = min(512, seq)
    tm_pool = 1024 if seq % 1024 == 0 else tm_main
    moe_sub = 2 if (nb * seq) % (2 * tm_main) == 0 else 1
    halo = POOL_STATE + 1

    row = lambda a: a.reshape(1, -1).astype(F32)
    rpad = lambda a, k: jnp.pad(a, ((0, 0), (0, k)) + ((0, 0),) * (a.ndim - 2))
    wrt = jnp.concatenate([rpad(jnp.swapaxes(w_group, 1, 2), SUBLANES - N_GROUPS),
                           rpad(jnp.swapaxes(w_expert_router, 1, 2), ROUTE_ROWS - SUBLANES - N_EXPERTS)], axis=1)
    wr2 = jnp.concatenate(_split_bf16(wrt, 2), axis=1)
    brt = jnp.concatenate([rpad(b_group, SUBLANES - N_GROUPS),
                           rpad(b_expert_router, ROUTE_ROWS - SUBLANES - N_EXPERTS)], axis=1)[..., None].astype(F32)
    wr = jnp.pad(jnp.concatenate([w_group, w_expert_router], axis=-1).astype(F32),
                 ((0, 0), (0, 0), (0, LANES - N_GROUPS - N_EXPERTS)))
    br = jnp.pad(jnp.concatenate([b_group, b_expert_router], axis=-1).astype(F32),
                 ((0, 0), (0, LANES - N_GROUPS - N_EXPERTS)))[:, None, :]
    tri = jnp.triu(jnp.ones((tm_main, tm_main), BF16))
    bf = lambda a: a.astype(BF16)
    w_pool_b, w_o_b = bf(w_pool), bf(w_o)
    w_conv_in_b, w_conv_out_b = bf(w_conv_in), bf(w_conv_out)

    hm = meta_tokens.astype(F32)[None]
    hp = x_prompt
    hs = x_sample
    pool_p, swa_p, meta_p, conv_p, pool_s, swa_s, conv_s = [], [], [], [], [], [], []
    for i in range(depth):
        j = i // N_MIXERS
        g = row(norm_mix[i])
        if i % N_MIXERS == 0:
            sc = row(pool_scale[j])
            hm, st_m = _pool_mix(hm, jnp.zeros((1, halo, d), F32), g, w_pool, j, sc, bb=1, tm=N_META,
                                 has_history=False, precise=True)
            hp, st_p = _pool_mix(hp, st_m, g, w_pool_b, j, sc, bb=1, tm=tm_pool, has_history=True,
                                 precise=False)
            hist_s = jnp.pad(state_pool[j].astype(F32), ((0, 0), (1, 0), (0, 0)))
            hs, st_s = _pool_mix(hs, hist_s, g, w_pool, j, sc, bb=db, tm=dseq, has_history=True,
                                 precise=True)
            pool_p.append(st_p[:, 1:])
            pool_s.append(st_s[:, 1:])
        elif i % N_MIXERS == 1:
            sinks = attn_sinks[j].astype(F32)
            qm, km, vm = _qkv_small(hm, g, w_qkv, j, jnp.arange(N_META))
            qt, kpad, vt, kst, vst = _qkv_t(hp, g, w_qkv[j], N_META + jnp.arange(seq), tm=tm_main)
            qs, ks, vs = _qkv_small(hs, g, w_qkv, j, PAST_LEN + N_META + jnp.arange(dseq))
            hm = _attn_small(qm, km, vm, sinks, hm, w_o, j)
            mkpad = jnp.pad(km[0].reshape(N_META, N_KV_HEADS, HEAD_DIM),
                            ((0, 0), (0, 0), (0, KPAD - HEAD_DIM))).reshape(N_META, -1).astype(BF16)
            hp = _attn_t(qt, kpad, vt, mkpad, vm[0].T.astype(BF16), sinks, hp, w_o_b, j, tq=tq_main)
            flat = lambda a: a.reshape(a.shape[0], a.shape[1], kvd)
            keys = jnp.concatenate([flat(cache_meta_kv[j][:, :, 0]), flat(cache_swa_kv[j][:, :, 0]), ks], axis=1)
            vals = jnp.concatenate([flat(cache_meta_kv[j][:, :, 1]), flat(cache_swa_kv[j][:, :, 1]), vs], axis=1)
            hs = _attn_small(qs, keys, vals, sinks, hs, w_o, j)
            heads = lambda a: a.reshape(a.shape[0], a.shape[1], N_KV_HEADS, HEAD_DIM)
            swa_p.append(jnp.stack([heads(kst), heads(vst)], axis=2))
            meta_kv = jnp.stack([heads(km), heads(vm)], axis=2)
            meta_p.append(jnp.broadcast_to(meta_kv, (nb,) + meta_kv.shape[1:]))
            swa_s.append(jnp.stack([heads(ks), heads(vs)], axis=2))
        else:
            cw = conv_w[j].astype(F32)
            hm, st_m = _conv_mix(hm, jnp.zeros((1, SUBLANES, d), F32), g, w_conv_in, j, cw, w_conv_out,
                                 bb=1, tm=N_META, precise=True)
            hp, st_p = _conv_mix(hp, st_m, g, w_conv_in_b, j, cw, w_conv_out_b, bb=1, tm=tm_main,
                                 precise=False)
            hist_s = jnp.pad(state_conv[j].astype(F32), ((0, 0), (SUBLANES - (CONV_WIDTH - 1), 0), (0, 0)))
            hs, st_s = _conv_mix(hs, hist_s, g, w_conv_in, j, cw, w_conv_out, bb=db, tm=dseq, precise=True)
            conv_p.append(st_p[:, SUBLANES - (CONV_WIDTH - 1):])
            conv_s.append(st_s[:, SUBLANES - (CONV_WIDTH - 1):])

        final = i == depth - 1
        gf = row(norm_final)
        gn = row(norm_ffn[i])
        small = jnp.concatenate([hm.reshape(-1, d), hs.reshape(-1, d)], axis=0)
        precise = any(k % N_MIXERS != 0 for k in range(i + 1, depth))
        small, wg_b, wu_b, wd_b = _moe_small(small, gn, wr, br, w_gate, w_up, w_down, i, gf,
                                             precise=precise, final_norm=final)
        hm = small[:N_META].reshape(1, N_META, d)
        hs = small[N_META:].reshape(db, dseq, d)
        hp = _moe_sparse(hp.reshape(-1, d), gn, wr2, brt, tri, wg_b, wu_b, wd_b.reshape(N_GROUPS, -1, d), i, gf,
                         tm=tm_main, nsub=moe_sub, final_norm=final).reshape(nb, seq, d)

    return (hp, hs, jnp.stack(pool_p), jnp.stack(swa_p), jnp.stack(meta_p), jnp.stack(conv_p),
            jnp.stack(pool_s), jnp.stack(swa_s), jnp.stack(conv_s))
```

```python
import functools

import jax
import jax.numpy as jnp
from jax import lax
from jax.experimental import pallas as pl
from jax.experimental.pallas import tpu as pltpu

F32 = jnp.float32
BF16 = jnp.bfloat16

CHUNK = 64
N_META = 16
N_MIXERS = 3
POOL_WINDOWS = (2, 4, 8, 16)
POOL_STATE = max(POOL_WINDOWS) - 1
HEAD_DIM = 64
N_KV_HEADS = 4
WINDOW = 128
WIN_CHUNKS = WINDOW // CHUNK
ROPE_THETA = 10000.0
CONV_WIDTH = 3
N_GROUPS = 4
EXPERTS_PER_GROUP = 4
N_EXPERTS = N_GROUPS * EXPERTS_PER_GROUP
PAST_LEN = 2048
EPS = 1e-6

LANES = 128
SUBLANES = 8
VMEM_LIMIT = 48 * 1024 * 1024
MOE_VMEM_LIMIT = 60 * 1024 * 1024
NEG_INF = float("-inf")


def _params(*sem):
    return pltpu.CompilerParams(dimension_semantics=sem, vmem_limit_bytes=VMEM_LIMIT)


def _split_bf16(x, parts):
    out = []
    x = x.astype(F32)
    for _ in range(parts):
        hi = x.astype(BF16)
        out.append(hi)
        x = x - hi.astype(F32)
    return out


def _dot(a, b, dn):
    return lax.dot_general(a, b, dn, preferred_element_type=F32)


def _mm_dn(a, b, dn, precise):
    if not precise:
        return _dot(a.astype(BF16), b.astype(BF16), dn)
    a_hi, a_lo = _split_bf16(a, 2)
    b_hi, b_lo = _split_bf16(b, 2)
    return _dot(a_hi, b_hi, dn) + (_dot(a_hi, b_lo, dn) + _dot(a_lo, b_hi, dn))


def _mm(a, b, precise):
    return _mm_dn(a, b, (((1,), (0,)), ((), ())), precise)


def _mm_nt(a, b, precise):
    return _mm_dn(a, b, (((1,), (1,)), ((), ())), precise)


def _rms(x, g):
    ms = jnp.mean(x * x, axis=-1, keepdims=True)
    return x * lax.rsqrt(ms + EPS) * g


def _layer_spec(shape, layer, single=False):
    nd = len(shape)
    mode = dict(pipeline_mode=pl.Buffered(1)) if single else {}
    return pl.BlockSpec((1,) + tuple(shape), lambda *_: (layer,) + (0,) * nd, **mode)


POOL_HALO = 16
POOL_LEAD = 16


def _pool_kernel(h_ref, hist_ref, g_ref, w_ref, scale_ref, out_ref, state_ref, buf_ref, sa_ref, sb_ref, *,
                 tm, has_history, precise):
    t = pl.program_id(1)
    base = POOL_LEAD + POOL_HALO
    bb, _, d = h_ref.shape
    pg = d // len(POOL_WINDOWS)
    end = base + tm

    @pl.when(t == 0)
    def _():
        buf_ref[:, 0:POOL_LEAD, :] = jnp.zeros((bb, POOL_LEAD, d), F32)
        buf_ref[:, POOL_LEAD:base, :] = jnp.broadcast_to(hist_ref[...], (bb, POOL_HALO, d))

    h = h_ref[...]
    hn = _rms(h, g_ref[...])
    buf_ref[:, base:end, :] = hn
    if not has_history:
        pos = t * tm + lax.broadcasted_iota(jnp.int32, (1, tm, 1), 1)
    ys = []
    for gi, w in enumerate(POOL_WINDOWS):
        c0, c1 = gi * pg, (gi + 1) * pg
        src, dst, shift, lo = buf_ref, sa_ref, 1, SUBLANES
        while shift < w:
            last = 2 * shift == w
            lo_k = base if last else lo
            s = src[:, lo_k:end, c0:c1] + src[:, lo_k - shift:end - shift, c0:c1]
            if last:
                win = s
            else:
                dst[:, lo_k:end, c0:c1] = s
                src, dst = dst, (sb_ref if dst is sa_ref else sa_ref)
            shift, lo = 2 * shift, lo + SUBLANES
        if has_history:
            mean = win * (1.0 / w)
        else:
            mean = win / jnp.minimum(pos + 1, w).astype(F32)
        ys.append(_mm((mean - hn[:, :, c0:c1]).reshape(bb * tm, pg), w_ref[0, gi], precise))
    y = jnp.concatenate(ys, axis=1).reshape(bb, tm, d) * scale_ref[...]
    out_ref[...] = h + y
    tail = buf_ref[:, end - POOL_HALO:end, :]
    buf_ref[:, POOL_LEAD:base, :] = tail

    @pl.when(t == pl.num_programs(1) - 1)
    def _():
        state_ref[...] = tail


def _pool_mix(h, hist, g, w, layer, scale, *, bb, tm, has_history, precise):
    nb, s, d = h.shape
    halo = POOL_HALO
    pg = d // len(POOL_WINDOWS)
    hist_map = (lambda b, t: (b, 0, 0)) if hist.shape[0] == nb and nb > 1 else (lambda b, t: (0, 0, 0))
    hb = bb if hist.shape[0] == nb and nb > 1 else 1
    out, state = pl.pallas_call(
        functools.partial(_pool_kernel, tm=tm, has_history=has_history, precise=precise),
        out_shape=(jax.ShapeDtypeStruct((nb, s, d), F32), jax.ShapeDtypeStruct((nb, halo, d), F32)),
        grid=(nb // bb, s // tm),
        in_specs=[
            pl.BlockSpec((bb, tm, d), lambda b, t: (b, t, 0)),
            pl.BlockSpec((hb, halo, d), hist_map),
            pl.BlockSpec((1, d), lambda b, t: (0, 0)),
            _layer_spec((len(POOL_WINDOWS), pg, pg), layer),
            pl.BlockSpec((1, d), lambda b, t: (0, 0)),
        ],
        out_specs=(pl.BlockSpec((bb, tm, d), lambda b, t: (b, t, 0)),
                   pl.BlockSpec((bb, halo, d), lambda b, t: (b, 0, 0))),
        scratch_shapes=[pltpu.VMEM((bb, POOL_LEAD + halo + tm, d), F32)] * 3,
        compiler_params=_params("arbitrary", "arbitrary"),
        name="pool_mix",
    )(h, hist, g, w, scale)
    return out, state


def _conv_kernel(h_ref, hist_ref, g_ref, win_ref, cw_ref, wout_ref, out_ref, state_ref, buf_ref, *,
                 tm, precise):
    t = pl.program_id(1)
    bb, _, d = h_ref.shape

    @pl.when(t == 0)
    def _():
        buf_ref[:, 0:SUBLANES, :] = jnp.broadcast_to(hist_ref[...], (bb, SUBLANES, d))

    h = h_ref[...]
    hn = _rms(h, g_ref[...])
    z = _mm(hn.reshape(bb * tm, d), win_ref[0], precise)
    gate_b = z[:, 0:d]
    buf_ref[:, SUBLANES:SUBLANES + tm, :] = (z[:, d:2 * d] * z[:, 2 * d:3 * d]).reshape(bb, tm, d)
    first = SUBLANES - (CONV_WIDTH - 1)
    acc = buf_ref[:, first:first + tm, :] * cw_ref[0:1, :]
    for k in range(1, CONV_WIDTH):
        acc = acc + buf_ref[:, first + k:first + k + tm, :] * cw_ref[k:k + 1, :]
    y = _mm(gate_b * acc.reshape(bb * tm, d), wout_ref[0], precise)
    out_ref[...] = h + y.reshape(bb, tm, d)
    tail = buf_ref[:, tm:tm + SUBLANES, :]
    buf_ref[:, 0:SUBLANES, :] = tail

    @pl.when(t == pl.num_programs(1) - 1)
    def _():
        state_ref[...] = tail


def _conv_mix(h, hist, g, w_in, layer, cw, w_out, *, bb, tm, precise):
    nb, s, d = h.shape
    per_batch = hist.shape[0] == nb and nb > 1
    hist_map = (lambda b, t: (b, 0, 0)) if per_batch else (lambda b, t: (0, 0, 0))
    out, state = pl.pallas_call(
        functools.partial(_conv_kernel, tm=tm, precise=precise),
        out_shape=(jax.ShapeDtypeStruct((nb, s, d), F32), jax.ShapeDtypeStruct((nb, SUBLANES, d), F32)),
        grid=(nb // bb, s // tm),
        in_specs=[
            pl.BlockSpec((bb, tm, d), lambda b, t: (b, t, 0)),
            pl.BlockSpec((bb if per_batch else 1, SUBLANES, d), hist_map),
            pl.BlockSpec((1, d), lambda b, t: (0, 0)),
            _layer_spec((d, 3 * d), layer, single=True),
            pl.BlockSpec((CONV_WIDTH, d), lambda b, t: (0, 0)),
            _layer_spec((d, d), layer, single=True),
        ],
        out_specs=(pl.BlockSpec((bb, tm, d), lambda b, t: (b, t, 0)),
                   pl.BlockSpec((bb, SUBLANES, d), lambda b, t: (b, 0, 0))),
        scratch_shapes=[pltpu.VMEM((bb, tm + SUBLANES, d), F32)],
        compiler_params=_params("arbitrary", "arbitrary"),
        name="conv_mix",
    )(h, hist, g, w_in, cw, w_out)
    return out, state


def _rope_tables(pos):
    half = HEAD_DIM // 2
    inv = ROPE_THETA ** (-jnp.arange(half, dtype=F32) / half)
    ang = pos.astype(F32)[:, None] * inv[None, :]
    cos, sin = jnp.cos(ang), jnp.sin(ang)
    reps = LANES // HEAD_DIM
    return (jnp.tile(jnp.concatenate([cos, cos], axis=1), (1, reps)),
            jnp.tile(jnp.concatenate([-sin, sin], axis=1), (1, reps)))


def _rope_block(blk, cos, sin):
    half = HEAD_DIM // 2
    lane = lax.broadcasted_iota(jnp.int32, (1, LANES), 1)
    partner = jnp.where((lane % HEAD_DIM) < half, pltpu.roll(blk, LANES - half, 1), pltpu.roll(blk, half, 1))
    return blk * cos + partner * sin


def _qkv_small_kernel(h_ref, g_ref, w_ref, cos_ref, sin_ref, q_ref, k_ref, v_ref):
    bb, s, d = h_ref.shape
    kvd = k_ref.shape[-1]
    hn = _rms(h_ref[...], g_ref[...]).reshape(bb * s, d)
    z = _mm(hn, w_ref[0], True)
    cos, sin = cos_ref[...], sin_ref[...]
    q = [_rope_block(z[:, j * LANES:(j + 1) * LANES], cos, sin) for j in range(d // LANES)]
    k = [_rope_block(z[:, d + j * LANES:d + (j + 1) * LANES], cos, sin) for j in range(kvd // LANES)]
    q_ref[...] = jnp.concatenate(q, axis=1).reshape(bb, s, d)
    k_ref[...] = jnp.concatenate(k, axis=1).reshape(bb, s, kvd)
    v_ref[...] = z[:, d + kvd:d + 2 * kvd].reshape(bb, s, kvd)


def _qkv_small(h, g, w_qkv, layer, pos):
    nb, s, d = h.shape
    kvd = N_KV_HEADS * HEAD_DIM
    cos, sin = _rope_tables(pos)
    cos, sin = jnp.tile(cos, (nb, 1)), jnp.tile(sin, (nb, 1))
    full = lambda shape: pl.BlockSpec(shape, lambda i: (0,) * len(shape))
    return pl.pallas_call(
        _qkv_small_kernel,
        out_shape=(jax.ShapeDtypeStruct((nb, s, d), F32), jax.ShapeDtypeStruct((nb, s, kvd), F32),
                   jax.ShapeDtypeStruct((nb, s, kvd), F32)),
        grid=(1,),
        in_specs=[full((nb, s, d)), full((1, d)), _layer_spec((d, d + 2 * kvd), layer),
                  full((nb * s, LANES)), full((nb * s, LANES))],
        out_specs=(full((nb, s, d)), full((nb, s, kvd)), full((nb, s, kvd))),
        compiler_params=_params("arbitrary"),
        name="qkv_small",
    )(h, g, w_qkv, cos, sin)


QT_TILE = 128
KPAD = LANES


def _qkv_t_kernel(h_ref, g_ref, wqt_ref, wk_ref, wvt_ref, wv_ref, cos_ref, sin_ref, cost_ref, sint_ref,
                  qt_ref, kpad_ref, vt_ref, kst_ref, vst_ref):
    t = pl.program_id(1)
    tm = h_ref.shape[1]
    half = HEAD_DIM // 2
    lane = lax.broadcasted_iota(jnp.int32, (1, LANES), 1)
    nparts = 2 if tm % (2 * LANES) == 0 else 1
    tp = tm // nparts
    parts = [slice(i * tp, (i + 1) * tp) for i in range(nparts)]
    hb = [_rms(h_ref[0, p, :], g_ref[...]).astype(BF16) for p in parts]
    zq = [_mm_nt(wqt_ref[...], x, False) for x in hb]
    zk = [jnp.dot(x, wk_ref[...], preferred_element_type=F32) for x in hb]
    zv = [_mm_nt(wvt_ref[...], x, False) for x in hb]
    kr = []
    for i, p in enumerate(parts):
        cost, sint = cost_ref[:, p], sint_ref[:, p]
        for hd in range(zq[i].shape[0] // HEAD_DIM):
            x1 = zq[i][hd * HEAD_DIM:hd * HEAD_DIM + half]
            x2 = zq[i][hd * HEAD_DIM + half:(hd + 1) * HEAD_DIM]
            qt_ref[0, hd * HEAD_DIM:hd * HEAD_DIM + half, p] = (x1 * cost - x2 * sint).astype(BF16)
            qt_ref[0, hd * HEAD_DIM + half:(hd + 1) * HEAD_DIM, p] = (x2 * cost + x1 * sint).astype(BF16)
        vt_ref[0, :, p] = zv[i].astype(BF16)
        blks = []
        for j in range(zk[i].shape[1] // LANES):
            blk = _rope_block(zk[i][:, j * LANES:(j + 1) * LANES], cos_ref[p, :], sin_ref[p, :])
            blks.append(blk)
            for sub in range(LANES // HEAD_DIM):
                hk = j * (LANES // HEAD_DIM) + sub
                shifted = blk if sub == 0 else pltpu.roll(blk, LANES - sub * HEAD_DIM, 1)
                kpad_ref[0, p, hk * KPAD:(hk + 1) * KPAD] = jnp.where(lane < HEAD_DIM, shifted, 0.0).astype(BF16)
        kr.append(jnp.concatenate(blks, axis=1))

    @pl.when(t == pl.num_programs(1) - 1)
    def _():
        kst_ref[0] = jnp.concatenate(kr, axis=0)[tm - WINDOW:]
        vst_ref[0] = jnp.dot(hb[-1][tp - WINDOW:], wv_ref[...], preferred_element_type=F32)


def _qkv_t(h, g, w_qkv, pos, *, tm):
    nb, s, d = h.shape
    kvd = N_KV_HEADS * HEAD_DIM
    half = HEAD_DIM // 2
    cos, sin = _rope_tables(pos)
    inv = ROPE_THETA ** (-jnp.arange(half, dtype=F32) / half)
    ang = inv[:, None] * pos.astype(F32)[None, :]
    cost, sint = jnp.cos(ang), jnp.sin(ang)
    wq_t = (w_qkv[:, :d] * (HEAD_DIM ** -0.5)).T.astype(BF16)
    wk = w_qkv[:, d:d + kvd].astype(BF16)
    wv = w_qkv[:, d + kvd:].astype(BF16)
    const = lambda shape: pl.BlockSpec(shape, lambda b, t: (0,) * len(shape))
    return pl.pallas_call(
        _qkv_t_kernel,
        out_shape=(jax.ShapeDtypeStruct((nb, d, s), BF16),
                   jax.ShapeDtypeStruct((nb, s, N_KV_HEADS * KPAD), BF16),
                   jax.ShapeDtypeStruct((nb, kvd, s), BF16),
                   jax.ShapeDtypeStruct((nb, WINDOW, kvd), F32),
                   jax.ShapeDtypeStruct((nb, WINDOW, kvd), F32)),
        grid=(nb, s // tm),
        in_specs=[
            pl.BlockSpec((1, tm, d), lambda b, t: (b, t, 0)),
            const((1, d)), const((d, d)), const((d, kvd)), const((kvd, d)), const((d, kvd)),
            pl.BlockSpec((tm, LANES), lambda b, t: (t, 0)),
            pl.BlockSpec((tm, LANES), lambda b, t: (t, 0)),
            pl.BlockSpec((half, tm), lambda b, t: (0, t)),
            pl.BlockSpec((half, tm), lambda b, t: (0, t)),
        ],
        out_specs=(pl.BlockSpec((1, d, tm), lambda b, t: (b, 0, t)),
                   pl.BlockSpec((1, tm, N_KV_HEADS * KPAD), lambda b, t: (b, t, 0)),
                   pl.BlockSpec((1, kvd, tm), lambda b, t: (b, 0, t)),
                   pl.BlockSpec((1, WINDOW, kvd), lambda b, t: (b, 0, 0)),
                   pl.BlockSpec((1, WINDOW, kvd), lambda b, t: (b, 0, 0))),
        compiler_params=_params("arbitrary", "arbitrary"),
        name="qkv_t",
    )(h, g, wq_t, wk, wv.T, wv, cos, sin, cost, sint)


def _attn_t_kernel(sinks_ref, qt_ref, kc_ref, kp_ref, vc_ref, vp_ref, mk_ref, mvt_ref, h_ref, wo_ref,
                   out_ref, ot_ref, *, tq):
    t = pl.program_id(1)
    gqa = qt_ref.shape[1] // (N_KV_HEADS * HEAD_DIM)
    band = QT_TILE + WIN_CHUNKS * CHUNK
    kk = jnp.concatenate([kp_ref[0], kc_ref[0]], axis=0)
    vv = jnp.concatenate([vp_ref[0], vc_ref[0]], axis=1)
    r = lax.broadcasted_iota(jnp.int32, (band, gqa * QT_TILE), 0)
    ln = lax.broadcasted_iota(jnp.int32, (band, gqa * QT_TILE), 1)
    kchunk = r // CHUNK
    qchunk = (ln // CHUNK) % (QT_TILE // CHUNK)
    visible = (kchunk >= qchunk) & (kchunk <= qchunk + WIN_CHUNKS)
    lgroup = lax.broadcasted_iota(jnp.int32, (1, gqa * QT_TILE), 1) // QT_TILE

    def project(p0, p1):
        proj = lax.dot_general(ot_ref[:, p0:p1], wo_ref[0], (((0,), (0,)), ((), ())), preferred_element_type=F32)
        out_ref[0, p0:p1, :] = h_ref[0, p0:p1, :] + proj

    projected = done = 0
    for sub in range(tq // QT_TILE):
        c0 = sub * QT_TILE
        if sub == 0:
            mask = visible & ((t > 0) | (r >= WIN_CHUNKS * CHUNK))
        else:
            mask = visible
        krows = kk[c0:c0 + band]
        vcols = vv[:, c0:c0 + band]
        heads = range(N_KV_HEADS)
        q4 = [jnp.concatenate(
            [qt_ref[0, (hk * gqa + g) * HEAD_DIM:(hk * gqa + g + 1) * HEAD_DIM, c0:c0 + QT_TILE]
             for g in range(gqa)], axis=1) for hk in heads]
        sb = [jnp.dot(krows[:, hk * KPAD:hk * KPAD + HEAD_DIM], q4[hk], preferred_element_type=F32)
              for hk in heads]
        sm = [jnp.dot(mk_ref[:, hk * KPAD:hk * KPAD + HEAD_DIM], q4[hk], preferred_element_type=F32)
              for hk in heads]
        sb = [jnp.where(mask, x, NEG_INF) for x in sb]
        if done - projected >= QT_TILE:
            project(projected, done)
            projected = done
        sink = []
        for hk in heads:
            row = jnp.zeros((1, gqa * QT_TILE), F32)
            for g in range(gqa):
                row = jnp.where(lgroup == g, sinks_ref[hk * gqa + g], row)
            sink.append(row)
        m = [jnp.maximum(jnp.maximum(jnp.max(sb[hk], axis=0, keepdims=True),
                                     jnp.max(sm[hk], axis=0, keepdims=True)), sink[hk]) for hk in heads]
        pb = [jnp.exp(sb[hk] - m[hk]) for hk in heads]
        pm = [jnp.exp(sm[hk] - m[hk]) for hk in heads]
        denom = [jnp.sum(pb[hk], axis=0, keepdims=True) + jnp.sum(pm[hk], axis=0, keepdims=True)
                 + jnp.exp(sink[hk] - m[hk]) for hk in heads]
        o = [(jnp.dot(vcols[hk * HEAD_DIM:(hk + 1) * HEAD_DIM], pb[hk].astype(BF16), preferred_element_type=F32)
              + jnp.dot(mvt_ref[hk * HEAD_DIM:(hk + 1) * HEAD_DIM, :], pm[hk].astype(BF16),
                        preferred_element_type=F32)) / denom[hk] for hk in heads]
        for hk in heads:
            for g in range(gqa):
                ot_ref[(hk * gqa + g) * HEAD_DIM:(hk * gqa + g + 1) * HEAD_DIM, c0:c0 + QT_TILE] = (
                    o[hk][:, g * QT_TILE:(g + 1) * QT_TILE].astype(BF16))
        done = (sub + 1) * QT_TILE
    project(projected, done)


def _attn_t(qt, kpad, vt, mkpad, mvt, sinks, h, w_o, layer, *, tq):
    nb, s, d = h.shape
    kvd = vt.shape[1]
    prev = WIN_CHUNKS * CHUNK
    ratio = tq // prev
    const = lambda shape: pl.BlockSpec(shape, lambda b, t: (0,) * len(shape))
    return pl.pallas_call(
        functools.partial(_attn_t_kernel, tq=tq),
        out_shape=jax.ShapeDtypeStruct((nb, s, d), F32),
        grid=(nb, s // tq),
        in_specs=[
            pl.BlockSpec(memory_space=pltpu.SMEM),
            pl.BlockSpec((1, d, tq), lambda b, t: (b, 0, t)),
            pl.BlockSpec((1, tq, N_KV_HEADS * KPAD), lambda b, t: (b, t, 0)),
            pl.BlockSpec((1, prev, N_KV_HEADS * KPAD), lambda b, t: (b, jnp.maximum(t * ratio - 1, 0), 0)),
            pl.BlockSpec((1, kvd, tq), lambda b, t: (b, 0, t)),
            pl.BlockSpec((1, kvd, prev), lambda b, t: (b, 0, jnp.maximum(t * ratio - 1, 0))),
            const((N_META, N_KV_HEADS * KPAD)), const((kvd, N_META)),
            pl.BlockSpec((1, tq, d), lambda b, t: (b, t, 0)),
            _layer_spec((d, d), layer),
        ],
        out_specs=pl.BlockSpec((1, tq, d), lambda b, t: (b, t, 0)),
        scratch_shapes=[pltpu.VMEM((d, tq), BF16)],
        compiler_params=_params("arbitrary", "arbitrary"),
        name="attn_t",
    )(sinks, qt, kpad, kpad, vt, vt, mkpad, mvt, h, w_o)


def _attn_small_kernel(sinks_ref, q_ref, k_ref, v_ref, h_ref, wo_ref, out_ref, o_ref):
    bb, s, d = q_ref.shape
    gqa = d // (N_KV_HEADS * HEAD_DIM)
    for b in range(bb):
        for hk in range(N_KV_HEADS):
            hs = slice(hk * HEAD_DIM, (hk + 1) * HEAD_DIM)
            qh = jnp.concatenate(
                [q_ref[b, :, (hk * gqa + g) * HEAD_DIM:(hk * gqa + g + 1) * HEAD_DIM] for g in range(gqa)],
                axis=0)
            sink = jnp.concatenate([jnp.full((s, 1), sinks_ref[hk * gqa + g], F32) for g in range(gqa)], axis=0)
            sc = _mm_nt(qh, k_ref[b, :, hs], True) * (HEAD_DIM ** -0.5)
            m = jnp.maximum(jnp.max(sc, axis=-1, keepdims=True), sink)
            p = jnp.exp(sc - m)
            denom = jnp.sum(p, axis=-1, keepdims=True) + jnp.exp(sink - m)
            o = _mm(p, v_ref[b, :, hs], True) / denom
            for g in range(gqa):
                o_ref[b * s:(b + 1) * s, (hk * gqa + g) * HEAD_DIM:(hk * gqa + g + 1) * HEAD_DIM] = (
                    o[g * s:(g + 1) * s])
    out_ref[...] = h_ref[...] + _mm(o_ref[...], wo_ref[0], True).reshape(bb, s, d)


def _attn_small(q, keys, vals, sinks, h, w_o, layer):
    nb, s, d = h.shape
    kn, kvd = keys.shape[1:]
    full = lambda shape: pl.BlockSpec(shape, lambda i: (0,) * len(shape))
    return pl.pallas_call(
        _attn_small_kernel,
        out_shape=jax.ShapeDtypeStruct((nb, s, d), F32),
        grid=(1,),
        in_specs=[pl.BlockSpec(memory_space=pltpu.SMEM), full((nb, s, d)), full((nb, kn, kvd)),
                  full((nb, kn, kvd)), full((nb, s, d)), _layer_spec((d, d), layer)],
        out_specs=full((nb, s, d)),
        scratch_shapes=[pltpu.VMEM((nb * s, d), F32)],
        compiler_params=_params("arbitrary"),
        name="attn_small",
    )(sinks, q, keys, vals, h, w_o)


ROUTE_ROWS = 32
MOE_CHUNK = 128
MOE_CALL_ROWS = (128, 144, 160)
SEG_ALIGN = 16


def _route_t(lt):
    n = lt.shape[1]
    row8 = lax.broadcasted_iota(jnp.int32, (SUBLANES, n), 0)
    lg = jnp.where(row8 < N_GROUPS, lt[0:SUBLANES], NEG_INF)
    gmax = jnp.max(lg, axis=0, keepdims=True)
    g_idx = jnp.min(jnp.where(lg == gmax, row8, SUBLANES), axis=0, keepdims=True)
    g_w = 1.0 / jnp.sum(jnp.exp(lg - gmax), axis=0, keepdims=True)
    le = lt[SUBLANES:SUBLANES + N_EXPERTS]
    row16 = lax.broadcasted_iota(jnp.int32, (N_EXPERTS, n), 0)
    in_group = (row16 // EXPERTS_PER_GROUP) == g_idx
    l1 = jnp.where(in_group, le, NEG_INF)
    m1 = jnp.max(l1, axis=0, keepdims=True)
    i1 = jnp.min(jnp.where(in_group & (l1 == m1), row16, N_EXPERTS), axis=0, keepdims=True)
    rest = in_group & (row16 != i1)
    l2 = jnp.where(rest, le, NEG_INF)
    m2 = jnp.max(l2, axis=0, keepdims=True)
    i2 = jnp.min(jnp.where(rest & (l2 == m2), row16, N_EXPERTS), axis=0, keepdims=True)
    e2 = jnp.exp(m2 - m1)
    p1 = 1.0 / (1.0 + e2)
    comb = g_w * (jnp.where(row16 == i1, p1, 0.0) + jnp.where(row16 == i2, e2 * p1, 0.0))
    c8 = comb[0:SUBLANES] + comb[SUBLANES:2 * SUBLANES]
    return g_idx, comb, c8 + pltpu.roll(c8, EXPERTS_PER_GROUP, 0)


def _route(logits):
    col = lax.broadcasted_iota(jnp.int32, logits.shape, 1)
    lg = jnp.where(col < N_GROUPS, logits, NEG_INF)
    gmax = jnp.max(lg, axis=-1, keepdims=True)
    g_idx = jnp.min(jnp.where(lg == gmax, col, LANES), axis=-1, keepdims=True)
    g_w = 1.0 / jnp.sum(jnp.exp(lg - gmax), axis=-1, keepdims=True)
    ecol = col - N_GROUPS
    in_group = (ecol >= 0) & (ecol < N_EXPERTS) & ((ecol // EXPERTS_PER_GROUP) == g_idx)
    l1 = jnp.where(in_group, logits, NEG_INF)
    m1 = jnp.max(l1, axis=-1, keepdims=True)
    i1 = jnp.min(jnp.where(in_group & (l1 == m1), col, LANES), axis=-1, keepdims=True)
    rest = in_group & (col != i1)
    l2 = jnp.where(rest, logits, NEG_INF)
    m2 = jnp.max(l2, axis=-1, keepdims=True)
    i2 = jnp.min(jnp.where(rest & (l2 == m2), col, LANES), axis=-1, keepdims=True)
    e2 = jnp.exp(m2 - m1)
    p1 = 1.0 / (1.0 + e2)
    return g_w * (jnp.where(col == i1, p1, 0.0) + jnp.where(col == i2, e2 * p1, 0.0))


def _router_logits_t(wr_ref, br_ref, hn_hi, hn_lo):
    a = _dot(wr_ref[0], hn_hi, (((1,), (1,)), ((), ())))
    b = _dot(wr_ref[0, 0:ROUTE_ROWS], hn_lo, (((1,), (1,)), ((), ())))
    return a[0:ROUTE_ROWS] + a[ROUTE_ROWS:2 * ROUTE_ROWS] + b + br_ref[0]


SMALL_EXPERTS_PER_STEP = 2


def _moe_small_kernel(h_ref, g_ref, wr_ref, br_ref, wg_ref, wu_ref, wd_ref, gf_ref,
                      out_ref, wgb_ref, wub_ref, wdb_ref, hn_ref, comb_ref, acc_ref, *, precise, final_norm):
    step = pl.program_id(0)
    dn = (((1,), (0,)), ((), ()))

    @pl.when(step == 0)
    def _():
        hn = _rms(h_ref[...], g_ref[...])
        hn_ref[0], hn_ref[1] = _split_bf16(hn, 2)
        comb_ref[...] = _route(_mm(hn, wr_ref[0], True) + br_ref[0])
        acc_ref[...] = jnp.zeros_like(acc_ref)

    def times(a_hi, a_lo, w):
        if not precise:
            w_hi = w.astype(BF16)
            return _dot(a_hi, w_hi, dn), w_hi
        w_hi, w_lo = _split_bf16(w, 2)
        return _dot(a_hi, w_hi, dn) + (_dot(a_hi, w_lo, dn) + _dot(a_lo, w_hi, dn)), w_hi

    js = range(SMALL_EXPERTS_PER_STEP)
    col = lax.broadcasted_iota(jnp.int32, comb_ref.shape, 1)
    c = [jnp.sum(jnp.where(col == step * SMALL_EXPERTS_PER_STEP + j + N_GROUPS, comb_ref[...], 0.0),
                 axis=-1, keepdims=True) for j in js]
    gates = [times(hn_ref[0], hn_ref[1], wg_ref[0, j]) for j in js]
    ups = [times(hn_ref[0], hn_ref[1], wu_ref[0, j]) for j in js]
    acts = [_split_bf16(gates[j][0] * jax.nn.sigmoid(gates[j][0]) * ups[j][0] * c[j], 2) for j in js]
    downs = [times(acts[j][0], acts[j][1], wd_ref[0, j]) for j in js]
    for j in js:
        wgb_ref[j], wub_ref[j], wdb_ref[j] = gates[j][1], ups[j][1], downs[j][1]
    acc_ref[...] += sum(downs[j][0] for j in js)

    @pl.when(step == pl.num_programs(0) - 1)
    def _():
        y = h_ref[...] + acc_ref[...]
        if final_norm:
            y = _rms(y, gf_ref[...])
        out_ref[...] = y


def _moe_small(h, g, wr, br, wg, wu, wd, layer, gf, *, precise, final_norm):
    n, d = h.shape
    _, ne, _, de = wg.shape
    k = SMALL_EXPERTS_PER_STEP
    const = lambda shape: pl.BlockSpec(shape, lambda e: (0,) * len(shape))
    return pl.pallas_call(
        functools.partial(_moe_small_kernel, precise=precise, final_norm=final_norm),
        out_shape=(jax.ShapeDtypeStruct((n, d), F32), jax.ShapeDtypeStruct((ne, d, de), BF16),
                   jax.ShapeDtypeStruct((ne, d, de), BF16), jax.ShapeDtypeStruct((ne, de, d), BF16)),
        grid=(ne // k,),
        in_specs=[
            const((n, d)), const((1, d)),
            _layer_spec((d, LANES), layer), _layer_spec((1, LANES), layer),
            pl.BlockSpec((1, k, d, de), lambda e: (layer, e, 0, 0)),
            pl.BlockSpec((1, k, d, de), lambda e: (layer, e, 0, 0)),
            pl.BlockSpec((1, k, de, d), lambda e: (layer, e, 0, 0)),
            const((1, d)),
        ],
        out_specs=(const((n, d)), pl.BlockSpec((k, d, de), lambda e: (e, 0, 0)),
                   pl.BlockSpec((k, d, de), lambda e: (e, 0, 0)), pl.BlockSpec((k, de, d), lambda e: (e, 0, 0))),
        scratch_shapes=[pltpu.VMEM((2, n, d), BF16), pltpu.VMEM((n, LANES), F32), pltpu.VMEM((n, d), F32)],
        compiler_params=_params("arbitrary"),
        name="moe_small",
    )(h, g, wr, br, wg, wu, wd, gf)


def _moe_sparse_kernel(h_ref, g_ref, wr_ref, br_ref, tri_ref, wg_ref, wu_ref, wd_ref, gf_ref, out_ref,
                       xs_ref, ys_ref, p_ref, cs_ref, *, tm, nsub, final_norm):
    rows = p_ref.shape[1]

    @pl.when(pl.program_id(0) == 0)
    def _():
        xs_ref[:, rows:, :] = jnp.zeros((nsub, xs_ref.shape[1] - rows, xs_ref.shape[2]), BF16)
        cs_ref[:, rows:, :] = jnp.zeros((nsub, cs_ref.shape[1] - rows, cs_ref.shape[2]), F32)

    def prep():
        tiles = range(nsub)
        row8 = lax.broadcasted_iota(jnp.int32, (SUBLANES, tm), 0)
        riota = lax.broadcasted_iota(jnp.int32, (rows, tm), 0)
        hn = [_rms(h_ref[t * tm:(t + 1) * tm, :], g_ref[...]) for t in tiles]
        hn_split = [_split_bf16(x, 2) for x in hn]
        logits = [_router_logits_t(wr_ref, br_ref, hi, lo) for hi, lo in hn_split]
        routed = [_route_t(lt) for lt in logits]
        onehot = [row8 == g_idx for g_idx, _, _ in routed]
        incl = [jnp.dot(jnp.where(oh, 1.0, 0.0).astype(BF16), tri_ref[...], preferred_element_type=F32)
                for oh in onehot]
        counts = [x[:, tm - 1:tm].astype(jnp.int32) for x in incl]
        plans = []
        for t in tiles:
            n = [counts[t][g, 0] for g in range(N_GROUPS)]
            starts = [jnp.int32(0)]
            for g in range(N_GROUPS - 1):
                starts.append(starts[-1] + (n[g] + SEG_ALIGN - 1) // SEG_ALIGN * SEG_ALIGN)
            plans.append((n, starts))
        for t in tiles:
            g_idx = routed[t][0]
            rank = jnp.sum(jnp.where(onehot[t], incl[t], 0.0), axis=0, keepdims=True).astype(jnp.int32) - 1
            start_tok = jnp.zeros_like(g_idx)
            for g in range(1, N_GROUPS):
                start_tok = jnp.where(g_idx == g, plans[t][1][g], start_tok)
            p_ref[t] = jnp.where(riota == start_tok + rank, 1.0, 0.0).astype(BF16)
        for t in tiles:
            xs_ref[t, 0:rows, :] = jnp.dot(p_ref[t], hn_split[t][0], preferred_element_type=F32).astype(BF16)
            comb_parts = jnp.concatenate(_split_bf16(routed[t][2], 3), axis=0)
            cs = _dot(p_ref[t], comb_parts, (((1,), (1,)), ((), ())))
            cs_ref[t, 0:rows, :] = (cs[:, 0:SUBLANES] + cs[:, SUBLANES:2 * SUBLANES]
                                    + cs[:, 2 * SUBLANES:3 * SUBLANES])
            ys_ref[t, 0:rows, :] = jnp.zeros((rows, ys_ref.shape[2]), BF16)
        return plans

    def experts(t, g, r0, m):
        x = xs_ref[t, pl.ds(r0, m), :]
        cc = cs_ref[t, pl.ds(r0, m), :]
        acts = []
        for j in range(EXPERTS_PER_GROUP):
            e = g * EXPERTS_PER_GROUP + j
            gate = jnp.dot(x, wg_ref[e], preferred_element_type=F32)
            up = jnp.dot(x, wu_ref[e], preferred_element_type=F32)
            acts.append((gate * jax.nn.sigmoid(gate) * up * cc[:, j:j + 1]).astype(BF16))
        y = jnp.dot(jnp.concatenate(acts, axis=1), wd_ref[g], preferred_element_type=F32)
        ys_ref[t, pl.ds(r0, m), :] = y.astype(BF16)

    plans = prep()

    for t, (n, starts) in enumerate(plans):
        for g in range(N_GROUPS):
            seg = pl.multiple_of(starts[g], SEG_ALIGN)
            lo = 0
            for m in MOE_CALL_ROWS:
                @pl.when((n[g] > lo) & (n[g] <= m))
                def _(t=t, g=g, seg=seg, m=m):
                    experts(t, g, seg, m)

                lo = m

            @pl.when(n[g] > MOE_CALL_ROWS[-1])
            def _(t=t, g=g, n=n, starts=starts):
                def chunk(c, carry):
                    experts(t, g, pl.multiple_of(starts[g] + c * MOE_CHUNK, SEG_ALIGN), MOE_CHUNK)
                    return carry

                lax.fori_loop(0, (n[g] + MOE_CHUNK - 1) // MOE_CHUNK, chunk, 0)

    for t in range(nsub):
        back = lax.dot_general(p_ref[t], ys_ref[t, 0:rows, :], (((0,), (0,)), ((), ())),
                               preferred_element_type=F32)
        y = h_ref[t * tm:(t + 1) * tm, :] + back
        if final_norm:
            y = _rms(y, gf_ref[...])
        out_ref[t * tm:(t + 1) * tm, :] = y


def _moe_sparse(h, g, wr2, brt, tri, wg, wu, wd4, layer, gf, *, tm, nsub, final_norm):
    n, d = h.shape
    ne, _, de = wg.shape
    rows = tm + N_GROUPS * SEG_ALIGN
    over = rows + MOE_CHUNK
    resident = lambda shape: pl.BlockSpec(shape, lambda i: (0,) * len(shape), pipeline_mode=pl.Buffered(1))
    const = lambda shape: pl.BlockSpec(shape, lambda i: (0,) * len(shape))
    return pl.pallas_call(
        functools.partial(_moe_sparse_kernel, tm=tm, nsub=nsub, final_norm=final_norm),
        out_shape=jax.ShapeDtypeStruct((n, d), F32),
        grid=(n // (tm * nsub),),
        in_specs=[
            pl.BlockSpec((tm * nsub, d), lambda i: (i, 0)), const((1, d)),
            _layer_spec((2 * ROUTE_ROWS, d), layer), _layer_spec((ROUTE_ROWS, 1), layer),
            const((tm, tm)),
            resident((ne, d, de)), resident((ne, d, de)), resident((N_GROUPS, EXPERTS_PER_GROUP * de, d)),
            const((1, d)),
        ],
        out_specs=pl.BlockSpec((tm * nsub, d), lambda i: (i, 0)),
        scratch_shapes=[pltpu.VMEM((nsub, over, d), BF16), pltpu.VMEM((nsub, over, d), BF16),
                        pltpu.VMEM((nsub, rows, tm), BF16), pltpu.VMEM((nsub, over, SUBLANES), F32)],
        compiler_params=pltpu.CompilerParams(dimension_semantics=("arbitrary",),
                                             vmem_limit_bytes=MOE_VMEM_LIMIT),
        name="moe_sparse",
    )(h, g, wr2, brt, tri, wg, wu, wd4, gf)


def kernel(x_prompt, x_sample, state_pool, cache_swa_kv, cache_meta_kv, state_conv, meta_tokens, norm_mix, norm_ffn, norm_final, w_pool, pool_scale, w_qkv, w_o, attn_sinks, w_conv_in, conv_w, w_conv_out, w_group, b_group, w_expert_router, b_expert_router, w_gate, w_up, w_down):
    nb, seq, d = x_prompt.shape
    db, dseq, _ = x_sample.shape
    depth = norm_mix.shape[0]
    kvd = N_KV_HEADS * HEAD_DIM
    tm_main = min(512, seq)
    tq_main = 1024 if seq % 1024 == 0 else min(512, seq)
    tm_pool = 1024 if seq % 1024 == 0 else tm_main
    moe_sub = 2 if (nb * seq) % (2 * tm_main) == 0 else 1
    halo = POOL_STATE + 1

    row = lambda a: a.reshape(1, -1).astype(F32)
    rpad = lambda a, k: jnp.pad(a, ((0, 0), (0, k)) + ((0, 0),) * (a.ndim - 2))
    wrt = jnp.concatenate([rpad(jnp.swapaxes(w_group, 1, 2), SUBLANES - N_GROUPS),
                           rpad(jnp.swapaxes(w_expert_router, 1, 2), ROUTE_ROWS - SUBLANES - N_EXPERTS)], axis=1)
    wr2 = jnp.concatenate(_split_bf16(wrt, 2), axis=1)
    brt = jnp.concatenate([rpad(b_group, SUBLANES - N_GROUPS),
                           rpad(b_expert_router, ROUTE_ROWS - SUBLANES - N_EXPERTS)], axis=1)[..., None].astype(F32)
    wr = jnp.pad(jnp.concatenate([w_group, w_expert_router], axis=-1).astype(F32),
                 ((0, 0), (0, 0), (0, LANES - N_GROUPS - N_EXPERTS)))
    br = jnp.pad(jnp.concatenate([b_group, b_expert_router], axis=-1).astype(F32),
                 ((0, 0), (0, LANES - N_GROUPS - N_EXPERTS)))[:, None, :]
    tri = jnp.triu(jnp.ones((tm_main, tm_main), BF16))
    bf = lambda a: a.astype(BF16)
    w_pool_b, w_o_b = bf(w_pool), bf(w_o)
    w_conv_in_b, w_conv_out_b = bf(w_conv_in), bf(w_conv_out)

    hm = meta_tokens.astype(F32)[None]
    hp = x_prompt
    hs = x_sample
    pool_p, swa_p, meta_p, conv_p, pool_s, swa_s, conv_s = [], [], [], [], [], [], []
    for i in range(depth):
        j = i // N_MIXERS
        g = row(norm_mix[i])
        if i % N_MIXERS == 0:
            sc = row(pool_scale[j])
            hm, st_m = _pool_mix(hm, jnp.zeros((1, halo, d), F32), g, w_pool, j, sc, bb=1, tm=N_META,
                                 has_history=False, precise=True)
            hp, st_p = _pool_mix(hp, st_m, g, w_pool_b, j, sc, bb=1, tm=tm_pool, has_history=True,
                                 precise=False)
            hist_s = jnp.pad(state_pool[j].astype(F32), ((0, 0), (1, 0), (0, 0)))
            hs, st_s = _pool_mix(hs, hist_s, g, w_pool, j, sc, bb=db, tm=dseq, has_history=True,
                                 precise=True)
            pool_p.append(st_p[:, 1:])
            pool_s.append(st_s[:, 1:])
        elif i % N_MIXERS == 1:
            sinks = attn_sinks[j].astype(F32)
            qm, km, vm = _qkv_small(hm, g, w_qkv, j, jnp.arange(N_META))
            qt, kpad, vt, kst, vst = _qkv_t(hp, g, w_qkv[j], N_META + jnp.arange(seq), tm=tq_main)
            qs, ks, vs = _qkv_small(hs, g, w_qkv, j, PAST_LEN + N_META + jnp.arange(dseq))
            hm = _attn_small(qm, km, vm, sinks, hm, w_o, j)
            mkpad = jnp.pad(km[0].reshape(N_META, N_KV_HEADS, HEAD_DIM),
                            ((0, 0), (0, 0), (0, KPAD - HEAD_DIM))).reshape(N_META, -1).astype(BF16)
            hp = _attn_t(qt, kpad, vt, mkpad, vm[0].T.astype(BF16), sinks, hp, w_o_b, j, tq=tq_main)
            flat = lambda a: a.reshape(a.shape[0], a.shape[1], kvd)
            keys = jnp.concatenate([flat(cache_meta_kv[j][:, :, 0]), flat(cache_swa_kv[j][:, :, 0]), ks], axis=1)
            vals = jnp.concatenate([flat(cache_meta_kv[j][:, :, 1]), flat(cache_swa_kv[j][:, :, 1]), vs], axis=1)
            hs = _attn_small(qs, keys, vals, sinks, hs, w_o, j)
            heads = lambda a: a.reshape(a.shape[0], a.shape[1], N_KV_HEADS, HEAD_DIM)
            swa_p.append(jnp.stack([heads(kst), heads(vst)], axis=2))
            meta_kv = jnp.stack([heads(km), heads(vm)], axis=2)
            meta_p.append(jnp.broadcast_to(meta_kv, (nb,) + meta_kv.shape[1:]))
            swa_s.append(jnp.stack([heads(ks), heads(vs)], axis=2))
        else:
            cw = conv_w[j].astype(F32)
            hm, st_m = _conv_mix(hm, jnp.zeros((1, SUBLANES, d), F32), g, w_conv_in, j, cw, w_conv_out,
                                 bb=1, tm=N_META, precise=True)
            hp, st_p = _conv_mix(hp, st_m, g, w_conv_in_b, j, cw, w_conv_out_b, bb=1, tm=tq_main,
                                 precise=False)
            hist_s = jnp.pad(state_conv[j].astype(F32), ((0, 0), (SUBLANES - (CONV_WIDTH - 1), 0), (0, 0)))
            hs, st_s = _conv_mix(hs, hist_s, g, w_conv_in, j, cw, w_conv_out, bb=db, tm=dseq, precise=True)
            conv_p.append(st_p[:, SUBLANES - (CONV_WIDTH - 1):])
            conv_s.append(st_s[:, SUBLANES - (CONV_WIDTH - 1):])

        final = i == depth - 1
        gf = row(norm_final)
        gn = row(norm_ffn[i])
        small = jnp.concatenate([hm.reshape(-1, d), hs.reshape(-1, d)], axis=0)
        precise = any(k % N_MIXERS != 0 for k in range(i + 1, depth))
        small, wg_b, wu_b, wd_b = _moe_small(small, gn, wr, br, w_gate, w_up, w_down, i, gf,
                                             precise=precise, final_norm=final)
        hm = small[:N_META].reshape(1, N_META, d)
        hs = small[N_META:].reshape(db, dseq, d)
        hp = _moe_sparse(hp.reshape(-1, d), gn, wr2, brt, tri, wg_b, wu_b, wd_b.reshape(N_GROUPS, -1, d), i, gf,
                         tm=tm_main, nsub=moe_sub, final_norm=final).reshape(nb, seq, d)

    return (hp, hs, jnp.stack(pool_p), jnp.stack(swa_p), jnp.stack(meta_p), jnp.stack(conv_p),
            jnp.stack(pool_s), jnp.stack(swa_s), jnp.stack(conv_s))
```

```python
import functools

import jax
import jax.numpy as jnp
from jax import lax
from jax.experimental import pallas as pl
from jax.experimental.pallas import tpu as pltpu

F32 = jnp.float32
BF16 = jnp.bfloat16

CHUNK = 64
N_META = 16
N_MIXERS = 3
POOL_WINDOWS = (2, 4, 8, 16)
POOL_STATE = max(POOL_WINDOWS) - 1
HEAD_DIM = 64
N_KV_HEADS = 4
WINDOW = 128
WIN_CHUNKS = WINDOW // CHUNK
ROPE_THETA = 10000.0
CONV_WIDTH = 3
N_GROUPS = 4
EXPERTS_PER_GROUP = 4
N_EXPERTS = N_GROUPS * EXPERTS_PER_GROUP
PAST_LEN = 2048
EPS = 1e-6

LANES = 128
SUBLANES = 8
VMEM_LIMIT = 48 * 1024 * 1024
MOE_VMEM_LIMIT = 60 * 1024 * 1024
NEG_INF = float("-inf")


def _params(*sem):
    return pltpu.CompilerParams(dimension_semantics=sem, vmem_limit_bytes=VMEM_LIMIT)


def _split_bf16(x, parts):
    out = []
    x = x.astype(F32)
    for _ in range(parts):
        hi = x.astype(BF16)
        out.append(hi)
        x = x - hi.astype(F32)
    return out


def _dot(a, b, dn):
    return lax.dot_general(a, b, dn, preferred_element_type=F32)


def _mm_dn(a, b, dn, precise):
    if not precise:
        return _dot(a.astype(BF16), b.astype(BF16), dn)
    a_hi, a_lo = _split_bf16(a, 2)
    b_hi, b_lo = _split_bf16(b, 2)
    return _dot(a_hi, b_hi, dn) + (_dot(a_hi, b_lo, dn) + _dot(a_lo, b_hi, dn))


def _mm(a, b, precise):
    return _mm_dn(a, b, (((1,), (0,)), ((), ())), precise)


def _mm_nt(a, b, precise):
    return _mm_dn(a, b, (((1,), (1,)), ((), ())), precise)


def _rms(x, g):
    ms = jnp.mean(x * x, axis=-1, keepdims=True)
    return x * lax.rsqrt(ms + EPS) * g


def _layer_spec(shape, layer, single=False):
    nd = len(shape)
    mode = dict(pipeline_mode=pl.Buffered(1)) if single else {}
    return pl.BlockSpec((1,) + tuple(shape), lambda *_: (layer,) + (0,) * nd, **mode)


POOL_HALO = 16
POOL_LEAD = 16


def _pool_kernel(h_ref, hist_ref, g_ref, w_ref, scale_ref, out_ref, state_ref, buf_ref, sa_ref, sb_ref, *,
                 tm, has_history, precise):
    t = pl.program_id(1)
    base = POOL_LEAD + POOL_HALO
    bb, _, d = h_ref.shape
    pg = d // len(POOL_WINDOWS)
    end = base + tm

    @pl.when(t == 0)
    def _():
        buf_ref[:, 0:POOL_LEAD, :] = jnp.zeros((bb, POOL_LEAD, d), F32)
        buf_ref[:, POOL_LEAD:base, :] = jnp.broadcast_to(hist_ref[...], (bb, POOL_HALO, d))

    h = h_ref[...]
    hn = _rms(h, g_ref[...])
    buf_ref[:, base:end, :] = hn
    if not has_history:
        pos = t * tm + lax.broadcasted_iota(jnp.int32, (1, tm, 1), 1)
    ys = []
    for gi, w in enumerate(POOL_WINDOWS):
        c0, c1 = gi * pg, (gi + 1) * pg
        src, dst, shift, lo = buf_ref, sa_ref, 1, SUBLANES
        while shift < w:
            last = 2 * shift == w
            lo_k = base if last else lo
            s = src[:, lo_k:end, c0:c1] + src[:, lo_k - shift:end - shift, c0:c1]
            if last:
                win = s
            else:
                dst[:, lo_k:end, c0:c1] = s
                src, dst = dst, (sb_ref if dst is sa_ref else sa_ref)
            shift, lo = 2 * shift, lo + SUBLANES
        if has_history:
            mean = win * (1.0 / w)
        else:
            mean = win / jnp.minimum(pos + 1, w).astype(F32)
        ys.append(_mm((mean - hn[:, :, c0:c1]).reshape(bb * tm, pg), w_ref[0, gi], precise))
    y = jnp.concatenate(ys, axis=1).reshape(bb, tm, d) * scale_ref[...]
    out_ref[...] = h + y
    tail = buf_ref[:, end - POOL_HALO:end, :]
    buf_ref[:, POOL_LEAD:base, :] = tail

    @pl.when(t == pl.num_programs(1) - 1)
    def _():
        state_ref[...] = tail


def _pool_mix(h, hist, g, w, layer, scale, *, bb, tm, has_history, precise):
    nb, s, d = h.shape
    halo = POOL_HALO
    pg = d // len(POOL_WINDOWS)
    hist_map = (lambda b, t: (b, 0, 0)) if hist.shape[0] == nb and nb > 1 else (lambda b, t: (0, 0, 0))
    hb = bb if hist.shape[0] == nb and nb > 1 else 1
    out, state = pl.pallas_call(
        functools.partial(_pool_kernel, tm=tm, has_history=has_history, precise=precise),
        out_shape=(jax.ShapeDtypeStruct((nb, s, d), F32), jax.ShapeDtypeStruct((nb, halo, d), F32)),
        grid=(nb // bb, s // tm),
        in_specs=[
            pl.BlockSpec((bb, tm, d), lambda b, t: (b, t, 0)),
            pl.BlockSpec((hb, halo, d), hist_map),
            pl.BlockSpec((1, d), lambda b, t: (0, 0)),
            _layer_spec((len(POOL_WINDOWS), pg, pg), layer),
            pl.BlockSpec((1, d), lambda b, t: (0, 0)),
        ],
        out_specs=(pl.BlockSpec((bb, tm, d), lambda b, t: (b, t, 0)),
                   pl.BlockSpec((bb, halo, d), lambda b, t: (b, 0, 0))),
        scratch_shapes=[pltpu.VMEM((bb, POOL_LEAD + halo + tm, d), F32)] * 3,
        compiler_params=_params("arbitrary", "arbitrary"),
        name="pool_mix",
    )(h, hist, g, w, scale)
    return out, state


def _conv_kernel(h_ref, hist_ref, g_ref, win_ref, cw_ref, wout_ref, out_ref, state_ref, buf_ref, *,
                 tm, precise):
    t = pl.program_id(1)
    bb, _, d = h_ref.shape

    @pl.when(t == 0)
    def _():
        buf_ref[:, 0:SUBLANES, :] = jnp.broadcast_to(hist_ref[...], (bb, SUBLANES, d))

    h = h_ref[...]
    hn = _rms(h, g_ref[...])
    z = _mm(hn.reshape(bb * tm, d), win_ref[0], precise)
    gate_b = z[:, 0:d]
    buf_ref[:, SUBLANES:SUBLANES + tm, :] = (z[:, d:2 * d] * z[:, 2 * d:3 * d]).reshape(bb, tm, d)
    first = SUBLANES - (CONV_WIDTH - 1)
    acc = buf_ref[:, first:first + tm, :] * cw_ref[0:1, :]
    for k in range(1, CONV_WIDTH):
        acc = acc + buf_ref[:, first + k:first + k + tm, :] * cw_ref[k:k + 1, :]
    y = _mm(gate_b * acc.reshape(bb * tm, d), wout_ref[0], precise)
    out_ref[...] = h + y.reshape(bb, tm, d)
    tail = buf_ref[:, tm:tm + SUBLANES, :]
    buf_ref[:, 0:SUBLANES, :] = tail

    @pl.when(t == pl.num_programs(1) - 1)
    def _():
        state_ref[...] = tail


def _conv_mix(h, hist, g, w_in, layer, cw, w_out, *, bb, tm, precise):
    nb, s, d = h.shape
    per_batch = hist.shape[0] == nb and nb > 1
    hist_map = (lambda b, t: (b, 0, 0)) if per_batch else (lambda b, t: (0, 0, 0))
    out, state = pl.pallas_call(
        functools.partial(_conv_kernel, tm=tm, precise=precise),
        out_shape=(jax.ShapeDtypeStruct((nb, s, d), F32), jax.ShapeDtypeStruct((nb, SUBLANES, d), F32)),
        grid=(nb // bb, s // tm),
        in_specs=[
            pl.BlockSpec((bb, tm, d), lambda b, t: (b, t, 0)),
            pl.BlockSpec((bb if per_batch else 1, SUBLANES, d), hist_map),
            pl.BlockSpec((1, d), lambda b, t: (0, 0)),
            _layer_spec((d, 3 * d), layer, single=True),
            pl.BlockSpec((CONV_WIDTH, d), lambda b, t: (0, 0)),
            _layer_spec((d, d), layer, single=True),
        ],
        out_specs=(pl.BlockSpec((bb, tm, d), lambda b, t: (b, t, 0)),
                   pl.BlockSpec((bb, SUBLANES, d), lambda b, t: (b, 0, 0))),
        scratch_shapes=[pltpu.VMEM((bb, tm + SUBLANES, d), F32)],
        compiler_params=_params("arbitrary", "arbitrary"),
        name="conv_mix",
    )(h, hist, g, w_in, cw, w_out)
    return out, state


def _rope_tables(pos):
    half = HEAD_DIM // 2
    inv = ROPE_THETA ** (-jnp.arange(half, dtype=F32) / half)
    ang = pos.astype(F32)[:, None] * inv[None, :]
    cos, sin = jnp.cos(ang), jnp.sin(ang)
    reps = LANES // HEAD_DIM
    return (jnp.tile(jnp.concatenate([cos, cos], axis=1), (1, reps)),
            jnp.tile(jnp.concatenate([-sin, sin], axis=1), (1, reps)))


def _rope_block(blk, cos, sin):
    half = HEAD_DIM // 2
    lane = lax.broadcasted_iota(jnp.int32, (1, LANES), 1)
    partner = jnp.where((lane % HEAD_DIM) < half, pltpu.roll(blk, LANES - half, 1), pltpu.roll(blk, half, 1))
    return blk * cos + partner * sin


def _qkv_small_kernel(h_ref, g_ref, w_ref, cos_ref, sin_ref, q_ref, k_ref, v_ref):
    bb, s, d = h_ref.shape
    kvd = k_ref.shape[-1]
    hn = _rms(h_ref[...], g_ref[...]).reshape(bb * s, d)
    z = _mm(hn, w_ref[0], True)
    cos, sin = cos_ref[...], sin_ref[...]
    q = [_rope_block(z[:, j * LANES:(j + 1) * LANES], cos, sin) for j in range(d // LANES)]
    k = [_rope_block(z[:, d + j * LANES:d + (j + 1) * LANES], cos, sin) for j in range(kvd // LANES)]
    q_ref[...] = jnp.concatenate(q, axis=1).reshape(bb, s, d)
    k_ref[...] = jnp.concatenate(k, axis=1).reshape(bb, s, kvd)
    v_ref[...] = z[:, d + kvd:d + 2 * kvd].reshape(bb, s, kvd)


def _qkv_small(h, g, w_qkv, layer, pos):
    nb, s, d = h.shape
    kvd = N_KV_HEADS * HEAD_DIM
    cos, sin = _rope_tables(pos)
    cos, sin = jnp.tile(cos, (nb, 1)), jnp.tile(sin, (nb, 1))
    full = lambda shape: pl.BlockSpec(shape, lambda i: (0,) * len(shape))
    return pl.pallas_call(
        _qkv_small_kernel,
        out_shape=(jax.ShapeDtypeStruct((nb, s, d), F32), jax.ShapeDtypeStruct((nb, s, kvd), F32),
                   jax.ShapeDtypeStruct((nb, s, kvd), F32)),
        grid=(1,),
        in_specs=[full((nb, s, d)), full((1, d)), _layer_spec((d, d + 2 * kvd), layer),
                  full((nb * s, LANES)), full((nb * s, LANES))],
        out_specs=(full((nb, s, d)), full((nb, s, kvd)), full((nb, s, kvd))),
        compiler_params=_params("arbitrary"),
        name="qkv_small",
    )(h, g, w_qkv, cos, sin)


QT_TILE = 128
KPAD = LANES


def _qkv_t_kernel(h_ref, g_ref, wqt_ref, wk_ref, wvt_ref, wv_ref, cos_ref, sin_ref, cost_ref, sint_ref,
                  qt_ref, kpad_ref, vt_ref, kst_ref, vst_ref):
    t = pl.program_id(1)
    tm = h_ref.shape[1]
    half = HEAD_DIM // 2
    lane = lax.broadcasted_iota(jnp.int32, (1, LANES), 1)
    nparts = 4 if tm % (4 * WINDOW) == 0 else 1
    tp = tm // nparts
    parts = [slice(i * tp, (i + 1) * tp) for i in range(nparts)]
    hb = [_rms(h_ref[0, p, :], g_ref[...]).astype(BF16) for p in parts]
    zq = [_mm_nt(wqt_ref[...], x, False) for x in hb]
    zk = [jnp.dot(x, wk_ref[...], preferred_element_type=F32) for x in hb]
    zv = [_mm_nt(wvt_ref[...], x, False) for x in hb]
    kr = []
    for i, p in enumerate(parts):
        cost, sint = cost_ref[:, p], sint_ref[:, p]
        for hd in range(zq[i].shape[0] // HEAD_DIM):
            x1 = zq[i][hd * HEAD_DIM:hd * HEAD_DIM + half]
            x2 = zq[i][hd * HEAD_DIM + half:(hd + 1) * HEAD_DIM]
            qt_ref[0, hd * HEAD_DIM:hd * HEAD_DIM + half, p] = (x1 * cost - x2 * sint).astype(BF16)
            qt_ref[0, hd * HEAD_DIM + half:(hd + 1) * HEAD_DIM, p] = (x2 * cost + x1 * sint).astype(BF16)
        vt_ref[0, :, p] = zv[i].astype(BF16)
        blks = []
        for j in range(zk[i].shape[1] // LANES):
            blk = _rope_block(zk[i][:, j * LANES:(j + 1) * LANES], cos_ref[p, :], sin_ref[p, :])
            blks.append(blk)
            for sub in range(LANES // HEAD_DIM):
                hk = j * (LANES // HEAD_DIM) + sub
                shifted = blk if sub == 0 else pltpu.roll(blk, LANES - sub * HEAD_DIM, 1)
                kpad_ref[0, p, hk * KPAD:(hk + 1) * KPAD] = jnp.where(lane < HEAD_DIM, shifted, 0.0).astype(BF16)
        kr.append(jnp.concatenate(blks, axis=1))

    @pl.when(t == pl.num_programs(1) - 1)
    def _():
        kst_ref[0] = jnp.concatenate(kr, axis=0)[tm - WINDOW:]
        vst_ref[0] = jnp.dot(hb[-1][tp - WINDOW:], wv_ref[...], preferred_element_type=F32)


def _qkv_t(h, g, w_qkv, pos, *, tm):
    nb, s, d = h.shape
    kvd = N_KV_HEADS * HEAD_DIM
    half = HEAD_DIM // 2
    cos, sin = _rope_tables(pos)
    inv = ROPE_THETA ** (-jnp.arange(half, dtype=F32) / half)
    ang = inv[:, None] * pos.astype(F32)[None, :]
    cost, sint = jnp.cos(ang), jnp.sin(ang)
    wq_t = (w_qkv[:, :d] * (HEAD_DIM ** -0.5)).T.astype(BF16)
    wk = w_qkv[:, d:d + kvd].astype(BF16)
    wv = w_qkv[:, d + kvd:].astype(BF16)
    const = lambda shape: pl.BlockSpec(shape, lambda b, t: (0,) * len(shape))
    return pl.pallas_call(
        _qkv_t_kernel,
        out_shape=(jax.ShapeDtypeStruct((nb, d, s), BF16),
                   jax.ShapeDtypeStruct((nb, s, N_KV_HEADS * KPAD), BF16),
                   jax.ShapeDtypeStruct((nb, kvd, s), BF16),
                   jax.ShapeDtypeStruct((nb, WINDOW, kvd), F32),
                   jax.ShapeDtypeStruct((nb, WINDOW, kvd), F32)),
        grid=(nb, s // tm),
        in_specs=[
            pl.BlockSpec((1, tm, d), lambda b, t: (b, t, 0)),
            const((1, d)), const((d, d)), const((d, kvd)), const((kvd, d)), const((d, kvd)),
            pl.BlockSpec((tm, LANES), lambda b, t: (t, 0)),
            pl.BlockSpec((tm, LANES), lambda b, t: (t, 0)),
            pl.BlockSpec((half, tm), lambda b, t: (0, t)),
            pl.BlockSpec((half, tm), lambda b, t: (0, t)),
        ],
        out_specs=(pl.BlockSpec((1, d, tm), lambda b, t: (b, 0, t)),
                   pl.BlockSpec((1, tm, N_KV_HEADS * KPAD), lambda b, t: (b, t, 0)),
                   pl.BlockSpec((1, kvd, tm), lambda b, t: (b, 0, t)),
                   pl.BlockSpec((1, WINDOW, kvd), lambda b, t: (b, 0, 0)),
                   pl.BlockSpec((1, WINDOW, kvd), lambda b, t: (b, 0, 0))),
        compiler_params=_params("arbitrary", "arbitrary"),
        name="qkv_t",
    )(h, g, wq_t, wk, wv.T, wv, cos, sin, cost, sint)


def _attn_t_kernel(sinks_ref, qt_ref, kc_ref, kp_ref, vc_ref, vp_ref, mk_ref, mvt_ref, h_ref, wo_ref,
                   out_ref, ot_ref, *, tq):
    t = pl.program_id(1)
    gqa = qt_ref.shape[1] // (N_KV_HEADS * HEAD_DIM)
    band = QT_TILE + WIN_CHUNKS * CHUNK
    kk = jnp.concatenate([kp_ref[0], kc_ref[0]], axis=0)
    vv = jnp.concatenate([vp_ref[0], vc_ref[0]], axis=1)
    r = lax.broadcasted_iota(jnp.int32, (band, gqa * QT_TILE), 0)
    ln = lax.broadcasted_iota(jnp.int32, (band, gqa * QT_TILE), 1)
    kchunk = r // CHUNK
    qchunk = (ln // CHUNK) % (QT_TILE // CHUNK)
    visible = (kchunk >= qchunk) & (kchunk <= qchunk + WIN_CHUNKS)
    lgroup = lax.broadcasted_iota(jnp.int32, (1, gqa * QT_TILE), 1) // QT_TILE

    def project(p0, p1):
        proj = lax.dot_general(ot_ref[:, p0:p1], wo_ref[0], (((0,), (0,)), ((), ())), preferred_element_type=F32)
        out_ref[0, p0:p1, :] = h_ref[0, p0:p1, :] + proj

    projected = done = 0
    for sub in range(tq // QT_TILE):
        c0 = sub * QT_TILE
        if sub == 0:
            mask = visible & ((t > 0) | (r >= WIN_CHUNKS * CHUNK))
        else:
            mask = visible
        krows = kk[c0:c0 + band]
        vcols = vv[:, c0:c0 + band]
        heads = range(N_KV_HEADS)
        q4 = [jnp.concatenate(
            [qt_ref[0, (hk * gqa + g) * HEAD_DIM:(hk * gqa + g + 1) * HEAD_DIM, c0:c0 + QT_TILE]
             for g in range(gqa)], axis=1) for hk in heads]
        sb = [jnp.dot(krows[:, hk * KPAD:hk * KPAD + HEAD_DIM], q4[hk], preferred_element_type=F32)
              for hk in heads]
        sm = [jnp.dot(mk_ref[:, hk * KPAD:hk * KPAD + HEAD_DIM], q4[hk], preferred_element_type=F32)
              for hk in heads]
        sb = [jnp.where(mask, x, NEG_INF) for x in sb]
        if done - projected >= QT_TILE:
            project(projected, done)
            projected = done
        sink = []
        for hk in heads:
            row = jnp.zeros((1, gqa * QT_TILE), F32)
            for g in range(gqa):
                row = jnp.where(lgroup == g, sinks_ref[hk * gqa + g], row)
            sink.append(row)
        m = [jnp.maximum(jnp.maximum(jnp.max(sb[hk], axis=0, keepdims=True),
                                     jnp.max(sm[hk], axis=0, keepdims=True)), sink[hk]) for hk in heads]
        pb = [jnp.exp(sb[hk] - m[hk]) for hk in heads]
        pm = [jnp.exp(sm[hk] - m[hk]) for hk in heads]
        denom = [jnp.sum(pb[hk], axis=0, keepdims=True) + jnp.sum(pm[hk], axis=0, keepdims=True)
                 + jnp.exp(sink[hk] - m[hk]) for hk in heads]
        o = [(jnp.dot(vcols[hk * HEAD_DIM:(hk + 1) * HEAD_DIM], pb[hk].astype(BF16), preferred_element_type=F32)
              + jnp.dot(mvt_ref[hk * HEAD_DIM:(hk + 1) * HEAD_DIM, :], pm[hk].astype(BF16),
                        preferred_element_type=F32)) / denom[hk] for hk in heads]
        for hk in heads:
            for g in range(gqa):
                ot_ref[(hk * gqa + g) * HEAD_DIM:(hk * gqa + g + 1) * HEAD_DIM, c0:c0 + QT_TILE] = (
                    o[hk][:, g * QT_TILE:(g + 1) * QT_TILE].astype(BF16))
        done = (sub + 1) * QT_TILE
    project(projected, done)


def _attn_t(qt, kpad, vt, mkpad, mvt, sinks, h, w_o, layer, *, tq):
    nb, s, d = h.shape
    kvd = vt.shape[1]
    prev = WIN_CHUNKS * CHUNK
    ratio = tq // prev
    const = lambda shape: pl.BlockSpec(shape, lambda b, t: (0,) * len(shape))
    return pl.pallas_call(
        functools.partial(_attn_t_kernel, tq=tq),
        out_shape=jax.ShapeDtypeStruct((nb, s, d), F32),
        grid=(nb, s // tq),
        in_specs=[
            pl.BlockSpec(memory_space=pltpu.SMEM),
            pl.BlockSpec((1, d, tq), lambda b, t: (b, 0, t)),
            pl.BlockSpec((1, tq, N_KV_HEADS * KPAD), lambda b, t: (b, t, 0)),
            pl.BlockSpec((1, prev, N_KV_HEADS * KPAD), lambda b, t: (b, jnp.maximum(t * ratio - 1, 0), 0)),
            pl.BlockSpec((1, kvd, tq), lambda b, t: (b, 0, t)),
            pl.BlockSpec((1, kvd, prev), lambda b, t: (b, 0, jnp.maximum(t * ratio - 1, 0))),
            const((N_META, N_KV_HEADS * KPAD)), const((kvd, N_META)),
            pl.BlockSpec((1, tq, d), lambda b, t: (b, t, 0)),
            _layer_spec((d, d), layer),
        ],
        out_specs=pl.BlockSpec((1, tq, d), lambda b, t: (b, t, 0)),
        scratch_shapes=[pltpu.VMEM((d, tq), BF16)],
        compiler_params=_params("arbitrary", "arbitrary"),
        name="attn_t",
    )(sinks, qt, kpad, kpad, vt, vt, mkpad, mvt, h, w_o)


def _attn_small_kernel(sinks_ref, q_ref, k_ref, v_ref, h_ref, wo_ref, out_ref, o_ref):
    bb, s, d = q_ref.shape
    gqa = d // (N_KV_HEADS * HEAD_DIM)
    for b in range(bb):
        for hk in range(N_KV_HEADS):
            hs = slice(hk * HEAD_DIM, (hk + 1) * HEAD_DIM)
            qh = jnp.concatenate(
                [q_ref[b, :, (hk * gqa + g) * HEAD_DIM:(hk * gqa + g + 1) * HEAD_DIM] for g in range(gqa)],
                axis=0)
            sink = jnp.concatenate([jnp.full((s, 1), sinks_ref[hk * gqa + g], F32) for g in range(gqa)], axis=0)
            sc = _mm_nt(qh, k_ref[b, :, hs], True) * (HEAD_DIM ** -0.5)
            m = jnp.maximum(jnp.max(sc, axis=-1, keepdims=True), sink)
            p = jnp.exp(sc - m)
            denom = jnp.sum(p, axis=-1, keepdims=True) + jnp.exp(sink - m)
            o = _mm(p, v_ref[b, :, hs], True) / denom
            for g in range(gqa):
                o_ref[b * s:(b + 1) * s, (hk * gqa + g) * HEAD_DIM:(hk * gqa + g + 1) * HEAD_DIM] = (
                    o[g * s:(g + 1) * s])
    out_ref[...] = h_ref[...] + _mm(o_ref[...], wo_ref[0], True).reshape(bb, s, d)


def _attn_small(q, keys, vals, sinks, h, w_o, layer):
    nb, s, d = h.shape
    kn, kvd = keys.shape[1:]
    full = lambda shape: pl.BlockSpec(shape, lambda i: (0,) * len(shape))
    return pl.pallas_call(
        _attn_small_kernel,
        out_shape=jax.ShapeDtypeStruct((nb, s, d), F32),
        grid=(1,),
        in_specs=[pl.BlockSpec(memory_space=pltpu.SMEM), full((nb, s, d)), full((nb, kn, kvd)),
                  full((nb, kn, kvd)), full((nb, s, d)), _layer_spec((d, d), layer)],
        out_specs=full((nb, s, d)),
        scratch_shapes=[pltpu.VMEM((nb * s, d), F32)],
        compiler_params=_params("arbitrary"),
        name="attn_small",
    )(sinks, q, keys, vals, h, w_o)


ROUTE_ROWS = 32
MOE_CHUNK = 128
MOE_CALL_ROWS = (128, 144, 160)
SEG_ALIGN = 16


def _route_t(lt):
    n = lt.shape[1]
    row8 = lax.broadcasted_iota(jnp.int32, (SUBLANES, n), 0)
    lg = jnp.where(row8 < N_GROUPS, lt[0:SUBLANES], NEG_INF)
    gmax = jnp.max(lg, axis=0, keepdims=True)
    g_idx = jnp.min(jnp.where(lg == gmax, row8, SUBLANES), axis=0, keepdims=True)
    g_w = 1.0 / jnp.sum(jnp.exp(lg - gmax), axis=0, keepdims=True)
    le = lt[SUBLANES:SUBLANES + N_EXPERTS]
    row16 = lax.broadcasted_iota(jnp.int32, (N_EXPERTS, n), 0)
    in_group = (row16 // EXPERTS_PER_GROUP) == g_idx
    l1 = jnp.where(in_group, le, NEG_INF)
    m1 = jnp.max(l1, axis=0, keepdims=True)
    i1 = jnp.min(jnp.where(in_group & (l1 == m1), row16, N_EXPERTS), axis=0, keepdims=True)
    rest = in_group & (row16 != i1)
    l2 = jnp.where(rest, le, NEG_INF)
    m2 = jnp.max(l2, axis=0, keepdims=True)
    i2 = jnp.min(jnp.where(rest & (l2 == m2), row16, N_EXPERTS), axis=0, keepdims=True)
    e2 = jnp.exp(m2 - m1)
    p1 = 1.0 / (1.0 + e2)
    comb = g_w * (jnp.where(row16 == i1, p1, 0.0) + jnp.where(row16 == i2, e2 * p1, 0.0))
    c8 = comb[0:SUBLANES] + comb[SUBLANES:2 * SUBLANES]
    return g_idx, comb, c8 + pltpu.roll(c8, EXPERTS_PER_GROUP, 0)


def _route(logits):
    col = lax.broadcasted_iota(jnp.int32, logits.shape, 1)
    lg = jnp.where(col < N_GROUPS, logits, NEG_INF)
    gmax = jnp.max(lg, axis=-1, keepdims=True)
    g_idx = jnp.min(jnp.where(lg == gmax, col, LANES), axis=-1, keepdims=True)
    g_w = 1.0 / jnp.sum(jnp.exp(lg - gmax), axis=-1, keepdims=True)
    ecol = col - N_GROUPS
    in_group = (ecol >= 0) & (ecol < N_EXPERTS) & ((ecol // EXPERTS_PER_GROUP) == g_idx)
    l1 = jnp.where(in_group, logits, NEG_INF)
    m1 = jnp.max(l1, axis=-1, keepdims=True)
    i1 = jnp.min(jnp.where(in_group & (l1 == m1), col, LANES), axis=-1, keepdims=True)
    rest = in_group & (col != i1)
    l2 = jnp.where(rest, logits, NEG_INF)
    m2 = jnp.max(l2, axis=-1, keepdims=True)
    i2 = jnp.min(jnp.where(rest & (l2 == m2), col, LANES), axis=-1, keepdims=True)
    e2 = jnp.exp(m2 - m1)
    p1 = 1.0 / (1.0 + e2)
    return g_w * (jnp.where(col == i1, p1, 0.0) + jnp.where(col == i2, e2 * p1, 0.0))


def _router_logits_t(wr_ref, br_ref, hn_hi, hn_lo):
    a = _dot(wr_ref[0], hn_hi, (((1,), (1,)), ((), ())))
    b = _dot(wr_ref[0, 0:ROUTE_ROWS], hn_lo, (((1,), (1,)), ((), ())))
    return a[0:ROUTE_ROWS] + a[ROUTE_ROWS:2 * ROUTE_ROWS] + b + br_ref[0]


SMALL_EXPERTS_PER_STEP = 2


def _moe_small_kernel(h_ref, g_ref, wr_ref, br_ref, wg_ref, wu_ref, wd_ref, gf_ref,
                      out_ref, wgb_ref, wub_ref, wdb_ref, hn_ref, comb_ref, acc_ref, *, precise, final_norm):
    step = pl.program_id(0)
    dn = (((1,), (0,)), ((), ()))

    @pl.when(step == 0)
    def _():
        hn = _rms(h_ref[...], g_ref[...])
        hn_ref[0], hn_ref[1] = _split_bf16(hn, 2)
        comb_ref[...] = _route(_mm(hn, wr_ref[0], True) + br_ref[0])
        acc_ref[...] = jnp.zeros_like(acc_ref)

    def times(a_hi, a_lo, w):
        if not precise:
            w_hi = w.astype(BF16)
            return _dot(a_hi, w_hi, dn), w_hi
        w_hi, w_lo = _split_bf16(w, 2)
        return _dot(a_hi, w_hi, dn) + (_dot(a_hi, w_lo, dn) + _dot(a_lo, w_hi, dn)), w_hi

    js = range(SMALL_EXPERTS_PER_STEP)
    col = lax.broadcasted_iota(jnp.int32, comb_ref.shape, 1)
    c = [jnp.sum(jnp.where(col == step * SMALL_EXPERTS_PER_STEP + j + N_GROUPS, comb_ref[...], 0.0),
                 axis=-1, keepdims=True) for j in js]
    gates = [times(hn_ref[0], hn_ref[1], wg_ref[0, j]) for j in js]
    ups = [times(hn_ref[0], hn_ref[1], wu_ref[0, j]) for j in js]
    acts = [_split_bf16(gates[j][0] * jax.nn.sigmoid(gates[j][0]) * ups[j][0] * c[j], 2) for j in js]
    downs = [times(acts[j][0], acts[j][1], wd_ref[0, j]) for j in js]
    for j in js:
        wgb_ref[j], wub_ref[j], wdb_ref[j] = gates[j][1], ups[j][1], downs[j][1]
    acc_ref[...] += sum(downs[j][0] for j in js)

    @pl.when(step == pl.num_programs(0) - 1)
    def _():
        y = h_ref[...] + acc_ref[...]
        if final_norm:
            y = _rms(y, gf_ref[...])
        out_ref[...] = y


def _moe_small(h, g, wr, br, wg, wu, wd, layer, gf, *, precise, final_norm):
    n, d = h.shape
    _, ne, _, de = wg.shape
    k = SMALL_EXPERTS_PER_STEP
    const = lambda shape: pl.BlockSpec(shape, lambda e: (0,) * len(shape))
    return pl.pallas_call(
        functools.partial(_moe_small_kernel, precise=precise, final_norm=final_norm),
        out_shape=(jax.ShapeDtypeStruct((n, d), F32), jax.ShapeDtypeStruct((ne, d, de), BF16),
                   jax.ShapeDtypeStruct((ne, d, de), BF16), jax.ShapeDtypeStruct((ne, de, d), BF16)),
        grid=(ne // k,),
        in_specs=[
            const((n, d)), const((1, d)),
            _layer_spec((d, LANES), layer), _layer_spec((1, LANES), layer),
            pl.BlockSpec((1, k, d, de), lambda e: (layer, e, 0, 0)),
            pl.BlockSpec((1, k, d, de), lambda e: (layer, e, 0, 0)),
            pl.BlockSpec((1, k, de, d), lambda e: (layer, e, 0, 0)),
            const((1, d)),
        ],
        out_specs=(const((n, d)), pl.BlockSpec((k, d, de), lambda e: (e, 0, 0)),
                   pl.BlockSpec((k, d, de), lambda e: (e, 0, 0)), pl.BlockSpec((k, de, d), lambda e: (e, 0, 0))),
        scratch_shapes=[pltpu.VMEM((2, n, d), BF16), pltpu.VMEM((n, LANES), F32), pltpu.VMEM((n, d), F32)],
        compiler_params=_params("arbitrary"),
        name="moe_small",
    )(h, g, wr, br, wg, wu, wd, gf)


def _moe_sparse_kernel(h_ref, g_ref, wr_ref, br_ref, tri_ref, wg_ref, wu_ref, wd_ref, gf_ref, out_ref,
                       xs_ref, ys_ref, p_ref, cs_ref, *, tm, nsub, final_norm):
    rows = p_ref.shape[1]

    @pl.when(pl.program_id(0) == 0)
    def _():
        xs_ref[:, rows:, :] = jnp.zeros((nsub, xs_ref.shape[1] - rows, xs_ref.shape[2]), BF16)
        cs_ref[:, rows:, :] = jnp.zeros((nsub, cs_ref.shape[1] - rows, cs_ref.shape[2]), F32)

    def prep():
        tiles = range(nsub)
        row8 = lax.broadcasted_iota(jnp.int32, (SUBLANES, tm), 0)
        riota = lax.broadcasted_iota(jnp.int32, (rows, tm), 0)
        hn = [_rms(h_ref[t * tm:(t + 1) * tm, :], g_ref[...]) for t in tiles]
        hn_split = [_split_bf16(x, 2) for x in hn]
        logits = [_router_logits_t(wr_ref, br_ref, hi, lo) for hi, lo in hn_split]
        routed = [_route_t(lt) for lt in logits]
        onehot = [row8 == g_idx for g_idx, _, _ in routed]
        incl = [jnp.dot(jnp.where(oh, 1.0, 0.0).astype(BF16), tri_ref[...], preferred_element_type=F32)
                for oh in onehot]
        counts = [x[:, tm - 1:tm].astype(jnp.int32) for x in incl]
        plans = []
        for t in tiles:
            n = [counts[t][g, 0] for g in range(N_GROUPS)]
            starts = [jnp.int32(0)]
            for g in range(N_GROUPS - 1):
                starts.append(starts[-1] + (n[g] + SEG_ALIGN - 1) // SEG_ALIGN * SEG_ALIGN)
            plans.append((n, starts))
        for t in tiles:
            g_idx = routed[t][0]
            rank = jnp.sum(jnp.where(onehot[t], incl[t], 0.0), axis=0, keepdims=True).astype(jnp.int32) - 1
            start_tok = jnp.zeros_like(g_idx)
            for g in range(1, N_GROUPS):
                start_tok = jnp.where(g_idx == g, plans[t][1][g], start_tok)
            p_ref[t] = jnp.where(riota == start_tok + rank, 1.0, 0.0).astype(BF16)
        for t in tiles:
            xs_ref[t, 0:rows, :] = jnp.dot(p_ref[t], hn_split[t][0], preferred_element_type=F32).astype(BF16)
            comb_parts = jnp.concatenate(_split_bf16(routed[t][2], 3), axis=0)
            cs = _dot(p_ref[t], comb_parts, (((1,), (1,)), ((), ())))
            cs_ref[t, 0:rows, :] = (cs[:, 0:SUBLANES] + cs[:, SUBLANES:2 * SUBLANES]
                                    + cs[:, 2 * SUBLANES:3 * SUBLANES])
            ys_ref[t, 0:rows, :] = jnp.zeros((rows, ys_ref.shape[2]), BF16)
        return plans

    def experts(t, g, r0, m):
        x = xs_ref[t, pl.ds(r0, m), :]
        cc = cs_ref[t, pl.ds(r0, m), :]
        acts = []
        for j in range(EXPERTS_PER_GROUP):
            e = g * EXPERTS_PER_GROUP + j
            gate = jnp.dot(x, wg_ref[e], preferred_element_type=F32)
            up = jnp.dot(x, wu_ref[e], preferred_element_type=F32)
            acts.append((gate * jax.nn.sigmoid(gate) * up * cc[:, j:j + 1]).astype(BF16))
        y = jnp.dot(jnp.concatenate(acts, axis=1), wd_ref[g], preferred_element_type=F32)
        ys_ref[t, pl.ds(r0, m), :] = y.astype(BF16)

    plans = prep()

    for t, (n, starts) in enumerate(plans):
        for g in range(N_GROUPS):
            seg = pl.multiple_of(starts[g], SEG_ALIGN)
            lo = 0
            for m in MOE_CALL_ROWS:
                @pl.when((n[g] > lo) & (n[g] <= m))
                def _(t=t, g=g, seg=seg, m=m):
                    experts(t, g, seg, m)

                lo = m

            @pl.when(n[g] > MOE_CALL_ROWS[-1])
            def _(t=t, g=g, n=n, starts=starts):
                def chunk(c, carry):
                    experts(t, g, pl.multiple_of(starts[g] + c * MOE_CHUNK, SEG_ALIGN), MOE_CHUNK)
                    return carry

                lax.fori_loop(0, (n[g] + MOE_CHUNK - 1) // MOE_CHUNK, chunk, 0)

    for t in range(nsub):
        back = lax.dot_general(p_ref[t], ys_ref[t, 0:rows, :], (((0,), (0,)), ((), ())),
                               preferred_element_type=F32)
        y = h_ref[t * tm:(t + 1) * tm, :] + back
        if final_norm:
            y = _rms(y, gf_ref[...])
        out_ref[t * tm:(t + 1) * tm, :] = y


def _moe_sparse(h, g, wr2, brt, tri, wg, wu, wd4, layer, gf, *, tm, nsub, final_norm):
    n, d = h.shape
    ne, _, de = wg.shape
    rows = tm + N_GROUPS * SEG_ALIGN
    over = rows + MOE_CHUNK
    resident = lambda shape: pl.BlockSpec(shape, lambda i: (0,) * len(shape), pipeline_mode=pl.Buffered(1))
    const = lambda shape: pl.BlockSpec(shape, lambda i: (0,) * len(shape))
    return pl.pallas_call(
        functools.partial(_moe_sparse_kernel, tm=tm, nsub=nsub, final_norm=final_norm),
        out_shape=jax.ShapeDtypeStruct((n, d), F32),
        grid=(n // (tm * nsub),),
        in_specs=[
            pl.BlockSpec((tm * nsub, d), lambda i: (i, 0)), const((1, d)),
            _layer_spec((2 * ROUTE_ROWS, d), layer), _layer_spec((ROUTE_ROWS, 1), layer),
            const((tm, tm)),
            resident((ne, d, de)), resident((ne, d, de)), resident((N_GROUPS, EXPERTS_PER_GROUP * de, d)),
            const((1, d)),
        ],
        out_specs=pl.BlockSpec((tm * nsub, d), lambda i: (i, 0)),
        scratch_shapes=[pltpu.VMEM((nsub, over, d), BF16), pltpu.VMEM((nsub, over, d), BF16),
                        pltpu.VMEM((nsub, rows, tm), BF16), pltpu.VMEM((nsub, over, SUBLANES), F32)],
        compiler_params=pltpu.CompilerParams(dimension_semantics=("arbitrary",),
                                             vmem_limit_bytes=MOE_VMEM_LIMIT),
        name="moe_sparse",
    )(h, g, wr2, brt, tri, wg, wu, wd4, gf)


MOE_TILE = 512
WIDE_TILE = 1024


def _prompt_tiles(nb, seq):
    moe = min(MOE_TILE, seq)
    wide = WIDE_TILE if seq % WIDE_TILE == 0 else moe
    return moe, wide, 2 if (nb * seq) % (2 * moe) == 0 else 1


def kernel(x_prompt, x_sample, state_pool, cache_swa_kv, cache_meta_kv, state_conv, meta_tokens, norm_mix, norm_ffn, norm_final, w_pool, pool_scale, w_qkv, w_o, attn_sinks, w_conv_in, conv_w, w_conv_out, w_group, b_group, w_expert_router, b_expert_router, w_gate, w_up, w_down):
    nb, seq, d = x_prompt.shape
    db, dseq, _ = x_sample.shape
    depth = norm_mix.shape[0]
    kvd = N_KV_HEADS * HEAD_DIM
    tm_main, tq_main, moe_sub = _prompt_tiles(nb, seq)
    tm_pool = tq_main
    halo = POOL_HALO
    assert seq % tm_main == 0 and tm_main >= 2 * WINDOW and tm_main % LANES == 0, (seq, tm_main)
    assert dseq >= POOL_HALO and dseq % SUBLANES == 0 and N_META == POOL_HALO, (dseq, N_META)

    row = lambda a: a.reshape(1, -1).astype(F32)
    rpad = lambda a, k: jnp.pad(a, ((0, 0), (0, k)) + ((0, 0),) * (a.ndim - 2))
    wrt = jnp.concatenate([rpad(jnp.swapaxes(w_group, 1, 2), SUBLANES - N_GROUPS),
                           rpad(jnp.swapaxes(w_expert_router, 1, 2), ROUTE_ROWS - SUBLANES - N_EXPERTS)], axis=1)
    wr2 = jnp.concatenate(_split_bf16(wrt, 2), axis=1)
    brt = jnp.concatenate([rpad(b_group, SUBLANES - N_GROUPS),
                           rpad(b_expert_router, ROUTE_ROWS - SUBLANES - N_EXPERTS)], axis=1)[..., None].astype(F32)
    wr = jnp.pad(jnp.concatenate([w_group, w_expert_router], axis=-1).astype(F32),
                 ((0, 0), (0, 0), (0, LANES - N_GROUPS - N_EXPERTS)))
    br = jnp.pad(jnp.concatenate([b_group, b_expert_router], axis=-1).astype(F32),
                 ((0, 0), (0, LANES - N_GROUPS - N_EXPERTS)))[:, None, :]
    tri = jnp.triu(jnp.ones((tm_main, tm_main), BF16))
    bf = lambda a: a.astype(BF16)
    w_pool_b, w_o_b = bf(w_pool), bf(w_o)
    w_conv_in_b, w_conv_out_b = bf(w_conv_in), bf(w_conv_out)

    hm = meta_tokens.astype(F32)[None]
    hp = x_prompt
    hs = x_sample
    pool_p, swa_p, meta_p, conv_p, pool_s, swa_s, conv_s = [], [], [], [], [], [], []
    for i in range(depth):
        j = i // N_MIXERS
        g = row(norm_mix[i])
        if i % N_MIXERS == 0:
            sc = row(pool_scale[j])
            hm, st_m = _pool_mix(hm, jnp.zeros((1, halo, d), F32), g, w_pool, j, sc, bb=1, tm=N_META,
                                 has_history=False, precise=True)
            hp, st_p = _pool_mix(hp, st_m, g, w_pool_b, j, sc, bb=1, tm=tm_pool, has_history=True,
                                 precise=False)
            hist_s = jnp.pad(state_pool[j].astype(F32), ((0, 0), (1, 0), (0, 0)))
            hs, st_s = _pool_mix(hs, hist_s, g, w_pool, j, sc, bb=db, tm=dseq, has_history=True,
                                 precise=True)
            pool_p.append(st_p[:, 1:])
            pool_s.append(st_s[:, 1:])
        elif i % N_MIXERS == 1:
            sinks = attn_sinks[j].astype(F32)
            qm, km, vm = _qkv_small(hm, g, w_qkv, j, jnp.arange(N_META))
            qt, kpad, vt, kst, vst = _qkv_t(hp, g, w_qkv[j], N_META + jnp.arange(seq), tm=tq_main)
            qs, ks, vs = _qkv_small(hs, g, w_qkv, j, PAST_LEN + N_META + jnp.arange(dseq))
            hm = _attn_small(qm, km, vm, sinks, hm, w_o, j)
            mkpad = jnp.pad(km[0].reshape(N_META, N_KV_HEADS, HEAD_DIM),
                            ((0, 0), (0, 0), (0, KPAD - HEAD_DIM))).reshape(N_META, -1).astype(BF16)
            hp = _attn_t(qt, kpad, vt, mkpad, vm[0].T.astype(BF16), sinks, hp, w_o_b, j, tq=tq_main)
            flat = lambda a: a.reshape(a.shape[0], a.shape[1], kvd)
            keys = jnp.concatenate([flat(cache_meta_kv[j][:, :, 0]), flat(cache_swa_kv[j][:, :, 0]), ks], axis=1)
            vals = jnp.concatenate([flat(cache_meta_kv[j][:, :, 1]), flat(cache_swa_kv[j][:, :, 1]), vs], axis=1)
            hs = _attn_small(qs, keys, vals, sinks, hs, w_o, j)
            heads = lambda a: a.reshape(a.shape[0], a.shape[1], N_KV_HEADS, HEAD_DIM)
            swa_p.append(jnp.stack([heads(kst), heads(vst)], axis=2))
            meta_kv = jnp.stack([heads(km), heads(vm)], axis=2)
            meta_p.append(jnp.broadcast_to(meta_kv, (nb,) + meta_kv.shape[1:]))
            swa_s.append(jnp.stack([heads(ks), heads(vs)], axis=2))
        else:
            cw = conv_w[j].astype(F32)
            hm, st_m = _conv_mix(hm, jnp.zeros((1, SUBLANES, d), F32), g, w_conv_in, j, cw, w_conv_out,
                                 bb=1, tm=N_META, precise=True)
            hp, st_p = _conv_mix(hp, st_m, g, w_conv_in_b, j, cw, w_conv_out_b, bb=1, tm=tq_main,
                                 precise=False)
            hist_s = jnp.pad(state_conv[j].astype(F32), ((0, 0), (SUBLANES - (CONV_WIDTH - 1), 0), (0, 0)))
            hs, st_s = _conv_mix(hs, hist_s, g, w_conv_in, j, cw, w_conv_out, bb=db, tm=dseq, precise=True)
            conv_p.append(st_p[:, SUBLANES - (CONV_WIDTH - 1):])
            conv_s.append(st_s[:, SUBLANES - (CONV_WIDTH - 1):])

        final = i == depth - 1
        gf = row(norm_final)
        gn = row(norm_ffn[i])
        small = jnp.concatenate([hm.reshape(-1, d), hs.reshape(-1, d)], axis=0)
        precise = any(k % N_MIXERS != 0 for k in range(i + 1, depth))
        small, wg_b, wu_b, wd_b = _moe_small(small, gn, wr, br, w_gate, w_up, w_down, i, gf,
                                             precise=precise, final_norm=final)
        hm = small[:N_META].reshape(1, N_META, d)
        hs = small[N_META:].reshape(db, dseq, d)
        hp = _moe_sparse(hp.reshape(-1, d), gn, wr2, brt, tri, wg_b, wu_b, wd_b.reshape(N_GROUPS, -1, d), i, gf,
                         tm=tm_main, nsub=moe_sub, final_norm=final).reshape(nb, seq, d)

    return (hp, hs, jnp.stack(pool_p), jnp.stack(swa_p), jnp.stack(meta_p), jnp.stack(conv_p),
            jnp.stack(pool_s), jnp.stack(swa_s), jnp.stack(conv_s))
```

```python
import functools

import jax
import jax.numpy as jnp
from jax import lax
from jax.experimental import pallas as pl
from jax.experimental.pallas import tpu as pltpu

F32 = jnp.float32
BF16 = jnp.bfloat16

CHUNK = 64
N_META = 16
N_MIXERS = 3
POOL_WINDOWS = (2, 4, 8, 16)
POOL_STATE = max(POOL_WINDOWS) - 1
HEAD_DIM = 64
N_KV_HEADS = 4
WINDOW = 128
WIN_CHUNKS = WINDOW // CHUNK
ROPE_THETA = 10000.0
CONV_WIDTH = 3
N_GROUPS = 4
EXPERTS_PER_GROUP = 4
N_EXPERTS = N_GROUPS * EXPERTS_PER_GROUP
PAST_LEN = 2048
EPS = 1e-6

LANES = 128
SUBLANES = 8
VMEM_LIMIT = 48 * 1024 * 1024
MOE_VMEM_LIMIT = 60 * 1024 * 1024
NEG_INF = float("-inf")


def _params(*sem):
    return pltpu.CompilerParams(dimension_semantics=sem, vmem_limit_bytes=VMEM_LIMIT)


def _split_bf16(x, parts):
    out = []
    x = x.astype(F32)
    for _ in range(parts):
        hi = x.astype(BF16)
        out.append(hi)
        x = x - hi.astype(F32)
    return out


def _dot(a, b, dn):
    return lax.dot_general(a, b, dn, preferred_element_type=F32)


def _mm_dn(a, b, dn, precise):
    if not precise:
        return _dot(a.astype(BF16), b.astype(BF16), dn)
    a_hi, a_lo = _split_bf16(a, 2)
    b_hi, b_lo = _split_bf16(b, 2)
    return _dot(a_hi, b_hi, dn) + (_dot(a_hi, b_lo, dn) + _dot(a_lo, b_hi, dn))


def _mm(a, b, precise):
    return _mm_dn(a, b, (((1,), (0,)), ((), ())), precise)


def _mm_nt(a, b, precise):
    return _mm_dn(a, b, (((1,), (1,)), ((), ())), precise)


def _rms(x, g):
    ms = jnp.mean(x * x, axis=-1, keepdims=True)
    return x * lax.rsqrt(ms + EPS) * g


def _layer_spec(shape, layer, single=False):
    nd = len(shape)
    mode = dict(pipeline_mode=pl.Buffered(1)) if single else {}
    return pl.BlockSpec((1,) + tuple(shape), lambda *_: (layer,) + (0,) * nd, **mode)


POOL_HALO = 16
POOL_LEAD = 16


def _pool_kernel(h_ref, hist_ref, g_ref, w_ref, scale_ref, out_ref, state_ref, buf_ref, sa_ref, sb_ref, *,
                 tm, has_history, precise):
    t = pl.program_id(1)
    base = POOL_LEAD + POOL_HALO
    bb, _, d = h_ref.shape
    pg = d // len(POOL_WINDOWS)
    end = base + tm

    @pl.when(t == 0)
    def _():
        buf_ref[:, 0:POOL_LEAD, :] = jnp.zeros((bb, POOL_LEAD, d), F32)
        buf_ref[:, POOL_LEAD:base, :] = jnp.broadcast_to(hist_ref[...], (bb, POOL_HALO, d))

    h = h_ref[...]
    hn = _rms(h, g_ref[...])
    buf_ref[:, base:end, :] = hn
    if not has_history:
        pos = t * tm + lax.broadcasted_iota(jnp.int32, (1, tm, 1), 1)
    ys = []
    for gi, w in enumerate(POOL_WINDOWS):
        c0, c1 = gi * pg, (gi + 1) * pg
        src, dst, shift, lo = buf_ref, sa_ref, 1, SUBLANES
        while shift < w:
            last = 2 * shift == w
            lo_k = base if last else lo
            s = src[:, lo_k:end, c0:c1] + src[:, lo_k - shift:end - shift, c0:c1]
            if last:
                win = s
            else:
                dst[:, lo_k:end, c0:c1] = s
                src, dst = dst, (sb_ref if dst is sa_ref else sa_ref)
            shift, lo = 2 * shift, lo + SUBLANES
        if has_history:
            mean = win * (1.0 / w)
        else:
            mean = win / jnp.minimum(pos + 1, w).astype(F32)
        ys.append(_mm((mean - hn[:, :, c0:c1]).reshape(bb * tm, pg), w_ref[0, gi], precise))
    y = jnp.concatenate(ys, axis=1).reshape(bb, tm, d) * scale_ref[...]
    out_ref[...] = h + y
    tail = buf_ref[:, end - POOL_HALO:end, :]
    buf_ref[:, POOL_LEAD:base, :] = tail

    @pl.when(t == pl.num_programs(1) - 1)
    def _():
        state_ref[...] = tail


def _pool_mix(h, hist, g, w, layer, scale, *, bb, tm, has_history, precise):
    nb, s, d = h.shape
    halo = POOL_HALO
    pg = d // len(POOL_WINDOWS)
    hist_map = (lambda b, t: (b, 0, 0)) if hist.shape[0] == nb and nb > 1 else (lambda b, t: (0, 0, 0))
    hb = bb if hist.shape[0] == nb and nb > 1 else 1
    out, state = pl.pallas_call(
        functools.partial(_pool_kernel, tm=tm, has_history=has_history, precise=precise),
        out_shape=(jax.ShapeDtypeStruct((nb, s, d), F32), jax.ShapeDtypeStruct((nb, halo, d), F32)),
        grid=(nb // bb, s // tm),
        in_specs=[
            pl.BlockSpec((bb, tm, d), lambda b, t: (b, t, 0)),
            pl.BlockSpec((hb, halo, d), hist_map),
            pl.BlockSpec((1, d), lambda b, t: (0, 0)),
            _layer_spec((len(POOL_WINDOWS), pg, pg), layer),
            pl.BlockSpec((1, d), lambda b, t: (0, 0)),
        ],
        out_specs=(pl.BlockSpec((bb, tm, d), lambda b, t: (b, t, 0)),
                   pl.BlockSpec((bb, halo, d), lambda b, t: (b, 0, 0))),
        scratch_shapes=[pltpu.VMEM((bb, POOL_LEAD + halo + tm, d), F32)] * 3,
        compiler_params=_params("arbitrary", "arbitrary"),
        name="pool_mix",
    )(h, hist, g, w, scale)
    return out, state


def _conv_kernel(h_ref, hist_ref, g_ref, win_ref, cw_ref, wout_ref, out_ref, state_ref, buf_ref, *,
                 tm, precise):
    t = pl.program_id(1)
    bb, _, d = h_ref.shape

    @pl.when(t == 0)
    def _():
        buf_ref[:, 0:SUBLANES, :] = jnp.broadcast_to(hist_ref[...], (bb, SUBLANES, d))

    h = h_ref[...]
    hn = _rms(h, g_ref[...])
    z = _mm(hn.reshape(bb * tm, d), win_ref[0], precise)
    gate_b = z[:, 0:d]
    buf_ref[:, SUBLANES:SUBLANES + tm, :] = (z[:, d:2 * d] * z[:, 2 * d:3 * d]).reshape(bb, tm, d)
    first = SUBLANES - (CONV_WIDTH - 1)
    acc = buf_ref[:, first:first + tm, :] * cw_ref[0:1, :]
    for k in range(1, CONV_WIDTH):
        acc = acc + buf_ref[:, first + k:first + k + tm, :] * cw_ref[k:k + 1, :]
    y = _mm(gate_b * acc.reshape(bb * tm, d), wout_ref[0], precise)
    out_ref[...] = h + y.reshape(bb, tm, d)
    tail = buf_ref[:, tm:tm + SUBLANES, :]
    buf_ref[:, 0:SUBLANES, :] = tail

    @pl.when(t == pl.num_programs(1) - 1)
    def _():
        state_ref[...] = tail


def _conv_mix(h, hist, g, w_in, layer, cw, w_out, *, bb, tm, precise):
    nb, s, d = h.shape
    per_batch = hist.shape[0] == nb and nb > 1
    hist_map = (lambda b, t: (b, 0, 0)) if per_batch else (lambda b, t: (0, 0, 0))
    out, state = pl.pallas_call(
        functools.partial(_conv_kernel, tm=tm, precise=precise),
        out_shape=(jax.ShapeDtypeStruct((nb, s, d), F32), jax.ShapeDtypeStruct((nb, SUBLANES, d), F32)),
        grid=(nb // bb, s // tm),
        in_specs=[
            pl.BlockSpec((bb, tm, d), lambda b, t: (b, t, 0)),
            pl.BlockSpec((bb if per_batch else 1, SUBLANES, d), hist_map),
            pl.BlockSpec((1, d), lambda b, t: (0, 0)),
            _layer_spec((d, 3 * d), layer, single=True),
            pl.BlockSpec((CONV_WIDTH, d), lambda b, t: (0, 0)),
            _layer_spec((d, d), layer, single=True),
        ],
        out_specs=(pl.BlockSpec((bb, tm, d), lambda b, t: (b, t, 0)),
                   pl.BlockSpec((bb, SUBLANES, d), lambda b, t: (b, 0, 0))),
        scratch_shapes=[pltpu.VMEM((bb, tm + SUBLANES, d), F32)],
        compiler_params=_params("arbitrary", "arbitrary"),
        name="conv_mix",
    )(h, hist, g, w_in, cw, w_out)
    return out, state


def _rope_tables(pos):
    half = HEAD_DIM // 2
    inv = ROPE_THETA ** (-jnp.arange(half, dtype=F32) / half)
    ang = pos.astype(F32)[:, None] * inv[None, :]
    cos, sin = jnp.cos(ang), jnp.sin(ang)
    reps = LANES // HEAD_DIM
    return (jnp.tile(jnp.concatenate([cos, cos], axis=1), (1, reps)),
            jnp.tile(jnp.concatenate([-sin, sin], axis=1), (1, reps)))


def _rope_block(blk, cos, sin):
    half = HEAD_DIM // 2
    lane = lax.broadcasted_iota(jnp.int32, (1, LANES), 1)
    partner = jnp.where((lane % HEAD_DIM) < half, pltpu.roll(blk, LANES - half, 1), pltpu.roll(blk, half, 1))
    return blk * cos + partner * sin


def _qkv_small_kernel(h_ref, g_ref, w_ref, cos_ref, sin_ref, q_ref, k_ref, v_ref):
    bb, s, d = h_ref.shape
    kvd = k_ref.shape[-1]
    hn = _rms(h_ref[...], g_ref[...]).reshape(bb * s, d)
    z = _mm(hn, w_ref[0], True)
    cos, sin = cos_ref[...], sin_ref[...]
    q = [_rope_block(z[:, j * LANES:(j + 1) * LANES], cos, sin) for j in range(d // LANES)]
    k = [_rope_block(z[:, d + j * LANES:d + (j + 1) * LANES], cos, sin) for j in range(kvd // LANES)]
    q_ref[...] = jnp.concatenate(q, axis=1).reshape(bb, s, d)
    k_ref[...] = jnp.concatenate(k, axis=1).reshape(bb, s, kvd)
    v_ref[...] = z[:, d + kvd:d + 2 * kvd].reshape(bb, s, kvd)


def _qkv_small(h, g, w_qkv, layer, pos):
    nb, s, d = h.shape
    kvd = N_KV_HEADS * HEAD_DIM
    cos, sin = _rope_tables(pos)
    cos, sin = jnp.tile(cos, (nb, 1)), jnp.tile(sin, (nb, 1))
    full = lambda shape: pl.BlockSpec(shape, lambda i: (0,) * len(shape))
    return pl.pallas_call(
        _qkv_small_kernel,
        out_shape=(jax.ShapeDtypeStruct((nb, s, d), F32), jax.ShapeDtypeStruct((nb, s, kvd), F32),
                   jax.ShapeDtypeStruct((nb, s, kvd), F32)),
        grid=(1,),
        in_specs=[full((nb, s, d)), full((1, d)), _layer_spec((d, d + 2 * kvd), layer),
                  full((nb * s, LANES)), full((nb * s, LANES))],
        out_specs=(full((nb, s, d)), full((nb, s, kvd)), full((nb, s, kvd))),
        compiler_params=_params("arbitrary"),
        name="qkv_small",
    )(h, g, w_qkv, cos, sin)


QT_TILE = 128
KPAD = LANES
ONES_ROWS = 16


def _qkv_t_kernel(h_ref, g_ref, wqt_ref, wk_ref, wvt_ref, wv_ref, cos_ref, sin_ref, cost_ref, sint_ref,
                  qt_ref, kpad_ref, vt_ref, kst_ref, vst_ref):
    t = pl.program_id(1)
    tm = h_ref.shape[1]
    half = HEAD_DIM // 2
    lane = lax.broadcasted_iota(jnp.int32, (1, LANES), 1)
    nparts = 4 if tm % (4 * WINDOW) == 0 else 1
    tp = tm // nparts
    parts = [slice(i * tp, (i + 1) * tp) for i in range(nparts)]
    hb = [_rms(h_ref[0, p, :], g_ref[...]).astype(BF16) for p in parts]
    zq = [_mm_nt(wqt_ref[...], x, False) for x in hb]
    zk = [jnp.dot(x, wk_ref[...], preferred_element_type=F32) for x in hb]
    zv = [_mm_nt(wvt_ref[...], x, False) for x in hb]
    kr = []
    for i, p in enumerate(parts):
        cost, sint = cost_ref[:, p], sint_ref[:, p]
        for hd in range(zq[i].shape[0] // HEAD_DIM):
            x1 = zq[i][hd * HEAD_DIM:hd * HEAD_DIM + half]
            x2 = zq[i][hd * HEAD_DIM + half:(hd + 1) * HEAD_DIM]
            qt_ref[0, hd * HEAD_DIM:hd * HEAD_DIM + half, p] = (x1 * cost - x2 * sint).astype(BF16)
            qt_ref[0, hd * HEAD_DIM + half:(hd + 1) * HEAD_DIM, p] = (x2 * cost + x1 * sint).astype(BF16)
        vt_ref[0, :, p] = zv[i].astype(BF16)
        blks = []
        for j in range(zk[i].shape[1] // LANES):
            blk = _rope_block(zk[i][:, j * LANES:(j + 1) * LANES], cos_ref[p, :], sin_ref[p, :])
            blks.append(blk)
            for sub in range(LANES // HEAD_DIM):
                hk = j * (LANES // HEAD_DIM) + sub
                shifted = blk if sub == 0 else pltpu.roll(blk, LANES - sub * HEAD_DIM, 1)
                kpad_ref[0, p, hk * KPAD:(hk + 1) * KPAD] = jnp.where(lane < HEAD_DIM, shifted, 0.0).astype(BF16)
        kr.append(jnp.concatenate(blks, axis=1))

    @pl.when(t == pl.num_programs(1) - 1)
    def _():
        kst_ref[0] = jnp.concatenate(kr, axis=0)[tm - WINDOW:]
        vst_ref[0] = jnp.dot(hb[-1][tp - WINDOW:], wv_ref[...], preferred_element_type=F32)


def _qkv_t(h, g, w_qkv, pos, *, tm):
    nb, s, d = h.shape
    kvd = N_KV_HEADS * HEAD_DIM
    half = HEAD_DIM // 2
    cos, sin = _rope_tables(pos)
    inv = ROPE_THETA ** (-jnp.arange(half, dtype=F32) / half)
    ang = inv[:, None] * pos.astype(F32)[None, :]
    cost, sint = jnp.cos(ang), jnp.sin(ang)
    wq_t = (w_qkv[:, :d] * (HEAD_DIM ** -0.5)).T.astype(BF16)
    wk = w_qkv[:, d:d + kvd].astype(BF16)
    wv = w_qkv[:, d + kvd:].astype(BF16)
    const = lambda shape: pl.BlockSpec(shape, lambda b, t: (0,) * len(shape))
    return pl.pallas_call(
        _qkv_t_kernel,
        out_shape=(jax.ShapeDtypeStruct((nb, d, s), BF16),
                   jax.ShapeDtypeStruct((nb, s, N_KV_HEADS * KPAD), BF16),
                   jax.ShapeDtypeStruct((nb, kvd, s), BF16),
                   jax.ShapeDtypeStruct((nb, WINDOW, kvd), F32),
                   jax.ShapeDtypeStruct((nb, WINDOW, kvd), F32)),
        grid=(nb, s // tm),
        in_specs=[
            pl.BlockSpec((1, tm, d), lambda b, t: (b, t, 0)),
            const((1, d)), const((d, d)), const((d, kvd)), const((kvd, d)), const((d, kvd)),
            pl.BlockSpec((tm, LANES), lambda b, t: (t, 0)),
            pl.BlockSpec((tm, LANES), lambda b, t: (t, 0)),
            pl.BlockSpec((half, tm), lambda b, t: (0, t)),
            pl.BlockSpec((half, tm), lambda b, t: (0, t)),
        ],
        out_specs=(pl.BlockSpec((1, d, tm), lambda b, t: (b, 0, t)),
                   pl.BlockSpec((1, tm, N_KV_HEADS * KPAD), lambda b, t: (b, t, 0)),
                   pl.BlockSpec((1, kvd, tm), lambda b, t: (b, 0, t)),
                   pl.BlockSpec((1, WINDOW, kvd), lambda b, t: (b, 0, 0)),
                   pl.BlockSpec((1, WINDOW, kvd), lambda b, t: (b, 0, 0))),
        compiler_params=_params("arbitrary", "arbitrary"),
        name="qkv_t",
    )(h, g, wq_t, wk, wv.T, wv, cos, sin, cost, sint)


def _attn_t_kernel(sinks_ref, qt_ref, kc_ref, kp_ref, vc_ref, vp_ref, mk_ref, mvt_ref, h_ref, wo_ref,
                   out_ref, ot_ref, *, tq):
    t = pl.program_id(1)
    gqa = qt_ref.shape[1] // (N_KV_HEADS * HEAD_DIM)
    band = QT_TILE + WIN_CHUNKS * CHUNK
    kk = jnp.concatenate([kp_ref[0], kc_ref[0]], axis=0)
    vv = jnp.concatenate([vp_ref[0], vc_ref[0]], axis=1)
    r = lax.broadcasted_iota(jnp.int32, (band, gqa * QT_TILE), 0)
    ln = lax.broadcasted_iota(jnp.int32, (band, gqa * QT_TILE), 1)
    kchunk = r // CHUNK
    qchunk = (ln // CHUNK) % (QT_TILE // CHUNK)
    visible = (kchunk >= qchunk) & (kchunk <= qchunk + WIN_CHUNKS)
    lgroup = lax.broadcasted_iota(jnp.int32, (1, gqa * QT_TILE), 1) // QT_TILE

    def project(p0, p1):
        proj = lax.dot_general(ot_ref[:, p0:p1], wo_ref[0], (((0,), (0,)), ((), ())), preferred_element_type=F32)
        out_ref[0, p0:p1, :] = h_ref[0, p0:p1, :] + proj

    projected = done = 0
    for sub in range(tq // QT_TILE):
        c0 = sub * QT_TILE
        if sub == 0:
            mask = visible & ((t > 0) | (r >= WIN_CHUNKS * CHUNK))
        else:
            mask = visible
        krows = kk[c0:c0 + band]
        vcols = vv[:, c0:c0 + band]
        heads = range(N_KV_HEADS)
        q4 = [jnp.concatenate(
            [qt_ref[0, (hk * gqa + g) * HEAD_DIM:(hk * gqa + g + 1) * HEAD_DIM, c0:c0 + QT_TILE]
             for g in range(gqa)], axis=1) for hk in heads]
        sb = [jnp.dot(krows[:, hk * KPAD:hk * KPAD + HEAD_DIM], q4[hk], preferred_element_type=F32)
              for hk in heads]
        sm = [jnp.dot(mk_ref[:, hk * KPAD:hk * KPAD + HEAD_DIM], q4[hk], preferred_element_type=F32)
              for hk in heads]
        sb = [jnp.where(mask, x, NEG_INF) for x in sb]
        if done - projected >= QT_TILE:
            project(projected, done)
            projected = done
        sink = []
        for hk in heads:
            row = jnp.zeros((1, gqa * QT_TILE), F32)
            for g in range(gqa):
                row = jnp.where(lgroup == g, sinks_ref[hk * gqa + g], row)
            sink.append(row)
        m = [jnp.maximum(jnp.maximum(jnp.max(sb[hk], axis=0, keepdims=True),
                                     jnp.max(sm[hk], axis=0, keepdims=True)), sink[hk]) for hk in heads]
        pb = [jnp.exp(sb[hk] - m[hk]) for hk in heads]
        pm = [jnp.exp(sm[hk] - m[hk]) for hk in heads]
        ones_b = jnp.ones((ONES_ROWS, band), BF16)
        ones_m = jnp.ones((ONES_ROWS, N_META), BF16)
        ox = [jnp.dot(jnp.concatenate([vcols[hk * HEAD_DIM:(hk + 1) * HEAD_DIM], ones_b], axis=0),
                      pb[hk].astype(BF16), preferred_element_type=F32)
              + jnp.dot(jnp.concatenate([mvt_ref[hk * HEAD_DIM:(hk + 1) * HEAD_DIM, :], ones_m], axis=0),
                        pm[hk].astype(BF16), preferred_element_type=F32) for hk in heads]
        o = [ox[hk][0:HEAD_DIM] / (ox[hk][HEAD_DIM:HEAD_DIM + 1] + jnp.exp(sink[hk] - m[hk])) for hk in heads]
        for hk in heads:
            for g in range(gqa):
                ot_ref[(hk * gqa + g) * HEAD_DIM:(hk * gqa + g + 1) * HEAD_DIM, c0:c0 + QT_TILE] = (
                    o[hk][:, g * QT_TILE:(g + 1) * QT_TILE].astype(BF16))
        done = (sub + 1) * QT_TILE
    project(projected, done)


def _attn_t(qt, kpad, vt, mkpad, mvt, sinks, h, w_o, layer, *, tq):
    nb, s, d = h.shape
    kvd = vt.shape[1]
    prev = WIN_CHUNKS * CHUNK
    ratio = tq // prev
    const = lambda shape: pl.BlockSpec(shape, lambda b, t: (0,) * len(shape))
    return pl.pallas_call(
        functools.partial(_attn_t_kernel, tq=tq),
        out_shape=jax.ShapeDtypeStruct((nb, s, d), F32),
        grid=(nb, s // tq),
        in_specs=[
            pl.BlockSpec(memory_space=pltpu.SMEM),
            pl.BlockSpec((1, d, tq), lambda b, t: (b, 0, t)),
            pl.BlockSpec((1, tq, N_KV_HEADS * KPAD), lambda b, t: (b, t, 0)),
            pl.BlockSpec((1, prev, N_KV_HEADS * KPAD), lambda b, t: (b, jnp.maximum(t * ratio - 1, 0), 0)),
            pl.BlockSpec((1, kvd, tq), lambda b, t: (b, 0, t)),
            pl.BlockSpec((1, kvd, prev), lambda b, t: (b, 0, jnp.maximum(t * ratio - 1, 0))),
            const((N_META, N_KV_HEADS * KPAD)), const((kvd, N_META)),
            pl.BlockSpec((1, tq, d), lambda b, t: (b, t, 0)),
            _layer_spec((d, d), layer),
        ],
        out_specs=pl.BlockSpec((1, tq, d), lambda b, t: (b, t, 0)),
        scratch_shapes=[pltpu.VMEM((d, tq), BF16)],
        compiler_params=_params("arbitrary", "arbitrary"),
        name="attn_t",
    )(sinks, qt, kpad, kpad, vt, vt, mkpad, mvt, h, w_o)


def _attn_small_kernel(sinks_ref, q_ref, k_ref, v_ref, h_ref, wo_ref, out_ref, o_ref):
    bb, s, d = q_ref.shape
    gqa = d // (N_KV_HEADS * HEAD_DIM)
    for b in range(bb):
        for hk in range(N_KV_HEADS):
            hs = slice(hk * HEAD_DIM, (hk + 1) * HEAD_DIM)
            qh = jnp.concatenate(
                [q_ref[b, :, (hk * gqa + g) * HEAD_DIM:(hk * gqa + g + 1) * HEAD_DIM] for g in range(gqa)],
                axis=0)
            sink = jnp.concatenate([jnp.full((s, 1), sinks_ref[hk * gqa + g], F32) for g in range(gqa)], axis=0)
            sc = _mm_nt(qh, k_ref[b, :, hs], True) * (HEAD_DIM ** -0.5)
            m = jnp.maximum(jnp.max(sc, axis=-1, keepdims=True), sink)
            p = jnp.exp(sc - m)
            denom = jnp.sum(p, axis=-1, keepdims=True) + jnp.exp(sink - m)
            o = _mm(p, v_ref[b, :, hs], True) / denom
            for g in range(gqa):
                o_ref[b * s:(b + 1) * s, (hk * gqa + g) * HEAD_DIM:(hk * gqa + g + 1) * HEAD_DIM] = (
                    o[g * s:(g + 1) * s])
    out_ref[...] = h_ref[...] + _mm(o_ref[...], wo_ref[0], True).reshape(bb, s, d)


def _attn_small(q, keys, vals, sinks, h, w_o, layer):
    nb, s, d = h.shape
    kn, kvd = keys.shape[1:]
    full = lambda shape: pl.BlockSpec(shape, lambda i: (0,) * len(shape))
    return pl.pallas_call(
        _attn_small_kernel,
        out_shape=jax.ShapeDtypeStruct((nb, s, d), F32),
        grid=(1,),
        in_specs=[pl.BlockSpec(memory_space=pltpu.SMEM), full((nb, s, d)), full((nb, kn, kvd)),
                  full((nb, kn, kvd)), full((nb, s, d)), _layer_spec((d, d), layer)],
        out_specs=full((nb, s, d)),
        scratch_shapes=[pltpu.VMEM((nb * s, d), F32)],
        compiler_params=_params("arbitrary"),
        name="attn_small",
    )(sinks, q, keys, vals, h, w_o)


ROUTE_ROWS = 32
MOE_CHUNK = 128
MOE_CALL_ROWS = (128, 144, 160)
SEG_ALIGN = 16


def _route_t(lt):
    n = lt.shape[1]
    row8 = lax.broadcasted_iota(jnp.int32, (SUBLANES, n), 0)
    lg = jnp.where(row8 < N_GROUPS, lt[0:SUBLANES], NEG_INF)
    gmax = jnp.max(lg, axis=0, keepdims=True)
    g_idx = jnp.min(jnp.where(lg == gmax, row8, SUBLANES), axis=0, keepdims=True)
    g_w = 1.0 / jnp.sum(jnp.exp(lg - gmax), axis=0, keepdims=True)
    le = lt[SUBLANES:SUBLANES + N_EXPERTS]
    row16 = lax.broadcasted_iota(jnp.int32, (N_EXPERTS, n), 0)
    in_group = (row16 // EXPERTS_PER_GROUP) == g_idx
    l1 = jnp.where(in_group, le, NEG_INF)
    m1 = jnp.max(l1, axis=0, keepdims=True)
    i1 = jnp.min(jnp.where(in_group & (l1 == m1), row16, N_EXPERTS), axis=0, keepdims=True)
    rest = in_group & (row16 != i1)
    l2 = jnp.where(rest, le, NEG_INF)
    m2 = jnp.max(l2, axis=0, keepdims=True)
    i2 = jnp.min(jnp.where(rest & (l2 == m2), row16, N_EXPERTS), axis=0, keepdims=True)
    e2 = jnp.exp(m2 - m1)
    p1 = 1.0 / (1.0 + e2)
    comb = g_w * (jnp.where(row16 == i1, p1, 0.0) + jnp.where(row16 == i2, e2 * p1, 0.0))
    c8 = comb[0:SUBLANES] + comb[SUBLANES:2 * SUBLANES]
    return g_idx, comb, c8 + pltpu.roll(c8, EXPERTS_PER_GROUP, 0)


def _route(logits):
    col = lax.broadcasted_iota(jnp.int32, logits.shape, 1)
    lg = jnp.where(col < N_GROUPS, logits, NEG_INF)
    gmax = jnp.max(lg, axis=-1, keepdims=True)
    g_idx = jnp.min(jnp.where(lg == gmax, col, LANES), axis=-1, keepdims=True)
    g_w = 1.0 / jnp.sum(jnp.exp(lg - gmax), axis=-1, keepdims=True)
    ecol = col - N_GROUPS
    in_group = (ecol >= 0) & (ecol < N_EXPERTS) & ((ecol // EXPERTS_PER_GROUP) == g_idx)
    l1 = jnp.where(in_group, logits, NEG_INF)
    m1 = jnp.max(l1, axis=-1, keepdims=True)
    i1 = jnp.min(jnp.where(in_group & (l1 == m1), col, LANES), axis=-1, keepdims=True)
    rest = in_group & (col != i1)
    l2 = jnp.where(rest, logits, NEG_INF)
    m2 = jnp.max(l2, axis=-1, keepdims=True)
    i2 = jnp.min(jnp.where(rest & (l2 == m2), col, LANES), axis=-1, keepdims=True)
    e2 = jnp.exp(m2 - m1)
    p1 = 1.0 / (1.0 + e2)
    return g_w * (jnp.where(col == i1, p1, 0.0) + jnp.where(col == i2, e2 * p1, 0.0))


def _router_logits_t(wr_ref, br_ref, hn_hi, hn_lo):
    a = _dot(wr_ref[0], hn_hi, (((1,), (1,)), ((), ())))
    b = _dot(wr_ref[0, 0:ROUTE_ROWS], hn_lo, (((1,), (1,)), ((), ())))
    return a[0:ROUTE_ROWS] + a[ROUTE_ROWS:2 * ROUTE_ROWS] + b + br_ref[0]


SMALL_EXPERTS_PER_STEP = 2


def _moe_small_kernel(h_ref, g_ref, wr_ref, br_ref, wg_ref, wu_ref, wd_ref, gf_ref,
                      out_ref, wgb_ref, wub_ref, wdb_ref, hn_ref, comb_ref, acc_ref, *, precise, final_norm):
    step = pl.program_id(0)
    dn = (((1,), (0,)), ((), ()))

    @pl.when(step == 0)
    def _():
        hn = _rms(h_ref[...], g_ref[...])
        hn_ref[0], hn_ref[1] = _split_bf16(hn, 2)
        comb_ref[...] = _route(_mm(hn, wr_ref[0], True) + br_ref[0])
        acc_ref[...] = jnp.zeros_like(acc_ref)

    def times(a_hi, a_lo, w):
        if not precise:
            w_hi = w.astype(BF16)
            return _dot(a_hi, w_hi, dn), w_hi
        w_hi, w_lo = _split_bf16(w, 2)
        return _dot(a_hi, w_hi, dn) + (_dot(a_hi, w_lo, dn) + _dot(a_lo, w_hi, dn)), w_hi

    js = range(SMALL_EXPERTS_PER_STEP)
    col = lax.broadcasted_iota(jnp.int32, comb_ref.shape, 1)
    c = [jnp.sum(jnp.where(col == step * SMALL_EXPERTS_PER_STEP + j + N_GROUPS, comb_ref[...], 0.0),
                 axis=-1, keepdims=True) for j in js]
    gates = [times(hn_ref[0], hn_ref[1], wg_ref[0, j]) for j in js]
    ups = [times(hn_ref[0], hn_ref[1], wu_ref[0, j]) for j in js]
    acts = [_split_bf16(gates[j][0] * jax.nn.sigmoid(gates[j][0]) * ups[j][0] * c[j], 2) for j in js]
    downs = [times(acts[j][0], acts[j][1], wd_ref[0, j]) for j in js]
    for j in js:
        wgb_ref[j], wub_ref[j], wdb_ref[j] = gates[j][1], ups[j][1], downs[j][1]
    acc_ref[...] += sum(downs[j][0] for j in js)

    @pl.when(step == pl.num_programs(0) - 1)
    def _():
        y = h_ref[...] + acc_ref[...]
        if final_norm:
            y = _rms(y, gf_ref[...])
        out_ref[...] = y


def _moe_small(h, g, wr, br, wg, wu, wd, layer, gf, *, precise, final_norm):
    n, d = h.shape
    _, ne, _, de = wg.shape
    k = SMALL_EXPERTS_PER_STEP
    const = lambda shape: pl.BlockSpec(shape, lambda e: (0,) * len(shape))
    return pl.pallas_call(
        functools.partial(_moe_small_kernel, precise=precise, final_norm=final_norm),
        out_shape=(jax.ShapeDtypeStruct((n, d), F32), jax.ShapeDtypeStruct((ne, d, de), BF16),
                   jax.ShapeDtypeStruct((ne, d, de), BF16), jax.ShapeDtypeStruct((ne, de, d), BF16)),
        grid=(ne // k,),
        in_specs=[
            const((n, d)), const((1, d)),
            _layer_spec((d, LANES), layer), _layer_spec((1, LANES), layer),
            pl.BlockSpec((1, k, d, de), lambda e: (layer, e, 0, 0)),
            pl.BlockSpec((1, k, d, de), lambda e: (layer, e, 0, 0)),
            pl.BlockSpec((1, k, de, d), lambda e: (layer, e, 0, 0)),
            const((1, d)),
        ],
        out_specs=(const((n, d)), pl.BlockSpec((k, d, de), lambda e: (e, 0, 0)),
                   pl.BlockSpec((k, d, de), lambda e: (e, 0, 0)), pl.BlockSpec((k, de, d), lambda e: (e, 0, 0))),
        scratch_shapes=[pltpu.VMEM((2, n, d), BF16), pltpu.VMEM((n, LANES), F32), pltpu.VMEM((n, d), F32)],
        compiler_params=_params("arbitrary"),
        name="moe_small",
    )(h, g, wr, br, wg, wu, wd, gf)


def _moe_sparse_kernel(h_ref, g_ref, wr_ref, br_ref, tri_ref, wg_ref, wu_ref, wd_ref, gf_ref, out_ref,
                       xs_ref, ys_ref, p_ref, cs_ref, *, tm, nsub, final_norm):
    rows = p_ref.shape[1]

    @pl.when(pl.program_id(0) == 0)
    def _():
        xs_ref[:, rows:, :] = jnp.zeros((nsub, xs_ref.shape[1] - rows, xs_ref.shape[2]), BF16)
        cs_ref[:, rows:, :] = jnp.zeros((nsub, cs_ref.shape[1] - rows, cs_ref.shape[2]), F32)

    def prep():
        tiles = range(nsub)
        row8 = lax.broadcasted_iota(jnp.int32, (SUBLANES, tm), 0)
        riota = lax.broadcasted_iota(jnp.int32, (rows, tm), 0)
        hn = [_rms(h_ref[t * tm:(t + 1) * tm, :], g_ref[...]) for t in tiles]
        hn_split = [_split_bf16(x, 2) for x in hn]
        logits = [_router_logits_t(wr_ref, br_ref, hi, lo) for hi, lo in hn_split]
        routed = [_route_t(lt) for lt in logits]
        onehot = [row8 == g_idx for g_idx, _, _ in routed]
        incl = [jnp.dot(jnp.where(oh, 1.0, 0.0).astype(BF16), tri_ref[...], preferred_element_type=F32)
                for oh in onehot]
        counts = [x[:, tm - 1:tm].astype(jnp.int32) for x in incl]
        plans = []
        for t in tiles:
            n = [counts[t][g, 0] for g in range(N_GROUPS)]
            starts = [jnp.int32(0)]
            for g in range(N_GROUPS - 1):
                starts.append(starts[-1] + (n[g] + SEG_ALIGN - 1) // SEG_ALIGN * SEG_ALIGN)
            plans.append((n, starts))
        for t in tiles:
            g_idx = routed[t][0]
            rank = jnp.sum(jnp.where(onehot[t], incl[t], 0.0), axis=0, keepdims=True).astype(jnp.int32) - 1
            start_tok = jnp.zeros_like(g_idx)
            for g in range(1, N_GROUPS):
                start_tok = jnp.where(g_idx == g, plans[t][1][g], start_tok)
            p_ref[t] = jnp.where(riota == start_tok + rank, 1.0, 0.0).astype(BF16)
        for t in tiles:
            xs_ref[t, 0:rows, :] = jnp.dot(p_ref[t], hn_split[t][0], preferred_element_type=F32).astype(BF16)
            comb_parts = jnp.concatenate(_split_bf16(routed[t][2], 3), axis=0)
            cs = _dot(p_ref[t], comb_parts, (((1,), (1,)), ((), ())))
            cs_ref[t, 0:rows, :] = (cs[:, 0:SUBLANES] + cs[:, SUBLANES:2 * SUBLANES]
                                    + cs[:, 2 * SUBLANES:3 * SUBLANES])
            ys_ref[t, 0:rows, :] = jnp.zeros((rows, ys_ref.shape[2]), BF16)
        return plans

    def experts(t, g, r0, m):
        x = xs_ref[t, pl.ds(r0, m), :]
        cc = cs_ref[t, pl.ds(r0, m), :]
        acts = []
        for j in range(EXPERTS_PER_GROUP):
            e = g * EXPERTS_PER_GROUP + j
            gate = jnp.dot(x, wg_ref[e], preferred_element_type=F32)
            up = jnp.dot(x, wu_ref[e], preferred_element_type=F32)
            acts.append((gate * jax.nn.sigmoid(gate) * up * cc[:, j:j + 1]).astype(BF16))
        y = jnp.dot(jnp.concatenate(acts, axis=1), wd_ref[g], preferred_element_type=F32)
        ys_ref[t, pl.ds(r0, m), :] = y.astype(BF16)

    plans = prep()

    for t, (n, starts) in enumerate(plans):
        for g in range(N_GROUPS):
            seg = pl.multiple_of(starts[g], SEG_ALIGN)
            lo = 0
            for m in MOE_CALL_ROWS:
                @pl.when((n[g] > lo) & (n[g] <= m))
                def _(t=t, g=g, seg=seg, m=m):
                    experts(t, g, seg, m)

                lo = m

            @pl.when(n[g] > MOE_CALL_ROWS[-1])
            def _(t=t, g=g, n=n, starts=starts):
                def chunk(c, carry):
                    experts(t, g, pl.multiple_of(starts[g] + c * MOE_CHUNK, SEG_ALIGN), MOE_CHUNK)
                    return carry

                lax.fori_loop(0, (n[g] + MOE_CHUNK - 1) // MOE_CHUNK, chunk, 0)

    for t in range(nsub):
        back = lax.dot_general(p_ref[t], ys_ref[t, 0:rows, :], (((0,), (0,)), ((), ())),
                               preferred_element_type=F32)
        y = h_ref[t * tm:(t + 1) * tm, :] + back
        if final_norm:
            y = _rms(y, gf_ref[...])
        out_ref[t * tm:(t + 1) * tm, :] = y


def _moe_sparse(h, g, wr2, brt, tri, wg, wu, wd4, layer, gf, *, tm, nsub, final_norm):
    n, d = h.shape
    ne, _, de = wg.shape
    rows = tm + N_GROUPS * SEG_ALIGN
    over = rows + MOE_CHUNK
    resident = lambda shape: pl.BlockSpec(shape, lambda i: (0,) * len(shape), pipeline_mode=pl.Buffered(1))
    const = lambda shape: pl.BlockSpec(shape, lambda i: (0,) * len(shape))
    return pl.pallas_call(
        functools.partial(_moe_sparse_kernel, tm=tm, nsub=nsub, final_norm=final_norm),
        out_shape=jax.ShapeDtypeStruct((n, d), F32),
        grid=(n // (tm * nsub),),
        in_specs=[
            pl.BlockSpec((tm * nsub, d), lambda i: (i, 0)), const((1, d)),
            _layer_spec((2 * ROUTE_ROWS, d), layer), _layer_spec((ROUTE_ROWS, 1), layer),
            const((tm, tm)),
            resident((ne, d, de)), resident((ne, d, de)), resident((N_GROUPS, EXPERTS_PER_GROUP * de, d)),
            const((1, d)),
        ],
        out_specs=pl.BlockSpec((tm * nsub, d), lambda i: (i, 0)),
        scratch_shapes=[pltpu.VMEM((nsub, over, d), BF16), pltpu.VMEM((nsub, over, d), BF16),
                        pltpu.VMEM((nsub, rows, tm), BF16), pltpu.VMEM((nsub, over, SUBLANES), F32)],
        compiler_params=pltpu.CompilerParams(dimension_semantics=("arbitrary",),
                                             vmem_limit_bytes=MOE_VMEM_LIMIT),
        name="moe_sparse",
    )(h, g, wr2, brt, tri, wg, wu, wd4, gf)


MOE_TILE = 512
WIDE_TILE = 1024


def _prompt_tiles(nb, seq):
    moe = min(MOE_TILE, seq)
    wide = WIDE_TILE if seq % WIDE_TILE == 0 else moe
    return moe, wide, 2 if (nb * seq) % (2 * moe) == 0 else 1


def kernel(x_prompt, x_sample, state_pool, cache_swa_kv, cache_meta_kv, state_conv, meta_tokens, norm_mix, norm_ffn, norm_final, w_pool, pool_scale, w_qkv, w_o, attn_sinks, w_conv_in, conv_w, w_conv_out, w_group, b_group, w_expert_router, b_expert_router, w_gate, w_up, w_down):
    nb, seq, d = x_prompt.shape
    db, dseq, _ = x_sample.shape
    depth = norm_mix.shape[0]
    kvd = N_KV_HEADS * HEAD_DIM
    tm_main, tq_main, moe_sub = _prompt_tiles(nb, seq)
    tm_pool = tq_main
    halo = POOL_HALO
    assert seq % tm_main == 0 and tm_main >= 2 * WINDOW and tm_main % LANES == 0, (seq, tm_main)
    assert dseq >= POOL_HALO and dseq % SUBLANES == 0 and N_META == POOL_HALO, (dseq, N_META)

    row = lambda a: a.reshape(1, -1).astype(F32)
    rpad = lambda a, k: jnp.pad(a, ((0, 0), (0, k)) + ((0, 0),) * (a.ndim - 2))
    wrt = jnp.concatenate([rpad(jnp.swapaxes(w_group, 1, 2), SUBLANES - N_GROUPS),
                           rpad(jnp.swapaxes(w_expert_router, 1, 2), ROUTE_ROWS - SUBLANES - N_EXPERTS)], axis=1)
    wr2 = jnp.concatenate(_split_bf16(wrt, 2), axis=1)
    brt = jnp.concatenate([rpad(b_group, SUBLANES - N_GROUPS),
                           rpad(b_expert_router, ROUTE_ROWS - SUBLANES - N_EXPERTS)], axis=1)[..., None].astype(F32)
    wr = jnp.pad(jnp.concatenate([w_group, w_expert_router], axis=-1).astype(F32),
                 ((0, 0), (0, 0), (0, LANES - N_GROUPS - N_EXPERTS)))
    br = jnp.pad(jnp.concatenate([b_group, b_expert_router], axis=-1).astype(F32),
                 ((0, 0), (0, LANES - N_GROUPS - N_EXPERTS)))[:, None, :]
    tri = jnp.triu(jnp.ones((tm_main, tm_main), BF16))
    bf = lambda a: a.astype(BF16)
    w_pool_b, w_o_b = bf(w_pool), bf(w_o)
    w_conv_in_b, w_conv_out_b = bf(w_conv_in), bf(w_conv_out)

    hm = meta_tokens.astype(F32)[None]
    hp = x_prompt
    hs = x_sample
    pool_p, swa_p, meta_p, conv_p, pool_s, swa_s, conv_s = [], [], [], [], [], [], []
    for i in range(depth):
        j = i // N_MIXERS
        g = row(norm_mix[i])
        if i % N_MIXERS == 0:
            sc = row(pool_scale[j])
            hm, st_m = _pool_mix(hm, jnp.zeros((1, halo, d), F32), g, w_pool, j, sc, bb=1, tm=N_META,
                                 has_history=False, precise=True)
            hp, st_p = _pool_mix(hp, st_m, g, w_pool_b, j, sc, bb=1, tm=tm_pool, has_history=True,
                                 precise=False)
            hist_s = jnp.pad(state_pool[j].astype(F32), ((0, 0), (1, 0), (0, 0)))
            hs, st_s = _pool_mix(hs, hist_s, g, w_pool, j, sc, bb=db, tm=dseq, has_history=True,
                                 precise=True)
            pool_p.append(st_p[:, 1:])
            pool_s.append(st_s[:, 1:])
        elif i % N_MIXERS == 1:
            sinks = attn_sinks[j].astype(F32)
            qm, km, vm = _qkv_small(hm, g, w_qkv, j, jnp.arange(N_META))
            qt, kpad, vt, kst, vst = _qkv_t(hp, g, w_qkv[j], N_META + jnp.arange(seq), tm=tq_main)
            qs, ks, vs = _qkv_small(hs, g, w_qkv, j, PAST_LEN + N_META + jnp.arange(dseq))
            hm = _attn_small(qm, km, vm, sinks, hm, w_o, j)
            mkpad = jnp.pad(km[0].reshape(N_META, N_KV_HEADS, HEAD_DIM),
                            ((0, 0), (0, 0), (0, KPAD - HEAD_DIM))).reshape(N_META, -1).astype(BF16)
            hp = _attn_t(qt, kpad, vt, mkpad, vm[0].T.astype(BF16), sinks, hp, w_o_b, j, tq=tq_main)
            flat = lambda a: a.reshape(a.shape[0], a.shape[1], kvd)
            keys = jnp.concatenate([flat(cache_meta_kv[j][:, :, 0]), flat(cache_swa_kv[j][:, :, 0]), ks], axis=1)
            vals = jnp.concatenate([flat(cache_meta_kv[j][:, :, 1]), flat(cache_swa_kv[j][:, :, 1]), vs], axis=1)
            hs = _attn_small(qs, keys, vals, sinks, hs, w_o, j)
            heads = lambda a: a.reshape(a.shape[0], a.shape[1], N_KV_HEADS, HEAD_DIM)
            swa_p.append(jnp.stack([heads(kst), heads(vst)], axis=2))
            meta_kv = jnp.stack([heads(km), heads(vm)], axis=2)
            meta_p.append(jnp.broadcast_to(meta_kv, (nb,) + meta_kv.shape[1:]))
            swa_s.append(jnp.stack([heads(ks), heads(vs)], axis=2))
        else:
            cw = conv_w[j].astype(F32)
            hm, st_m = _conv_mix(hm, jnp.zeros((1, SUBLANES, d), F32), g, w_conv_in, j, cw, w_conv_out,
                                 bb=1, tm=N_META, precise=True)
            hp, st_p = _conv_mix(hp, st_m, g, w_conv_in_b, j, cw, w_conv_out_b, bb=1, tm=tq_main,
                                 precise=False)
            hist_s = jnp.pad(state_conv[j].astype(F32), ((0, 0), (SUBLANES - (CONV_WIDTH - 1), 0), (0, 0)))
            hs, st_s = _conv_mix(hs, hist_s, g, w_conv_in, j, cw, w_conv_out, bb=db, tm=dseq, precise=True)
            conv_p.append(st_p[:, SUBLANES - (CONV_WIDTH - 1):])
            conv_s.append(st_s[:, SUBLANES - (CONV_WIDTH - 1):])

        final = i == depth - 1
        gf = row(norm_final)
        gn = row(norm_ffn[i])
        small = jnp.concatenate([hm.reshape(-1, d), hs.reshape(-1, d)], axis=0)
        precise = any(k % N_MIXERS != 0 for k in range(i + 1, depth))
        small, wg_b, wu_b, wd_b = _moe_small(small, gn, wr, br, w_gate, w_up, w_down, i, gf,
                                             precise=precise, final_norm=final)
        hm = small[:N_META].reshape(1, N_META, d)
        hs = small[N_META:].reshape(db, dseq, d)
        hp = _moe_sparse(hp.reshape(-1, d), gn, wr2, brt, tri, wg_b, wu_b, wd_b.reshape(N_GROUPS, -1, d), i, gf,
                         tm=tm_main, nsub=moe_sub, final_norm=final).reshape(nb, seq, d)

    return (hp, hs, jnp.stack(pool_p), jnp.stack(swa_p), jnp.stack(meta_p), jnp.stack(conv_p),
            jnp.stack(pool_s), jnp.stack(swa_s), jnp.stack(conv_s))
```
